```python
import math
import jax, jax.numpy as jnp
from jax import lax
import numpy as np

D_MODEL = 1024
BATCH = 4
SEQ = 4096
DEPTH = 2
DEC_BATCH = 32
DEC_SEQ = 8
PAST_LEN = 8192
PAGE_SIZE = 128

D_MIX = D_MODEL
N_HEADS = 8
HEAD_DIM = 64
D_ATT = N_HEADS * HEAD_DIM
N_KV_HEADS = 2
GROUP = N_HEADS // N_KV_HEADS
N_IDX_HEADS = 4
IDX_DIM = 64
TOPK_MAX = 256
QBLOCK = 128
NUM_BUCKETS = 32
MAX_EXACT = NUM_BUCKETS // 2
MAX_DISTANCE = 128
D_REC = D_MIX // 4
N_REC_BLOCKS = 4
REC_BLK = D_REC // N_REC_BLOCKS
REC_CONV_W = 4
LRU_C = 8.0
D_GM = D_MIX // 4
N_GM_HEADS = 4
GM_HEAD_DIM = D_GM // N_GM_HEADS
CHUNK = 128
D_FF = 2816
FFN_CONV_W = 3
EPS = 1e-6

SPLITS = (D_ATT, N_KV_HEADS * HEAD_DIM, N_KV_HEADS * HEAD_DIM, N_IDX_HEADS * IDX_DIM, IDX_DIM,
          N_IDX_HEADS, D_REC, D_REC, D_GM, D_GM)
D_IN = D_ATT + 2 * N_KV_HEADS * HEAD_DIM + N_IDX_HEADS * IDX_DIM + IDX_DIM + N_IDX_HEADS + 2 * D_REC + 2 * D_GM

kernel_name = 'hybrid_dsa_rglru_gmlp_convffn_step'


def rmsnorm(x, g):
    xf = x.astype(jnp.float32)
    y = xf * lax.rsqrt(jnp.mean(xf * xf, axis=-1, keepdims=True) + EPS)
    return (y * g.astype(jnp.float32)).astype(x.dtype)


def layernorm(x, g, b):
    xf = x.astype(jnp.float32)
    mu = jnp.mean(xf, axis=-1, keepdims=True)
    xc = xf - mu
    var = jnp.mean(xc * xc, axis=-1, keepdims=True)
    y = xc * lax.rsqrt(var + EPS) * g.astype(jnp.float32) + b.astype(jnp.float32)
    return y.astype(x.dtype)


def split_proj(z):
    offs = []
    acc = 0
    for s in SPLITS[:-1]:
        acc += s
        offs.append(acc)
    return jnp.split(z, offs, axis=-1)


def causal_dwconv(x_ext, w, b):
    width = w.shape[0]
    t = x_ext.shape[1] - width + 1
    out = b
    for j in range(width):
        out = out + x_ext[:, j:j + t] * w[j]
    return out


def rel_bucket(dist):
    n = jnp.maximum(dist, 0)
    nf = jnp.maximum(n, 1).astype(jnp.float32)
    large = MAX_EXACT + (jnp.log(nf / MAX_EXACT) / math.log(MAX_DISTANCE / MAX_EXACT)
                         * (NUM_BUCKETS - MAX_EXACT)).astype(jnp.int32)
    large = jnp.minimum(large, NUM_BUCKETS - 1)
    return jnp.where(n < MAX_EXACT, n, large)


def dsa_attend(q, qi, wi, q_pos, ki_all, gather, topk, rel_bias):
    b, nq = q.shape[:2]
    l = ki_all.shape[1]
    s = jnp.einsum('bqhd,bld->bqhl', qi, ki_all).astype(jnp.float32)
    s = jnp.einsum('bqhl,bqh->bql', jax.nn.relu(s), wi.astype(jnp.float32))
    key_pos = jnp.arange(l, dtype=jnp.int32)
    s = jnp.where(key_pos[None, None, :] <= q_pos[None, :, None], s, -jnp.inf)
    _, idx = lax.top_k(s, topk)
    valid = idx <= q_pos[None, :, None]
    k_sel, v_sel = gather(idx)
    qg = q.reshape(b, nq, N_KV_HEADS, GROUP, HEAD_DIM)
    logits = jnp.einsum('bqgrd,bqkgd->bqgrk', qg, k_sel).astype(jnp.float32) * (HEAD_DIM ** -0.5)
    bias = rel_bias[rel_bucket(q_pos[None, :, None] - idx)]
    bias = jnp.moveaxis(bias, -1, 2).reshape(b, nq, N_KV_HEADS, GROUP, topk).astype(jnp.float32)
    logits = jnp.where(valid[:, :, None, None, :], logits + bias, -jnp.inf)
    p = jax.nn.softmax(logits, axis=-1).astype(v_sel.dtype)
    out = jnp.einsum('bqgrk,bqkgd->bqgrd', p, v_sel)
    return out.reshape(b, nq, D_ATT)


def prompt_attend(q, k, v, qi, ki, wi, rel_bias):
    b, s = q.shape[:2]
    nb = s // QBLOCK
    topk = min(TOPK_MAX, s // 4)
    take = jax.vmap(lambda a, i: a[i])

    def gather(idx):
        return take(k, idx), take(v, idx)

    def block(args):
        qb, qib, wib, posb = args
        return dsa_attend(qb, qib, wib, posb, ki, gather, topk, rel_bias)

    def to_blocks(a):
        return a.reshape((b, nb, QBLOCK) + a.shape[2:]).swapaxes(0, 1)

    pos = jnp.arange(s, dtype=jnp.int32).reshape(nb, QBLOCK)
    att = lax.map(block, (to_blocks(q), to_blocks(qi), to_blocks(wi), pos))
    return att.swapaxes(0, 1).reshape(b, s, D_ATT)


def sample_attend(q, k, v, qi, ki, wi, rel_bias, cache_k, cache_v, cache_kidx, page_table, layer):
    b, t = q.shape[:2]
    ki_past = cache_kidx[layer, page_table].reshape(b, PAST_LEN, IDX_DIM).astype(ki.dtype)
    ki_all = jnp.concatenate([ki_past, ki], axis=1)
    topk = min(TOPK_MAX, (PAST_LEN + t) // 4)
    take = jax.vmap(lambda a, i: a[i])

    def gather(idx):
        in_past = (idx < PAST_LEN)[..., None, None]
        pi = jnp.minimum(idx, PAST_LEN - 1)
        phys = jnp.take_along_axis(page_table, (pi // PAGE_SIZE).reshape(b, -1), axis=1).reshape(idx.shape)
        off = pi % PAGE_SIZE
        ni = jnp.clip(idx - PAST_LEN, 0, t - 1)
        k_sel = jnp.where(in_past, cache_k[layer, phys, off].astype(k.dtype), take(k, ni))
        v_sel = jnp.where(in_past, cache_v[layer, phys, off].astype(v.dtype), take(v, ni))
        return k_sel, v_sel

    pos = PAST_LEN + jnp.arange(t, dtype=jnp.int32)
    return dsa_attend(q, qi, wi, pos, ki_all, gather, topk, rel_bias)


def rg_lru(xc, h0, wa, ba, wx, bx, lam):
    b, t, _ = xc.shape
    xb = xc.reshape(b, t, N_REC_BLOCKS, REC_BLK)
    r = jax.nn.sigmoid(jnp.einsum('btnc,ncd->btnd', xb, wa).reshape(b, t, D_REC) + ba)
    i = jax.nn.sigmoid(jnp.einsum('btnc,ncd->btnd', xb, wx).reshape(b, t, D_REC) + bx)
    log_a = -LRU_C * r.astype(jnp.float32) * jax.nn.softplus(-lam.astype(jnp.float32))
    a = jnp.exp(log_a)
    u = jnp.sqrt(-jnp.expm1(2.0 * log_a)) * (i * xc).astype(jnp.float32)

    def step(h, au):
        a_t, u_t = au
        h = a_t * h + u_t
        return h, h

    h_last, hs = lax.scan(step, h0.astype(jnp.float32), (a.swapaxes(0, 1), u.swapaxes(0, 1)))
    return hs.swapaxes(0, 1).astype(xc.dtype), h_last.astype(xc.dtype)


def gmlp(gu, gv, ln_g, ln_b, ws, bs):
    b, t, _ = gu.shape
    c = min(CHUNK, t)
    u = jax.nn.gelu(gu)
    vn = layernorm(jax.nn.gelu(gv), ln_g, ln_b)
    w = (ws * jnp.tril(jnp.ones((CHUNK, CHUNK), ws.dtype)))[:, :c, :c]
    vb = vn.reshape(b, t // c, c, N_GM_HEADS, GM_HEAD_DIM)
    mix = jnp.einsum('gts,bnsgd->bntgd', w, vb) + bs[:, :c].T[None, None, :, :, None]
    return u * mix.reshape(b, t, D_GM), vn


def layer(x, p, attend, rec_hist, h0, ffn_hist):
    b, t, _ = x.shape
    h = rmsnorm(x, p['g_mix'])
    q, k, v, qi, ki, wi, rx, rg, gu, gv = split_proj(h @ p['w_in'])
    q = q.reshape(b, t, N_HEADS, HEAD_DIM)
    k = k.reshape(b, t, N_KV_HEADS, HEAD_DIM)
    v = v.reshape(b, t, N_KV_HEADS, HEAD_DIM)
    qi = qi.reshape(b, t, N_IDX_HEADS, IDX_DIM)
    att = attend(q, k, v, qi, ki, wi)
    rx_ext = jnp.concatenate([rec_hist.astype(rx.dtype), rx], axis=1)
    xc = causal_dwconv(rx_ext, p['rec_conv_w'], p['rec_conv_b'])
    hs, h_last = rg_lru(xc, h0, p['lru_wa'], p['lru_ba'], p['lru_wx'], p['lru_bx'], p['lru_lam'])
    rec = jax.nn.gelu(rg) * hs
    gm, vn = gmlp(gu, gv, p['gm_ln_g'], p['gm_ln_b'], p['gm_ws'], p['gm_bs'])
    x = x + jnp.concatenate([att, rec, gm], axis=-1) @ p['w_out']
    h2 = rmsnorm(x, p['g_ffn'])
    up = h2 @ p['w_up']
    up_ext = jnp.concatenate([ffn_hist.astype(up.dtype), up], axis=1)
    uc = causal_dwconv(up_ext, p['ffn_conv_w'], p['ffn_conv_b'])
    ga, gb = jnp.split(uc, 2, axis=-1)
    x = x + (jax.nn.gelu(ga) * gb) @ p['w_down']
    return x, (k, v, ki, h_last, rx_ext[:, -(REC_CONV_W - 1):], up_ext[:, -(FFN_CONV_W - 1):], vn)


def setup_inputs(seed: int = 0) -> dict:
    key = jax.random.key(seed)
    ks = iter(jax.random.split(key, 40))
    f32 = jnp.float32

    def nrm(shape, scale):
        return jax.random.normal(next(ks), shape, f32) * scale

    n_pages = PAST_LEN // PAGE_SIZE
    n_used = DEC_BATCH * n_pages
    n_pool = n_used + max(1, n_used // 4)
    page_table = jax.random.permutation(next(ks), n_pool)[:n_used].reshape(DEC_BATCH, n_pages).astype(jnp.int32)
    ua = jax.random.uniform(next(ks), (DEPTH, D_REC), f32, 0.9, 0.999)
    a = ua ** (1.0 / LRU_C)
    lru_lam = jnp.log(a) - jnp.log1p(-a)
    return {
        'x_prompt': nrm((BATCH, SEQ, D_MODEL), 1.0),
        'x_sample': nrm((DEC_BATCH, DEC_SEQ, D_MODEL), 1.0),
        'cache_k': nrm((DEPTH, n_pool, PAGE_SIZE, N_KV_HEADS, HEAD_DIM), 1.0),
        'cache_v': nrm((DEPTH, n_pool, PAGE_SIZE, N_KV_HEADS, HEAD_DIM), 1.0),
        'cache_kidx': nrm((DEPTH, n_pool, PAGE_SIZE, IDX_DIM), 1.0),
        'page_table': page_table,
        'state_lru_h': nrm((DEPTH, DEC_BATCH, D_REC), 0.5),
        'state_conv_rec': nrm((DEPTH, DEC_BATCH, REC_CONV_W - 1, D_REC), 1.0),
        'state_conv_ffn': nrm((DEPTH, DEC_BATCH, FFN_CONV_W - 1, 2 * D_FF), 1.0),
        'rel_bias': nrm((NUM_BUCKETS, N_HEADS), 0.5),
        'g_mix': 1.0 + nrm((DEPTH, D_MODEL), 0.01),
        'w_in': nrm((DEPTH, D_MODEL, D_IN), D_MODEL ** -0.5),
        'rec_conv_w': nrm((DEPTH, REC_CONV_W, D_REC), REC_CONV_W ** -0.5),
        'rec_conv_b': nrm((DEPTH, D_REC), 0.01),
        'lru_wa': nrm((DEPTH, N_REC_BLOCKS, REC_BLK, REC_BLK), REC_BLK ** -0.5),
        'lru_ba': nrm((DEPTH, D_REC), 0.01),
        'lru_wx': nrm((DEPTH, N_REC_BLOCKS, REC_BLK, REC_BLK), REC_BLK ** -0.5),
        'lru_bx': nrm((DEPTH, D_REC), 0.01),
        'lru_lam': lru_lam,
        'gm_ln_g': 1.0 + nrm((DEPTH, D_GM), 0.01),
        'gm_ln_b': nrm((DEPTH, D_GM), 0.01),
        'gm_ws': nrm((DEPTH, N_GM_HEADS, CHUNK, CHUNK), CHUNK ** -0.5),
        'gm_bs': 1.0 + nrm((DEPTH, N_GM_HEADS, CHUNK), 0.01),
        'w_out': nrm((DEPTH, D_MIX, D_MODEL), D_MIX ** -0.5),
        'g_ffn': 1.0 + nrm((DEPTH, D_MODEL), 0.01),
        'w_up': nrm((DEPTH, D_MODEL, 2 * D_FF), D_MODEL ** -0.5),
        'ffn_conv_w': nrm((DEPTH, FFN_CONV_W, 2 * D_FF), FFN_CONV_W ** -0.5),
        'ffn_conv_b': nrm((DEPTH, 2 * D_FF), 0.01),
        'w_down': nrm((DEPTH, D_FF, D_MODEL), D_FF ** -0.5),
        'g_final': 1.0 + nrm((D_MODEL,), 0.01),
    }


def reference(x_prompt, x_sample, cache_k, cache_v, cache_kidx, page_table, state_lru_h,
              state_conv_rec, state_conv_ffn, rel_bias, g_mix, w_in, rec_conv_w, rec_conv_b,
              lru_wa, lru_ba, lru_wx, lru_bx, lru_lam, gm_ln_g, gm_ln_b, gm_ws, gm_bs, w_out,
              g_ffn, w_up, ffn_conv_w, ffn_conv_b, w_down, g_final):
    xp, xs = x_prompt, x_sample
    bp, bs_ = x_prompt.shape[0], x_sample.shape[0]
    kp_l, vp_l, kip_l, hp_l, crp_l, cfp_l = [], [], [], [], [], []
    ks_l, vs_l, kis_l, hs_l, crs_l, cfs_l, gvs_l = [], [], [], [], [], [], []
    for l in range(DEPTH):
        p = {'g_mix': g_mix[l], 'w_in': w_in[l], 'rec_conv_w': rec_conv_w[l], 'rec_conv_b': rec_conv_b[l],
             'lru_wa': lru_wa[l], 'lru_ba': lru_ba[l], 'lru_wx': lru_wx[l], 'lru_bx': lru_bx[l],
             'lru_lam': lru_lam[l], 'gm_ln_g': gm_ln_g[l], 'gm_ln_b': gm_ln_b[l], 'gm_ws': gm_ws[l],
             'gm_bs': gm_bs[l], 'w_out': w_out[l], 'g_ffn': g_ffn[l], 'w_up': w_up[l],
             'ffn_conv_w': ffn_conv_w[l], 'ffn_conv_b': ffn_conv_b[l], 'w_down': w_down[l]}

        def attend_p(q, k, v, qi, ki, wi):
            return prompt_attend(q, k, v, qi, ki, wi, rel_bias)

        def attend_s(q, k, v, qi, ki, wi, layer_idx=l):
            return sample_attend(q, k, v, qi, ki, wi, rel_bias, cache_k, cache_v, cache_kidx,
                                 page_table, layer_idx)

        xp, (k1, v1, ki1, h1, cr1, cf1, _) = layer(
            xp, p, attend_p,
            jnp.zeros((bp, REC_CONV_W - 1, D_REC), xp.dtype),
            jnp.zeros((bp, D_REC), xp.dtype),
            jnp.zeros((bp, FFN_CONV_W - 1, 2 * D_FF), xp.dtype))
        xs, (k2, v2, ki2, h2, cr2, cf2, gv2) = layer(
            xs, p, attend_s, state_conv_rec[l], state_lru_h[l], state_conv_ffn[l])
        kp_l.append(k1); vp_l.append(v1); kip_l.append(ki1); hp_l.append(h1); crp_l.append(cr1); cfp_l.append(cf1)
        ks_l.append(k2); vs_l.append(v2); kis_l.append(ki2); hs_l.append(h2); crs_l.append(cr2); cfs_l.append(cf2)
        gvs_l.append(gv2)
    y_prompt = rmsnorm(xp, g_final)
    y_sample = rmsnorm(xs, g_final)
    new_k_prompt = jnp.stack(kp_l)
    new_v_prompt = jnp.stack(vp_l)
    new_kidx_prompt = jnp.stack(kip_l)
    new_lru_h_prompt = jnp.stack(hp_l)
    new_conv_rec_prompt = jnp.stack(crp_l)
    new_conv_ffn_prompt = jnp.stack(cfp_l)
    new_k_sample = jnp.stack(ks_l)
    new_v_sample = jnp.stack(vs_l)
    new_kidx_sample = jnp.stack(kis_l)
    new_lru_h_sample = jnp.stack(hs_l)
    new_conv_rec_sample = jnp.stack(crs_l)
    new_conv_ffn_sample = jnp.stack(cfs_l)
    new_gm_v_sample = jnp.stack(gvs_l)
    return (y_prompt, y_sample, new_k_prompt, new_v_prompt, new_kidx_prompt, new_lru_h_prompt,
            new_conv_rec_prompt, new_conv_ffn_prompt, new_k_sample, new_v_sample, new_kidx_sample,
            new_lru_h_sample, new_conv_rec_sample, new_conv_ffn_sample, new_gm_v_sample)
```

```python
import functools
import math

import jax
import jax.numpy as jnp
from jax import lax
from jax.experimental import pallas as pl
from jax.experimental.pallas import tpu as pltpu

F32 = jnp.float32
I32 = jnp.int32
MXU_DTYPE = jnp.bfloat16

D_MODEL = 1024
N_HEADS = 8
HEAD_DIM = 64
D_ATT = N_HEADS * HEAD_DIM
N_KV_HEADS = 2
GROUP = N_HEADS // N_KV_HEADS
D_KV = N_KV_HEADS * HEAD_DIM
N_IDX_HEADS = 4
IDX_DIM = 64
D_QI = N_IDX_HEADS * IDX_DIM
TOPK_MAX = 256
NUM_BUCKETS = 32
MAX_EXACT = NUM_BUCKETS // 2
MAX_DISTANCE = 128
D_REC = 256
N_REC_BLOCKS = 4
REC_CONV_W = 4
LRU_C = 8.0
D_GM = 256
N_GM_HEADS = 4
GM_HEAD_DIM = D_GM // N_GM_HEADS
CHUNK = 128
D_FF = 2816
FFN_CONV_W = 3
EPS = 1e-6
PAGE_SIZE = 128
Q_SCALE = HEAD_DIM ** -0.5

LANES = 128
SUBLANES = 8
VMEM_LIMIT_BYTES = 56 * 1024 * 1024

C_Q = 0
C_KV = C_Q + D_ATT
C_QI = C_KV + 2 * D_KV
C_REC = C_QI + D_QI
C_GM = C_REC + 2 * D_REC
C_KIWI = C_GM + 2 * D_GM
D_IN_PAD = C_KIWI + LANES

KEY_CHUNK = 256
Q_BLOCK = 128
INT_MIN = -2 ** 31
NEG_BIG = -1e30

BUCKET_LO = tuple(
    b if b <= MAX_EXACT else math.ceil(MAX_EXACT * (MAX_DISTANCE / MAX_EXACT) ** ((b - MAX_EXACT) / (NUM_BUCKETS - MAX_EXACT)))
    for b in range(NUM_BUCKETS))


def _cparams(*sem):
    return pltpu.CompilerParams(dimension_semantics=sem, vmem_limit_bytes=VMEM_LIMIT_BYTES)


def _const_spec(shape):
    nd = len(shape)
    return pl.BlockSpec(shape, lambda *_: (0,) * nd)


def _rmsnorm(x, g):
    return x * lax.rsqrt(jnp.mean(x * x, axis=-1, keepdims=True) + EPS) * g


def _gelu(x):
    return 0.5 * x * (1.0 + jnp.tanh(math.sqrt(2.0 / math.pi) * (x + 0.044715 * (x * x * x))))


def _sigmoid(x):
    return 1.0 / (1.0 + jnp.exp(-x))


def _mm(a, b):
    return jnp.dot(a, b, preferred_element_type=F32)


def _mm_nt(a, b):
    return lax.dot_general(a, b, (((1,), (1,)), ((), ())), preferred_element_type=F32)


def _sortable_key(s):
    bits = pltpu.bitcast(s, I32)
    key = bits ^ ((bits >> 31) & 0x7FFFFFFF)
    return jnp.where(s == 0.0, 0, key)


def _in_proj_body(x_ref, g_ref, w_ref, q_ref, k_ref, v_ref, ki_ref, kbf_ref, vt_ref, kibf_ref,
                  qi_ref, kiwi_ref, wit_ref, rec_ref, gm_ref):
    n_chunks = kbf_ref.shape[0]
    h = _rmsnorm(x_ref[...], g_ref[...]).astype(MXU_DTYPE)

    def proj(lo, hi):
        return _mm(h, w_ref[:, lo:hi])

    q_ref[...] = (proj(C_Q, C_KV) * Q_SCALE).astype(MXU_DTYPE)
    kv = proj(C_KV, C_QI)
    k = kv[:, :D_KV]
    v = kv[:, D_KV:]
    k_ref[...] = k
    v_ref[...] = v
    vt = v.T
    kb = k.astype(MXU_DTYPE)
    qi_ref[...] = proj(C_QI, C_REC).astype(MXU_DTYPE)
    rec_ref[...] = proj(C_REC, C_GM)
    gm_ref[...] = proj(C_GM, C_KIWI)
    kiwi = proj(C_KIWI, D_IN_PAD)
    ki = kiwi[:, :IDX_DIM]
    ki_ref[...] = ki
    kiwi_ref[...] = kiwi
    kib = ki.astype(MXU_DTYPE)
    wit_ref[...] = kiwi.T[IDX_DIM:IDX_DIM + SUBLANES, :]
    for c in range(n_chunks):
        rows = slice(c * KEY_CHUNK, (c + 1) * KEY_CHUNK)
        kbf_ref[c] = kb[rows]
        kibf_ref[c] = kib[rows]
        vt_ref[c] = vt[:, rows].astype(MXU_DTYPE)


def _in_proj(x, g, w, tm):
    m = x.shape[0]
    nc = tm // KEY_CHUNK
    row = lambda n: pl.BlockSpec((tm, n), lambda i: (i, 0))
    chunked = lambda a, b: pl.BlockSpec((nc, a, b), lambda i: (i, 0, 0))
    out_shape = (
        jax.ShapeDtypeStruct((m, D_ATT), MXU_DTYPE),
        jax.ShapeDtypeStruct((m, D_KV), F32),
        jax.ShapeDtypeStruct((m, D_KV), F32),
        jax.ShapeDtypeStruct((m, IDX_DIM), F32),
        jax.ShapeDtypeStruct((m // KEY_CHUNK, KEY_CHUNK, D_KV), MXU_DTYPE),
        jax.ShapeDtypeStruct((m // KEY_CHUNK, D_KV, KEY_CHUNK), MXU_DTYPE),
        jax.ShapeDtypeStruct((m // KEY_CHUNK, KEY_CHUNK, IDX_DIM), MXU_DTYPE),
        jax.ShapeDtypeStruct((m, D_QI), MXU_DTYPE),
        jax.ShapeDtypeStruct((m, LANES), F32),
        jax.ShapeDtypeStruct((SUBLANES, m), F32),
        jax.ShapeDtypeStruct((m, 2 * D_REC), F32),
        jax.ShapeDtypeStruct((m, 2 * D_GM), F32),
    )
    out_specs = (
        row(D_ATT), row(D_KV), row(D_KV), row(IDX_DIM),
        chunked(KEY_CHUNK, D_KV), chunked(D_KV, KEY_CHUNK), chunked(KEY_CHUNK, IDX_DIM),
        row(D_QI), row(LANES), pl.BlockSpec((SUBLANES, tm), lambda i: (0, i)),
        row(2 * D_REC), row(2 * D_GM),
    )
    return pl.pallas_call(
        _in_proj_body,
        grid=(m // tm,),
        in_specs=[row(D_MODEL), _const_spec((1, D_MODEL)), _const_spec((D_MODEL, D_IN_PAD))],
        out_specs=out_specs,
        out_shape=out_shape,
        compiler_params=_cparams("arbitrary"),
        name="in_proj",
    )(x, g, w)


def _bias_table(rb_ref, head, delta, shape):
    row = lax.broadcasted_iota(I32, shape, 0)
    lane = lax.broadcasted_iota(I32, shape, 1)
    d = delta + lane - row
    far = rb_ref[NUM_BUCKETS - 1, head]
    val = jnp.full(shape, rb_ref[0, head] - far, F32)
    for b in range(1, NUM_BUCKETS - 1):
        val = jnp.where(d >= BUCKET_LO[b], rb_ref[b, head] - far, val)
    return jnp.where(d >= BUCKET_LO[NUM_BUCKETS - 1], 0.0, val)


def _prompt_attn_body(rb_ref, q_ref, qi_ref, wit_ref, ki_ref, k_ref, vt_ref, att_ref,
                      key_s, mb_s, tab_s, tri_s, m_s, l_s, acc_s, qpad_s, *, topk):
    b = pl.program_id(0)
    i = pl.program_id(1)
    n_ch = (i + 2) // 2
    ck = (KEY_CHUNK, Q_BLOCK)

    @pl.when((b == 0) & (i == 0))
    def _init_tables():
        r = lax.broadcasted_iota(I32, (KEY_CHUNK, KEY_CHUNK), 0)
        c = lax.broadcasted_iota(I32, (KEY_CHUNK, KEY_CHUNK), 1)
        tri_s[...] = jnp.where(c <= r, 1.0, 0.0).astype(MXU_DTYPE)

        def per_head(h, carry):
            for ti in range(4):
                tab_s[ti, h] = _bias_table(rb_ref, h, ti * Q_BLOCK, ck)
            return carry

        lax.fori_loop(0, N_HEADS, per_head, 0)

    row = lax.broadcasted_iota(I32, ck, 0)
    lane = lax.broadcasted_iota(I32, ck, 1)
    q_pos = i * Q_BLOCK + lane

    qi = qi_ref[...]
    qis = jnp.concatenate([qi[:, h * IDX_DIM:(h + 1) * IDX_DIM] for h in range(N_IDX_HEADS)], axis=0)
    wit = wit_ref[...]
    w_row = jnp.concatenate([wit[h:h + 1, :] for h in range(N_IDX_HEADS)], axis=1)

    def score_chunk(c, carry):
        s4 = jnp.maximum(_mm_nt(ki_ref[c], qis), 0.0) * w_row
        s = s4[:, 0:Q_BLOCK]
        for h in range(1, N_IDX_HEADS):
            s = s + s4[:, h * Q_BLOCK:(h + 1) * Q_BLOCK]
        valid = (c * KEY_CHUNK + row) <= q_pos
        key_s[c] = jnp.where(valid, _sortable_key(s), INT_MIN)
        return carry

    lax.fori_loop(0, n_ch, score_chunk, 0)

    def count(pred_fn):
        def body(c, acc):
            return acc + jnp.sum(jnp.where(pred_fn(key_s[c]), 1.0, 0.0), axis=0, keepdims=True)
        return lax.fori_loop(0, n_ch, body, jnp.zeros((1, Q_BLOCK), F32))

    def search_step(step, t):
        cand = t + lax.shift_left(jnp.int32(1), 31 - step)
        cnt = count(lambda k: k >= cand)
        return jnp.where(cnt >= topk, cand, t)

    t = lax.fori_loop(0, 32, search_step, jnp.full((1, Q_BLOCK), INT_MIN, I32))
    need = topk - count(lambda k: k > t)

    tri = tri_s[...]

    def mask_chunk(c, carry):
        key = key_s[c]
        tie = (key == t) & (key > INT_MIN)
        tie_f = jnp.where(tie, 1.0, 0.0)
        incl = _mm(tri, tie_f.astype(MXU_DTYPE))
        rank = carry + incl - tie_f
        sel = (key > t) | (tie & (rank < need))
        mb_s[c] = jnp.where(sel, 0.0, -jnp.inf)
        return carry + incl[KEY_CHUNK - 1:KEY_CHUNK, :]

    lax.fori_loop(0, n_ch, mask_chunk, jnp.zeros((1, Q_BLOCK), F32))

    q = q_ref[...]
    zeros = jnp.zeros((Q_BLOCK, HEAD_DIM), MXU_DTYPE)
    for h in range(N_HEADS):
        qh = q[:, h * HEAD_DIM:(h + 1) * HEAD_DIM]
        parts = [qh, zeros] if h < GROUP else [zeros, qh]
        qpad_s[h * Q_BLOCK:(h + 1) * Q_BLOCK, :] = jnp.concatenate(parts, axis=1)
    m_s[...] = jnp.full(m_s.shape, NEG_BIG, F32)
    l_s[...] = jnp.zeros(l_s.shape, F32)
    acc_s[...] = jnp.zeros(acc_s.shape, F32)

    def attend(c, tab_idx):
        logits = _mm_nt(k_ref[c], qpad_s[...])
        mb = mb_s[c]
        for g in range(N_KV_HEADS):
            ps, alphas = [], []
            for hh in range(GROUP):
                h = g * GROUP + hh
                cols = slice(h * Q_BLOCK, (h + 1) * Q_BLOCK)
                lt = logits[:, cols] + mb
                if tab_idx is not None:
                    lt = lt + tab_s[tab_idx, h]
                m_old = m_s[0:1, cols]
                m_new = jnp.maximum(m_old, jnp.max(lt, axis=0, keepdims=True))
                alpha = jnp.exp(m_old - m_new)
                p = jnp.exp(lt - m_new)
                l_s[0:1, cols] = alpha * l_s[0:1, cols] + jnp.sum(p, axis=0, keepdims=True)
                m_s[0:1, cols] = m_new
                ps.append(p.astype(MXU_DTYPE))
                alphas.append(alpha)
            pg = jnp.concatenate(ps, axis=1)
            ag = jnp.concatenate(alphas, axis=1)
            vt = vt_ref[c, g * HEAD_DIM:(g + 1) * HEAD_DIM, :]
            acc_s[g] = acc_s[g] * ag + _mm(vt, pg)

    def far_chunk(c, carry):
        attend(c, None)
        return carry

    lax.fori_loop(0, jnp.maximum(n_ch - 2, 0), far_chunk, 0)
    parity = i % 2

    @pl.when(n_ch >= 2)
    def _second_last():
        attend(n_ch - 2, 2 + parity)

    attend(n_ch - 1, parity)

    for h in range(N_HEADS):
        g, hh = divmod(h, GROUP)
        cols = slice(hh * Q_BLOCK, (hh + 1) * Q_BLOCK)
        o = acc_s[g][:, cols] / l_s[0:1, h * Q_BLOCK:(h + 1) * Q_BLOCK]
        att_ref[:, h * HEAD_DIM:(h + 1) * HEAD_DIM] = o.T.astype(att_ref.dtype)


def _prompt_attention(rel_bias, q, qi, wit, kibf, kbf, vtbf, batch, seq):
    nq = seq // Q_BLOCK
    nc = seq // KEY_CHUNK
    topk = float(min(TOPK_MAX, seq // 4))
    qrow = lambda n: pl.BlockSpec((Q_BLOCK, n), lambda b, i: (b * nq + i, 0))
    per_batch = lambda a, c: pl.BlockSpec((nc, a, c), lambda b, i: (b, 0, 0))
    return pl.pallas_call(
        functools.partial(_prompt_attn_body, topk=topk),
        grid=(batch, nq),
        in_specs=[
            pl.BlockSpec(memory_space=pltpu.SMEM),
            qrow(D_ATT), qrow(D_QI),
            pl.BlockSpec((SUBLANES, Q_BLOCK), lambda b, i: (0, b * nq + i)),
            per_batch(KEY_CHUNK, IDX_DIM), per_batch(KEY_CHUNK, D_KV), per_batch(D_KV, KEY_CHUNK),
        ],
        out_specs=qrow(D_ATT),
        out_shape=jax.ShapeDtypeStruct((batch * seq, D_ATT), MXU_DTYPE),
        scratch_shapes=[
            pltpu.VMEM((nc, KEY_CHUNK, Q_BLOCK), I32),
            pltpu.VMEM((nc, KEY_CHUNK, Q_BLOCK), F32),
            pltpu.VMEM((4, N_HEADS, KEY_CHUNK, Q_BLOCK), F32),
            pltpu.VMEM((KEY_CHUNK, KEY_CHUNK), MXU_DTYPE),
            pltpu.VMEM((SUBLANES, N_HEADS * Q_BLOCK), F32),
            pltpu.VMEM((SUBLANES, N_HEADS * Q_BLOCK), F32),
            pltpu.VMEM((N_KV_HEADS, HEAD_DIM, GROUP * Q_BLOCK), F32),
            pltpu.VMEM((N_HEADS * Q_BLOCK, D_KV), MXU_DTYPE),
        ],
        compiler_params=_cparams("arbitrary", "arbitrary"),
        name="prompt_attn",
    )(rel_bias, q, qi, wit, kibf, kbf, vtbf)


def _sample_attn_body(pt_ref, rb_ref, q_ref, qi_ref, kiwi_ref, knew_ref, vnew_ref,
                      ck_hbm, cv_hbm, cki_hbm, att_ref,
                      kbuf, vbuf, kibuf, sem, *, topk, n_pages, layer, t_new):
    b = pl.program_id(0)
    past = n_pages * PAGE_SIZE

    def page_copies(p):
        phys = pt_ref[b, p]
        rows = pl.ds(p * PAGE_SIZE, PAGE_SIZE)
        return (pltpu.make_async_copy(ck_hbm.at[layer, phys], kbuf.at[rows], sem.at[0]),
                pltpu.make_async_copy(cv_hbm.at[layer, phys], vbuf.at[rows], sem.at[1]),
                pltpu.make_async_copy(cki_hbm.at[layer, phys], kibuf.at[rows], sem.at[2]))

    def start_page(p, carry):
        for cp in page_copies(p):
            cp.start()
        return carry

    def wait_page(p, carry):
        for cp in page_copies(p):
            cp.wait()
        return carry

    lax.fori_loop(0, n_pages, start_page, 0)

    qi = qi_ref[...]
    qis = jnp.concatenate([qi[:, h * IDX_DIM:(h + 1) * IDX_DIM] for h in range(N_IDX_HEADS)], axis=0)
    kiwi = kiwi_ref[...]
    w_col = jnp.concatenate([kiwi[:, IDX_DIM + h:IDX_DIM + h + 1] for h in range(N_IDX_HEADS)], axis=0)
    q = q_ref[...]
    zeros = jnp.zeros((t_new, HEAD_DIM), MXU_DTYPE)
    qpad = jnp.concatenate(
        [jnp.concatenate([q[:, h * HEAD_DIM:(h + 1) * HEAD_DIM], zeros] if h < GROUP else
                         [zeros, q[:, h * HEAD_DIM:(h + 1) * HEAD_DIM]], axis=1)
         for h in range(N_HEADS)], axis=0)

    lax.fori_loop(0, n_pages, wait_page, 0)

    def idx_score(keys_bf):
        s4 = jnp.maximum(_mm_nt(qis, keys_bf), 0.0) * w_col
        s = s4[0:t_new]
        for h in range(1, N_IDX_HEADS):
            s = s + s4[h * t_new:(h + 1) * t_new]
        return s

    key_p = _sortable_key(idx_score(kibuf[...].astype(MXU_DTYPE)))
    ki_new = kiwi[:, :IDX_DIM].astype(MXU_DTYPE)
    ki_new = jnp.concatenate([ki_new, jnp.zeros((LANES - t_new, IDX_DIM), MXU_DTYPE)], axis=0)
    rown = lax.broadcasted_iota(I32, (t_new, LANES), 0)
    lanen = lax.broadcasted_iota(I32, (t_new, LANES), 1)
    valid_n = lanen <= rown
    key_n = jnp.where(valid_n, _sortable_key(idx_score(ki_new)), INT_MIN)

    def count(pred_fn):
        return (jnp.sum(jnp.where(pred_fn(key_p), 1.0, 0.0), axis=1, keepdims=True)
                + jnp.sum(jnp.where(pred_fn(key_n), 1.0, 0.0), axis=1, keepdims=True))

    def search_step(step, t):
        cand = t + lax.shift_left(jnp.int32(1), 31 - step)
        return jnp.where(count(lambda k: k >= cand) >= topk, cand, t)

    t = lax.fori_loop(0, 32, search_step, jnp.full((t_new, 1), INT_MIN, I32))
    need = topk - count(lambda k: k > t)

    r = lax.broadcasted_iota(I32, (KEY_CHUNK, KEY_CHUNK), 0)
    c = lax.broadcasted_iota(I32, (KEY_CHUNK, KEY_CHUNK), 1)
    tri = jnp.where(r <= c, 1.0, 0.0).astype(MXU_DTYPE)
    tie_p = key_p == t
    carry = jnp.zeros((t_new, 1), F32)
    mask_parts = []
    for ch in range(past // KEY_CHUNK):
        cols = slice(ch * KEY_CHUNK, (ch + 1) * KEY_CHUNK)
        tie_c = tie_p[:, cols]
        tie_f = jnp.where(tie_c, 1.0, 0.0)
        incl = _mm(tie_f.astype(MXU_DTYPE), tri)
        sel = (key_p[:, cols] > t) | (tie_c & ((carry + incl - tie_f) < need))
        mask_parts.append(jnp.where(sel, 0.0, -jnp.inf))
        carry = carry + incl[:, KEY_CHUNK - 1:KEY_CHUNK]
    mb_p = jnp.concatenate(mask_parts, axis=1)
    tie_n = (key_n == t) & valid_n
    tie_nf = jnp.where(tie_n, 1.0, 0.0)
    incl_n = _mm(tie_nf.astype(MXU_DTYPE), tri[:LANES, :LANES])
    sel_n = (key_n > t) | (tie_n & ((carry + incl_n - tie_nf) < need))
    mb_n = jnp.where(sel_n & valid_n, 0.0, -jnp.inf)

    kb = kbuf[...].astype(MXU_DTYPE)
    lg_p = _mm_nt(qpad, kb)
    k_new = jnp.concatenate([knew_ref[...].astype(MXU_DTYPE),
                             jnp.zeros((LANES - t_new, D_KV), MXU_DTYPE)], axis=0)
    lg_n = _mm_nt(qpad, k_new)
    near = slice(past - LANES, past)
    lp_rows, ln_rows = [], []
    for h in range(N_HEADS):
        rows = slice(h * t_new, (h + 1) * t_new)
        far = rb_ref[NUM_BUCKETS - 1, h]

        def bias(d):
            val = jnp.full(d.shape, rb_ref[0, h] - far, F32)
            for bk in range(1, NUM_BUCKETS - 1):
                val = jnp.where(d >= BUCKET_LO[bk], rb_ref[bk, h] - far, val)
            return jnp.where(d >= BUCKET_LO[NUM_BUCKETS - 1], 0.0, val)

        lp = lg_p[rows] + mb_p
        lp_near = lp[:, near] + bias(rown + (LANES - lanen))
        lp_rows.append(jnp.concatenate([lp[:, :past - LANES], lp_near], axis=1))
        ln_rows.append(lg_n[rows] + mb_n + bias(rown - lanen))
    lp_all = jnp.concatenate(lp_rows, axis=0)
    ln_all = jnp.concatenate(ln_rows, axis=0)
    m = jnp.maximum(jnp.max(lp_all, axis=1, keepdims=True), jnp.max(ln_all, axis=1, keepdims=True))
    p_p = jnp.exp(lp_all - m)
    p_n = jnp.exp(ln_all - m)
    denom = jnp.sum(p_p, axis=1, keepdims=True) + jnp.sum(p_n, axis=1, keepdims=True)
    v_new = jnp.concatenate([vnew_ref[...].astype(MXU_DTYPE),
                             jnp.zeros((LANES - t_new, D_KV), MXU_DTYPE)], axis=0)
    o = (_mm(p_p.astype(MXU_DTYPE), vbuf[...].astype(MXU_DTYPE))
         + _mm(p_n.astype(MXU_DTYPE), v_new)) / denom
    for h in range(N_HEADS):
        g = h // GROUP
        att_ref[:, h * HEAD_DIM:(h + 1) * HEAD_DIM] = (
            o[h * t_new:(h + 1) * t_new, g * HEAD_DIM:(g + 1) * HEAD_DIM].astype(att_ref.dtype))


def _sample_attention(page_table, rel_bias, q, qi, kiwi, k_new, v_new, cache_k, cache_v, cache_kidx,
                      layer, batch, t_new):
    n_pages = page_table.shape[1]
    past = n_pages * PAGE_SIZE
    topk = float(min(TOPK_MAX, (past + t_new) // 4))
    depth, n_pool = cache_k.shape[:2]
    ck = cache_k.reshape(depth, n_pool, PAGE_SIZE, D_KV)
    cv = cache_v.reshape(depth, n_pool, PAGE_SIZE, D_KV)
    qrow = lambda n: pl.BlockSpec((t_new, n), lambda b, pt: (b, 0))
    hbm = pl.BlockSpec(memory_space=pl.ANY)
    grid_spec = pltpu.PrefetchScalarGridSpec(
        num_scalar_prefetch=1,
        grid=(batch,),
        in_specs=[pl.BlockSpec(memory_space=pltpu.SMEM),
                  qrow(D_ATT), qrow(D_QI), qrow(LANES), qrow(D_KV), qrow(D_KV), hbm, hbm, hbm],
        out_specs=qrow(D_ATT),
        scratch_shapes=[
            pltpu.VMEM((past, D_KV), F32),
            pltpu.VMEM((past, D_KV), F32),
            pltpu.VMEM((past, IDX_DIM), F32),
            pltpu.SemaphoreType.DMA((3,)),
        ],
    )
    return pl.pallas_call(
        functools.partial(_sample_attn_body, topk=topk, n_pages=n_pages, layer=layer, t_new=t_new),
        grid_spec=grid_spec,
        out_shape=jax.ShapeDtypeStruct((batch * t_new, D_ATT), MXU_DTYPE),
        compiler_params=_cparams("arbitrary"),
        name="sample_attn",
    )(page_table, rel_bias, q, qi, kiwi, k_new, v_new, ck, cv, cache_kidx)


def _rec_body(rec_ref, hist_ref, h0_ref, cw_ref, cb_ref, wa_ref, ba_ref, wx_ref, bx_ref, lam_ref,
              out_ref, hlast_ref, cstate_ref, hcar, xprev):
    tt = rec_ref.shape[0]

    @pl.when(pl.program_id(1) == 0)
    def _load_state():
        hcar[...] = h0_ref[0]
        xprev[...] = hist_ref[0]

    rx = rec_ref[:, :D_REC]
    rg = rec_ref[:, D_REC:]
    ext = jnp.concatenate([xprev[...], rx], axis=0)
    xc = cb_ref[...]
    for j in range(REC_CONV_W - 1):
        xc = xc + pltpu.roll(ext, REC_CONV_W - 1 - j, 0)[SUBLANES:] * cw_ref[j:j + 1, :]
    xc = xc + rx * cw_ref[REC_CONV_W - 1:REC_CONV_W, :]
    xcb = xc.astype(MXU_DTYPE)
    r = _sigmoid(_mm(xcb, wa_ref[...]) + ba_ref[...])
    gi = _sigmoid(_mm(xcb, wx_ref[...]) + bx_ref[...])
    nl = -lam_ref[...]
    softplus = jnp.maximum(nl, 0.0) + jnp.log1p(jnp.exp(-jnp.abs(nl)))
    log_a = -LRU_C * r * softplus
    a = jnp.exp(log_a)
    u = jnp.sqrt(-jnp.tanh(log_a) * (a * a + 1.0)) * (gi * xc)
    row = lax.broadcasted_iota(I32, (tt, D_REC), 0)
    s = 1
    while s < tt:
        keep = row >= s
        u = jnp.where(keep, u + a * pltpu.roll(u, s, 0), u)
        a = jnp.where(keep, a * pltpu.roll(a, s, 0), a)
        s *= 2
    hs = u + a * hcar[0:1, :]
    hcar[...] = jnp.broadcast_to(hs[tt - 1:tt, :], hcar.shape)
    xprev[...] = rx[tt - SUBLANES:, :]
    out_ref[...] = (_gelu(rg) * hs).astype(out_ref.dtype)
    hlast_ref[0] = hs[tt - SUBLANES:, :]
    cstate_ref[0] = rx[tt - SUBLANES:, :]


def _rec_branch(rec, hist8, h08, cw, cb, wa, ba, wx, bx, lam, batch, t_len, tt):
    nt = t_len // tt
    state = pl.BlockSpec((1, SUBLANES, D_REC), lambda b, t: (b, 0, 0))
    vec = _const_spec((1, D_REC))
    return pl.pallas_call(
        _rec_body,
        grid=(batch, nt),
        in_specs=[pl.BlockSpec((tt, 2 * D_REC), lambda b, t: (b * nt + t, 0)), state, state,
                  _const_spec((REC_CONV_W, D_REC)), vec, _const_spec((D_REC, D_REC)), vec,
                  _const_spec((D_REC, D_REC)), vec, vec],
        out_specs=(pl.BlockSpec((tt, D_REC), lambda b, t: (b * nt + t, 0)), state, state),
        out_shape=(jax.ShapeDtypeStruct((batch * t_len, D_REC), MXU_DTYPE),
                   jax.ShapeDtypeStruct((batch, SUBLANES, D_REC), F32),
                   jax.ShapeDtypeStruct((batch, SUBLANES, D_REC), F32)),
        scratch_shapes=[pltpu.VMEM((SUBLANES, D_REC), F32), pltpu.VMEM((SUBLANES, D_REC), F32)],
        compiler_params=_cparams("arbitrary", "arbitrary"),
        name="rec_branch",
    )(rec, hist8, h08, cw, cb, wa, ba, wx, bx, lam)


def _gmlp_body(gm_ref, wmix_ref, bias_ref, lng_ref, lnb_ref, out_ref, vn_ref):
    rows = wmix_ref.shape[1]
    n_sub = gm_ref.shape[0] // rows
    head_of_lane = lax.broadcasted_iota(I32, (rows, D_GM), 1) // GM_HEAD_DIM
    for sb in range(n_sub):
        sl = slice(sb * rows, (sb + 1) * rows)
        u = _gelu(gm_ref[sl, :D_GM])
        gv = _gelu(gm_ref[sl, D_GM:])
        xc = gv - jnp.mean(gv, axis=-1, keepdims=True)
        var = jnp.mean(xc * xc, axis=-1, keepdims=True)
        vn = xc * lax.rsqrt(var + EPS) * lng_ref[...] + lnb_ref[...]
        vn_ref[sl, :] = vn
        vnb = vn.astype(MXU_DTYPE)
        mix = jnp.zeros((rows, D_GM), F32)
        for g in range(N_GM_HEADS):
            mix = jnp.where(head_of_lane == g, _mm(wmix_ref[g], vnb), mix)
        out_ref[sl, :] = (u * (mix + bias_ref[...])).astype(out_ref.dtype)


def _gmlp(gm, wmix, bias, lng, lnb, tm):
    m = gm.shape[0]
    rows = wmix.shape[1]
    return pl.pallas_call(
        _gmlp_body,
        grid=(m // tm,),
        in_specs=[pl.BlockSpec((tm, 2 * D_GM), lambda i: (i, 0)),
                  _const_spec((N_GM_HEADS, rows, rows)), _const_spec((rows, D_GM)),
                  _const_spec((1, D_GM)), _const_spec((1, D_GM))],
        out_specs=(pl.BlockSpec((tm, D_GM), lambda i: (i, 0)), pl.BlockSpec((tm, D_GM), lambda i: (i, 0))),
        out_shape=(jax.ShapeDtypeStruct((m, D_GM), MXU_DTYPE), jax.ShapeDtypeStruct((m, D_GM), F32)),
        compiler_params=_cparams("arbitrary"),
        name="gmlp",
    )(gm, wmix, bias, lng, lnb)


def _out_proj_body(x_ref, att_ref, rec_ref, gm_ref, w_ref, g_ref, x1_ref, h2_ref):
    y = (_mm(att_ref[...], w_ref[0:D_ATT, :])
         + _mm(rec_ref[...], w_ref[D_ATT:D_ATT + D_REC, :])
         + _mm(gm_ref[...], w_ref[D_ATT + D_REC:, :]))
    x1 = x_ref[...] + y
    x1_ref[...] = x1
    h2_ref[...] = _rmsnorm(x1, g_ref[...]).astype(h2_ref.dtype)


def _out_proj(x, att, rec, gm, w, g, tm):
    m = x.shape[0]
    row = lambda n: pl.BlockSpec((tm, n), lambda i: (i, 0))
    return pl.pallas_call(
        _out_proj_body,
        grid=(m // tm,),
        in_specs=[row(D_MODEL), row(D_ATT), row(D_REC), row(D_GM),
                  _const_spec((D_MODEL, D_MODEL)), _const_spec((1, D_MODEL))],
        out_specs=(row(D_MODEL), row(D_MODEL)),
        out_shape=(jax.ShapeDtypeStruct((m, D_MODEL), F32), jax.ShapeDtypeStruct((m, D_MODEL), MXU_DTYPE)),
        compiler_params=_cparams("arbitrary"),
        name="out_proj",
    )(x, att, rec, gm, w, g)


FF_TILE = 256
HALO = 16


def _ffn_finish(x1, acc, gf_ref, out_ref, final):
    x2 = x1 + acc
    out_ref[...] = _rmsnorm(x2, gf_ref[...]) if final else x2


def _ffn_prompt_body(h_ref, halo_ref, x1_ref, hist_ref, wup_ref, cw_ref, cb_ref, wdn_ref, gf_ref,
                     out_ref, upst_ref, *, tiles_per_seq, final):
    tm = h_ref.shape[0]
    first = (pl.program_id(0) % tiles_per_seq) == 0
    hm = h_ref[...]
    hh = halo_ref[...]
    acc = jnp.zeros((tm, D_MODEL), F32)
    for j in range(D_FF // FF_TILE):
        halves = []
        for base in (0, D_FF):
            cols = slice(base + j * FF_TILE, base + (j + 1) * FF_TILE)
            up_m = _mm(hm, wup_ref[:, cols])
            up_h = jnp.where(first, hist_ref[0, :, cols], _mm(hh, wup_ref[:, cols]))
            upst_ref[0, :, cols] = up_m[tm - HALO:, :]
            ext = jnp.concatenate([up_h, up_m], axis=0)
            uc = cb_ref[:, cols]
            for jj in range(FFN_CONV_W - 1):
                uc = uc + pltpu.roll(ext, FFN_CONV_W - 1 - jj, 0)[HALO:] * cw_ref[jj:jj + 1, cols]
            halves.append(uc + up_m * cw_ref[FFN_CONV_W - 1:FFN_CONV_W, cols])
        act = (_gelu(halves[0]) * halves[1]).astype(MXU_DTYPE)
        acc = acc + _mm(act, wdn_ref[j * FF_TILE:(j + 1) * FF_TILE, :])
    _ffn_finish(x1_ref[...], acc, gf_ref, out_ref, final)


def _ffn_prompt(h2, x1, hist16, wup, cw, cb, wdn, gf, batch, seq, tm, final):
    m = h2.shape[0]
    tps = seq // tm
    row = lambda n: pl.BlockSpec((tm, n), lambda i: (i, 0))
    state = pl.BlockSpec((1, HALO, 2 * D_FF), lambda i: (i // tps, 0, 0))
    return pl.pallas_call(
        functools.partial(_ffn_prompt_body, tiles_per_seq=tps, final=final),
        grid=(m // tm,),
        in_specs=[row(D_MODEL),
                  pl.BlockSpec((HALO, D_MODEL), lambda i: (jnp.maximum(i * (tm // HALO) - 1, 0), 0)),
                  row(D_MODEL), state,
                  _const_spec((D_MODEL, 2 * D_FF)), _const_spec((FFN_CONV_W, 2 * D_FF)),
                  _const_spec((1, 2 * D_FF)), _const_spec((D_FF, D_MODEL)), _const_spec((1, D_MODEL))],
        out_specs=(row(D_MODEL), state),
        out_shape=(jax.ShapeDtypeStruct((m, D_MODEL), F32),
                   jax.ShapeDtypeStruct((batch, HALO, 2 * D_FF), F32)),
        compiler_params=_cparams("arbitrary"),
        name="ffn_prompt",
    )(h2, h2, x1, hist16, wup, cw, cb, wdn, gf)


def _ffn_sample_body(h_ref, x1_ref, p1_ref, p2_ref, wup_ref, cw_ref, cb_ref, wdn_ref, gf_ref,
                     out_ref, up_ref, *, t_new, final):
    tm = h_ref.shape[0]
    hm = h_ref[...]
    pos = lax.broadcasted_iota(I32, (tm, FF_TILE), 0) % t_new
    acc = jnp.zeros((tm, D_MODEL), F32)
    for j in range(D_FF // FF_TILE):
        halves = []
        for base in (0, D_FF):
            cols = slice(base + j * FF_TILE, base + (j + 1) * FF_TILE)
            up = _mm(hm, wup_ref[:, cols])
            up_ref[:, cols] = up
            tap2 = jnp.where(pos < 2, p2_ref[:, cols], pltpu.roll(up, 2, 0))
            tap1 = jnp.where(pos < 1, p1_ref[:, cols], pltpu.roll(up, 1, 0))
            halves.append(cb_ref[:, cols] + tap2 * cw_ref[0:1, cols] + tap1 * cw_ref[1:2, cols]
                          + up * cw_ref[2:3, cols])
        act = (_gelu(halves[0]) * halves[1]).astype(MXU_DTYPE)
        acc = acc + _mm(act, wdn_ref[j * FF_TILE:(j + 1) * FF_TILE, :])
    _ffn_finish(x1_ref[...], acc, gf_ref, out_ref, final)


def _ffn_sample(h2, x1, p1, p2, wup, cw, cb, wdn, gf, t_new, final):
    m = h2.shape[0]
    full = lambda n: _const_spec((m, n))
    return pl.pallas_call(
        functools.partial(_ffn_sample_body, t_new=t_new, final=final),
        grid=(1,),
        in_specs=[full(D_MODEL), full(D_MODEL), full(2 * D_FF), full(2 * D_FF),
                  _const_spec((D_MODEL, 2 * D_FF)), _const_spec((FFN_CONV_W, 2 * D_FF)),
                  _const_spec((1, 2 * D_FF)), _const_spec((D_FF, D_MODEL)), _const_spec((1, D_MODEL))],
        out_specs=(full(D_MODEL), full(2 * D_FF)),
        out_shape=(jax.ShapeDtypeStruct((m, D_MODEL), F32), jax.ShapeDtypeStruct((m, 2 * D_FF), F32)),
        compiler_params=_cparams("arbitrary"),
        name="ffn_sample",
    )(h2, x1, p1, p2, wup, cw, cb, wdn, gf)


def _block_diag(w):
    n, blk, _ = w.shape
    out = jnp.zeros((n * blk, n * blk), w.dtype)
    for i in range(n):
        out = out.at[i * blk:(i + 1) * blk, i * blk:(i + 1) * blk].set(w[i])
    return out


def _pad_rows_front(a, rows):
    pad = rows - a.shape[1]
    return jnp.pad(a, ((0, 0), (pad, 0), (0, 0)))


def _layer_weights(l, g_mix, w_in, rec_conv_w, rec_conv_b, lru_wa, lru_ba, lru_wx, lru_bx, lru_lam,
                   gm_ln_g, gm_ln_b, gm_ws, gm_bs, w_out, g_ffn, w_up, ffn_conv_w, ffn_conv_b, w_down):
    w = w_in[l]
    c_ki = D_ATT + 2 * D_KV + D_QI
    c_rx = c_ki + IDX_DIM + N_IDX_HEADS
    w_r = jnp.concatenate([w[:, :c_ki], w[:, c_rx:], w[:, c_ki:c_rx],
                           jnp.zeros((D_MODEL, LANES - IDX_DIM - N_IDX_HEADS), w.dtype)], axis=1)
    tril = jnp.tril(jnp.ones((CHUNK, CHUNK), gm_ws.dtype))
    row2 = lambda a: a.reshape(1, -1)
    return dict(
        g_mix=row2(g_mix[l]), w_in=w_r.astype(MXU_DTYPE),
        rec_cw=rec_conv_w[l], rec_cb=row2(rec_conv_b[l]),
        wa=_block_diag(lru_wa[l]).astype(MXU_DTYPE), ba=row2(lru_ba[l]),
        wx=_block_diag(lru_wx[l]).astype(MXU_DTYPE), bx=row2(lru_bx[l]), lam=row2(lru_lam[l]),
        ln_g=row2(gm_ln_g[l]), ln_b=row2(gm_ln_b[l]), gm_w=gm_ws[l] * tril, gm_b=gm_bs[l],
        w_out=w_out[l].astype(MXU_DTYPE), g_ffn=row2(g_ffn[l]),
        w_up=w_up[l].astype(MXU_DTYPE), ffn_cw=ffn_conv_w[l], ffn_cb=row2(ffn_conv_b[l]),
        w_down=w_down[l].astype(MXU_DTYPE))


def _gmlp_mix_weights(p, c, n_seq):
    w = p['gm_w'][:, :c, :c]
    if n_seq > 1:
        w = jnp.einsum('ab,gts->gatbs', jnp.eye(n_seq, dtype=w.dtype), w).reshape(
            N_GM_HEADS, n_seq * c, n_seq * c)
    bias = jnp.repeat(p['gm_b'][:, :c].T, GM_HEAD_DIM, axis=1)
    return w.astype(MXU_DTYPE), jnp.tile(bias, (n_seq, 1))


def _prompt_layer(x, p, rel_bias, g_final, batch, seq, final):
    tm = 512
    (q, k, v, ki, kbf, vtbf, kibf, qi, _, wit, rec, gm) = _in_proj(x, p['g_mix'], p['w_in'], tm)
    att = _prompt_attention(rel_bias, q, qi, wit, kibf, kbf, vtbf, batch, seq)
    zeros8 = jnp.zeros((batch, SUBLANES, D_REC), F32)
    rec_out, hlast, cstate = _rec_branch(rec, zeros8, zeros8, p['rec_cw'], p['rec_cb'], p['wa'], p['ba'],
                                         p['wx'], p['bx'], p['lam'], batch, seq, 256)
    c = min(CHUNK, seq)
    wmix, bias = _gmlp_mix_weights(p, c, 1)
    gm_out, _ = _gmlp(gm, wmix, bias, p['ln_g'], p['ln_b'], tm)
    x1, h2 = _out_proj(x, att, rec_out, gm_out, p['w_out'], p['g_ffn'], tm)
    hist16 = jnp.zeros((batch, HALO, 2 * D_FF), F32)
    x2, upst = _ffn_prompt(h2, x1, hist16, p['w_up'], p['ffn_cw'], p['ffn_cb'], p['w_down'], g_final,
                           batch, seq, tm, final)
    state = (k.reshape(batch, seq, N_KV_HEADS, HEAD_DIM), v.reshape(batch, seq, N_KV_HEADS, HEAD_DIM),
             ki.reshape(batch, seq, IDX_DIM), hlast[:, SUBLANES - 1, :],
             cstate[:, SUBLANES - (REC_CONV_W - 1):, :], upst[:, HALO - (FFN_CONV_W - 1):, :])
    return x2, state


def _sample_layer(x, p, rel_bias, g_final, l, batch, t_new, final, page_table, cache_k, cache_v, cache_kidx,
                  h0, rec_hist, ffn_hist):
    m = batch * t_new
    (q, k, v, ki, _, _, _, qi, kiwi, _, rec, gm) = _in_proj(x, p['g_mix'], p['w_in'], m)
    att = _sample_attention(page_table, rel_bias, q, qi, kiwi, k, v, cache_k, cache_v, cache_kidx,
                            l, batch, t_new)
    hist8 = _pad_rows_front(rec_hist, SUBLANES)
    h08 = jnp.broadcast_to(h0[:, None, :], (batch, SUBLANES, D_REC))
    rec_out, hlast, cstate = _rec_branch(rec, hist8, h08, p['rec_cw'], p['rec_cb'], p['wa'], p['ba'],
                                         p['wx'], p['bx'], p['lam'], batch, t_new, t_new)
    c = min(CHUNK, t_new)
    wmix, bias = _gmlp_mix_weights(p, c, m // c)
    gm_out, vn = _gmlp(gm, wmix, bias, p['ln_g'], p['ln_b'], m)
    x1, h2 = _out_proj(x, att, rec_out, gm_out, p['w_out'], p['g_ffn'], m)
    zrow = jnp.zeros((batch, t_new - 2, 2 * D_FF), F32)
    p2 = jnp.concatenate([ffn_hist, zrow], axis=1).reshape(m, 2 * D_FF)
    p1 = jnp.concatenate([ffn_hist[:, 1:], zrow, zrow[:, :1]], axis=1).reshape(m, 2 * D_FF)
    x2, up = _ffn_sample(h2, x1, p1, p2, p['w_up'], p['ffn_cw'], p['ffn_cb'], p['w_down'], g_final,
                         t_new, final)
    state = (k.reshape(batch, t_new, N_KV_HEADS, HEAD_DIM), v.reshape(batch, t_new, N_KV_HEADS, HEAD_DIM),
             ki.reshape(batch, t_new, IDX_DIM), hlast[:, SUBLANES - 1, :],
             cstate[:, SUBLANES - (REC_CONV_W - 1):, :],
             up.reshape(batch, t_new, 2 * D_FF)[:, t_new - (FFN_CONV_W - 1):, :],
             vn.reshape(batch, t_new, D_GM))
    return x2, state


def kernel(x_prompt, x_sample, cache_k, cache_v, cache_kidx, page_table, state_lru_h, state_conv_rec,
           state_conv_ffn, rel_bias, g_mix, w_in, rec_conv_w, rec_conv_b, lru_wa, lru_ba, lru_wx, lru_bx,
           lru_lam, gm_ln_g, gm_ln_b, gm_ws, gm_bs, w_out, g_ffn, w_up, ffn_conv_w, ffn_conv_b, w_down,
           g_final):
    batch, seq, _ = x_prompt.shape
    dec_batch, t_new, _ = x_sample.shape
    depth = w_in.shape[0]
    assert seq % 512 == 0 and t_new == SUBLANES and (dec_batch * t_new) % KEY_CHUNK == 0
    xp = x_prompt.reshape(batch * seq, D_MODEL)
    xs = x_sample.reshape(dec_batch * t_new, D_MODEL)
    gf = g_final.reshape(1, D_MODEL)
    p_states, s_states = [], []
    for l in range(depth):
        p = _layer_weights(l, g_mix, w_in, rec_conv_w, rec_conv_b, lru_wa, lru_ba, lru_wx, lru_bx, lru_lam,
                           gm_ln_g, gm_ln_b, gm_ws, gm_bs, w_out, g_ffn, w_up, ffn_conv_w, ffn_conv_b, w_down)
        final = l == depth - 1
        xp, st = _prompt_layer(xp, p, rel_bias, gf, batch, seq, final)
        p_states.append(st)
        xs, st = _sample_layer(xs, p, rel_bias, gf, l, dec_batch, t_new, final, page_table, cache_k, cache_v,
                               cache_kidx, state_lru_h[l], state_conv_rec[l], state_conv_ffn[l])
        s_states.append(st)
    stack = lambda states, i: jnp.stack([s[i] for s in states])
    return ((xp.reshape(batch, seq, D_MODEL), xs.reshape(dec_batch, t_new, D_MODEL))
            + tuple(stack(p_states, i) for i in range(6))
            + tuple(stack(s_states, i) for i in range(7)))
```

```python
import functools
import math

import jax
import jax.numpy as jnp
from jax import lax
from jax.experimental import pallas as pl
from jax.experimental.pallas import tpu as pltpu

F32 = jnp.float32
I32 = jnp.int32
I16 = jnp.int16
MXU_DTYPE = jnp.bfloat16

D_MODEL = 1024
N_HEADS = 8
HEAD_DIM = 64
D_ATT = N_HEADS * HEAD_DIM
N_KV_HEADS = 2
GROUP = N_HEADS // N_KV_HEADS
D_KV = N_KV_HEADS * HEAD_DIM
N_IDX_HEADS = 4
IDX_DIM = 64
D_QI = N_IDX_HEADS * IDX_DIM
TOPK_MAX = 256
NUM_BUCKETS = 32
MAX_EXACT = NUM_BUCKETS // 2
MAX_DISTANCE = 128
D_REC = 256
N_REC_BLOCKS = 4
REC_CONV_W = 4
LRU_C = 8.0
D_GM = 256
N_GM_HEADS = 4
GM_HEAD_DIM = D_GM // N_GM_HEADS
CHUNK = 128
D_FF = 2816
FFN_CONV_W = 3
EPS = 1e-6
PAGE_SIZE = 128
Q_SCALE = HEAD_DIM ** -0.5

LANES = 128
SUBLANES = 8
VMEM_LIMIT_BYTES = 56 * 1024 * 1024

C_Q = 0
C_KV = C_Q + D_ATT
C_QI = C_KV + 2 * D_KV
C_REC = C_QI + D_QI
C_GM = C_REC + 2 * D_REC
C_KIWI = C_GM + 2 * D_GM
D_IN_PAD = C_KIWI + LANES

KEY_CHUNK = 256
Q_BLOCK = 128
INT_MIN = -2 ** 31
MIN16 = -2 ** 15
PACKED_ROWS = 2 * SUBLANES
NEG_BIG = -1e30

BUCKET_LO = tuple(
    b if b <= MAX_EXACT else math.ceil(MAX_EXACT * (MAX_DISTANCE / MAX_EXACT) ** ((b - MAX_EXACT) / (NUM_BUCKETS - MAX_EXACT)))
    for b in range(NUM_BUCKETS))


def _cparams(*sem):
    return pltpu.CompilerParams(dimension_semantics=sem, vmem_limit_bytes=VMEM_LIMIT_BYTES)


def _const_spec(shape):
    nd = len(shape)
    return pl.BlockSpec(shape, lambda *_: (0,) * nd)


def _rmsnorm(x, g):
    return x * lax.rsqrt(jnp.mean(x * x, axis=-1, keepdims=True) + EPS) * g


def _gelu(x):
    return 0.5 * x * (1.0 + jnp.tanh(math.sqrt(2.0 / math.pi) * (x + 0.044715 * (x * x * x))))


def _sigmoid(x):
    return 1.0 / (1.0 + jnp.exp(-x))


def _mm(a, b):
    return jnp.dot(a, b, preferred_element_type=F32)


def _mm_nt(a, b):
    return lax.dot_general(a, b, (((1,), (1,)), ((), ())), preferred_element_type=F32)


def _sortable_key(s):
    bits = pltpu.bitcast(s, I32)
    key = bits ^ ((bits >> 31) & 0x7FFFFFFF)
    return jnp.where(s == 0.0, 0, key)


def _split_key(key):
    hi = (key >> 16).astype(I16)
    lo = ((key & 0xFFFF) + MIN16).astype(I16)
    return hi, lo


def _row16(t):
    return jnp.broadcast_to(t, (PACKED_ROWS, t.shape[1])).astype(I16)


def _tree_sum(xs):
    while len(xs) > 1:
        xs = [a + b for a, b in zip(xs[0::2], xs[1::2])] + ([xs[-1]] if len(xs) % 2 else [])
    return xs[0]


def _pair_loop(n_pairs, chunk_fn, init):
    def body(p, carry):
        return chunk_fn(2 * p + 1, chunk_fn(2 * p, carry))
    return lax.fori_loop(0, n_pairs, body, init)


def _count16(ref, n_pairs, pred):
    one = jnp.ones((PACKED_ROWS, LANES), I16)
    zero = jnp.zeros((PACKED_ROWS, LANES), I16)

    def body(p, acc):
        hits = []
        for x in (ref[2 * p], ref[2 * p + 1]):
            for j in range(KEY_CHUNK // PACKED_ROWS):
                hits.append(jnp.where(pred(x[j * PACKED_ROWS:(j + 1) * PACKED_ROWS]), one, zero))
        return acc + _tree_sum(hits)

    acc = lax.fori_loop(0, n_pairs, body, zero)
    return jnp.sum(acc.astype(F32), axis=0, keepdims=True)


def _search16(ref, n_pairs, need):
    def step(s, t):
        cand = t + lax.shift_left(jnp.int32(1), 15 - s)
        cand16 = _row16(cand)
        cnt = _count16(ref, n_pairs, lambda x: x >= cand16)
        return jnp.where(cnt >= need, cand, t)

    return lax.fori_loop(0, 16, step, jnp.full((1, LANES), MIN16, I32))


def _in_proj_body(x_ref, g_ref, w_ref, q_ref, k_ref, v_ref, ki_ref, kbf_ref, vt_ref, kibf_ref,
                  qi_ref, kiwi_ref, wit_ref, rec_ref, gm_ref):
    n_chunks = kbf_ref.shape[0]
    h = _rmsnorm(x_ref[...], g_ref[...]).astype(MXU_DTYPE)

    def proj(lo, hi):
        return _mm(h, w_ref[:, lo:hi])

    q_ref[...] = (proj(C_Q, C_KV) * Q_SCALE).astype(MXU_DTYPE)
    kv = proj(C_KV, C_QI)
    k = kv[:, :D_KV]
    v = kv[:, D_KV:]
    k_ref[...] = k
    v_ref[...] = v
    vt = v.T
    kb = k.astype(MXU_DTYPE)
    qi_ref[...] = proj(C_QI, C_REC).astype(MXU_DTYPE)
    rec_ref[...] = proj(C_REC, C_GM)
    gm_ref[...] = proj(C_GM, C_KIWI)
    kiwi = proj(C_KIWI, D_IN_PAD)
    ki = kiwi[:, :IDX_DIM]
    ki_ref[...] = ki
    kiwi_ref[...] = kiwi
    kib = ki.astype(MXU_DTYPE)
    wit_ref[...] = kiwi.T[IDX_DIM:IDX_DIM + SUBLANES, :]
    for c in range(n_chunks):
        rows = slice(c * KEY_CHUNK, (c + 1) * KEY_CHUNK)
        kbf_ref[c] = kb[rows]
        kibf_ref[c] = kib[rows]
        vt_ref[c] = vt[:, rows].astype(MXU_DTYPE)


def _in_proj(x, g, w, tm):
    m = x.shape[0]
    nc = tm // KEY_CHUNK
    row = lambda n: pl.BlockSpec((tm, n), lambda i: (i, 0))
    chunked = lambda a, b: pl.BlockSpec((nc, a, b), lambda i: (i, 0, 0))
    out_shape = (
        jax.ShapeDtypeStruct((m, D_ATT), MXU_DTYPE),
        jax.ShapeDtypeStruct((m, D_KV), F32),
        jax.ShapeDtypeStruct((m, D_KV), F32),
        jax.ShapeDtypeStruct((m, IDX_DIM), F32),
        jax.ShapeDtypeStruct((m // KEY_CHUNK, KEY_CHUNK, D_KV), MXU_DTYPE),
        jax.ShapeDtypeStruct((m // KEY_CHUNK, D_KV, KEY_CHUNK), MXU_DTYPE),
        jax.ShapeDtypeStruct((m // KEY_CHUNK, KEY_CHUNK, IDX_DIM), MXU_DTYPE),
        jax.ShapeDtypeStruct((m, D_QI), MXU_DTYPE),
        jax.ShapeDtypeStruct((m, LANES), F32),
        jax.ShapeDtypeStruct((SUBLANES, m), F32),
        jax.ShapeDtypeStruct((m, 2 * D_REC), F32),
        jax.ShapeDtypeStruct((m, 2 * D_GM), F32),
    )
    out_specs = (
        row(D_ATT), row(D_KV), row(D_KV), row(IDX_DIM),
        chunked(KEY_CHUNK, D_KV), chunked(D_KV, KEY_CHUNK), chunked(KEY_CHUNK, IDX_DIM),
        row(D_QI), row(LANES), pl.BlockSpec((SUBLANES, tm), lambda i: (0, i)),
        row(2 * D_REC), row(2 * D_GM),
    )
    return pl.pallas_call(
        _in_proj_body,
        grid=(m // tm,),
        in_specs=[row(D_MODEL), _const_spec((1, D_MODEL)), _const_spec((D_MODEL, D_IN_PAD))],
        out_specs=out_specs,
        out_shape=out_shape,
        compiler_params=_cparams("arbitrary"),
        name="in_proj",
    )(x, g, w)


def _bias_table(rb_ref, head, delta, shape):
    row = lax.broadcasted_iota(I32, shape, 0)
    lane = lax.broadcasted_iota(I32, shape, 1)
    d = delta + lane - row
    far = rb_ref[NUM_BUCKETS - 1, head]
    val = jnp.full(shape, rb_ref[0, head] - far, F32)
    for b in range(1, NUM_BUCKETS - 1):
        val = jnp.where(d >= BUCKET_LO[b], rb_ref[b, head] - far, val)
    return jnp.where(d >= BUCKET_LO[NUM_BUCKETS - 1], 0.0, val)


def _prompt_attn_body(rb_ref, q_ref, qi_ref, wit_ref, ki_ref, k_ref, vt_ref, att_ref,
                      key_s, hi_s, lo_s, mb_s, tab_s, tri_s, m_s, l_s, acc_s, qpad_s, *, topk):
    b = pl.program_id(0)
    i = pl.program_id(1)
    n_pairs = i // 4 + 1
    ck = (KEY_CHUNK, Q_BLOCK)

    @pl.when((b == 0) & (i == 0))
    def _init_tables():
        r = lax.broadcasted_iota(I32, (KEY_CHUNK, KEY_CHUNK), 0)
        c = lax.broadcasted_iota(I32, (KEY_CHUNK, KEY_CHUNK), 1)
        tri_s[...] = jnp.where(c <= r, 1.0, 0.0).astype(MXU_DTYPE)

        def per_head(h, carry):
            for ti in range(4):
                tab_s[ti, h] = _bias_table(rb_ref, h, ti * Q_BLOCK, ck)
            return carry

        lax.fori_loop(0, N_HEADS, per_head, 0)

    row = lax.broadcasted_iota(I32, ck, 0)
    lane = lax.broadcasted_iota(I32, ck, 1)
    q_pos = i * Q_BLOCK + lane

    qi = qi_ref[...]
    qis = jnp.concatenate([qi[:, h * IDX_DIM:(h + 1) * IDX_DIM] for h in range(N_IDX_HEADS)], axis=0)
    wit = wit_ref[...]
    w_row = jnp.concatenate([wit[h:h + 1, :] for h in range(N_IDX_HEADS)], axis=1)

    def score_chunk(c, carry):
        s4 = jnp.maximum(_mm_nt(ki_ref[c], qis), 0.0) * w_row
        s = _tree_sum([s4[:, h * Q_BLOCK:(h + 1) * Q_BLOCK] for h in range(N_IDX_HEADS)])
        valid = (c * KEY_CHUNK + row) <= q_pos
        key = jnp.where(valid, _sortable_key(s), INT_MIN)
        key_s[c] = key
        hi_s[c], lo_s[c] = _split_key(key)
        return carry

    _pair_loop(n_pairs, score_chunk, 0)

    t_hi = _search16(hi_s, n_pairs, topk)
    t_hi16 = jnp.broadcast_to(t_hi, ck).astype(I16)
    need_lo = topk - _count16(hi_s, n_pairs, lambda x: x > t_hi16[:PACKED_ROWS])

    def narrow_chunk(c, carry):
        lo_s[c] = jnp.where(hi_s[c] == t_hi16, lo_s[c], MIN16)
        return carry

    _pair_loop(n_pairs, narrow_chunk, 0)
    t_lo = _search16(lo_s, n_pairs, need_lo)
    t_lo16 = _row16(t_lo)
    t = t_hi * 65536 + (t_lo - MIN16)
    need = need_lo - _count16(lo_s, n_pairs, lambda x: x > t_lo16)

    tri = tri_s[...]

    def mask_chunk(c, carry):
        key = key_s[c]
        tie = (key == t) & (key > INT_MIN)
        tie_f = jnp.where(tie, 1.0, 0.0)
        incl = _mm(tri, tie_f.astype(MXU_DTYPE))
        rank = carry + incl - tie_f
        sel = (key > t) | (tie & (rank < need))
        mb_s[c] = jnp.where(sel, 0.0, -jnp.inf)
        return carry + incl[KEY_CHUNK - 1:KEY_CHUNK, :]

    _pair_loop(n_pairs, mask_chunk, jnp.zeros((1, Q_BLOCK), F32))

    q = q_ref[...]
    zeros = jnp.zeros((Q_BLOCK, HEAD_DIM), MXU_DTYPE)
    for h in range(N_HEADS):
        qh = q[:, h * HEAD_DIM:(h + 1) * HEAD_DIM]
        parts = [qh, zeros] if h < GROUP else [zeros, qh]
        qpad_s[h * Q_BLOCK:(h + 1) * Q_BLOCK, :] = jnp.concatenate(parts, axis=1)
    m_s[...] = jnp.full(m_s.shape, NEG_BIG, F32)
    l_s[...] = jnp.zeros(l_s.shape, F32)
    acc_s[...] = jnp.zeros(acc_s.shape, F32)

    def attend_pair(p, near):
        c0, c1 = 2 * p, 2 * p + 1
        keys = jnp.concatenate([k_ref[c0], k_ref[c1]], axis=0)
        logits = _mm_nt(keys, qpad_s[...])
        mb = jnp.concatenate([mb_s[c0], mb_s[c1]], axis=0)
        if near:
            t0 = jnp.clip(i - 2 * c0, 0, 3)
            t1 = jnp.clip(i - 2 * c1, 0, 3)
        for g in range(N_KV_HEADS):
            ps, alphas = [], []
            for hh in range(GROUP):
                h = g * GROUP + hh
                cols = slice(h * Q_BLOCK, (h + 1) * Q_BLOCK)
                lt = logits[:, cols] + mb
                if near:
                    lt = lt + jnp.concatenate([tab_s[t0, h], tab_s[t1, h]], axis=0)
                m_old = m_s[0:1, cols]
                m_new = jnp.maximum(m_old, jnp.max(lt, axis=0, keepdims=True))
                alpha = jnp.exp(m_old - m_new)
                p_h = jnp.exp(lt - m_new)
                l_s[0:1, cols] = alpha * l_s[0:1, cols] + jnp.sum(p_h, axis=0, keepdims=True)
                m_s[0:1, cols] = m_new
                ps.append(p_h.astype(MXU_DTYPE))
                alphas.append(alpha)
            pg = jnp.concatenate(ps, axis=1)
            ag = jnp.concatenate(alphas, axis=1)
            rows = slice(g * HEAD_DIM, (g + 1) * HEAD_DIM)
            vt = jnp.concatenate([vt_ref[c0, rows, :], vt_ref[c1, rows, :]], axis=1)
            acc_s[g] = acc_s[g] * ag + _mm(vt, pg)

    def far_pair(p, carry):
        attend_pair(p, False)
        return carry

    lax.fori_loop(0, jnp.maximum(n_pairs - 2, 0), far_pair, 0)

    @pl.when(n_pairs >= 2)
    def _second_last():
        attend_pair(n_pairs - 2, True)

    attend_pair(n_pairs - 1, True)

    for h in range(N_HEADS):
        g, hh = divmod(h, GROUP)
        cols = slice(hh * Q_BLOCK, (hh + 1) * Q_BLOCK)
        o = acc_s[g][:, cols] / l_s[0:1, h * Q_BLOCK:(h + 1) * Q_BLOCK]
        att_ref[:, h * HEAD_DIM:(h + 1) * HEAD_DIM] = o.T.astype(att_ref.dtype)


def _prompt_attention(rel_bias, q, qi, wit, kibf, kbf, vtbf, batch, seq):
    nq = seq // Q_BLOCK
    nc = seq // KEY_CHUNK
    topk = float(min(TOPK_MAX, seq // 4))
    qrow = lambda n: pl.BlockSpec((Q_BLOCK, n), lambda b, i: (b * nq + i, 0))
    per_batch = lambda a, c: pl.BlockSpec((nc, a, c), lambda b, i: (b, 0, 0))
    return pl.pallas_call(
        functools.partial(_prompt_attn_body, topk=topk),
        grid=(batch, nq),
        in_specs=[
            pl.BlockSpec(memory_space=pltpu.SMEM),
            qrow(D_ATT), qrow(D_QI),
            pl.BlockSpec((SUBLANES, Q_BLOCK), lambda b, i: (0, b * nq + i)),
            per_batch(KEY_CHUNK, IDX_DIM), per_batch(KEY_CHUNK, D_KV), per_batch(D_KV, KEY_CHUNK),
        ],
        out_specs=qrow(D_ATT),
        out_shape=jax.ShapeDtypeStruct((batch * seq, D_ATT), MXU_DTYPE),
        scratch_shapes=[
            pltpu.VMEM((nc, KEY_CHUNK, Q_BLOCK), I32),
            pltpu.VMEM((nc, KEY_CHUNK, Q_BLOCK), I16),
            pltpu.VMEM((nc, KEY_CHUNK, Q_BLOCK), I16),
            pltpu.VMEM((nc, KEY_CHUNK, Q_BLOCK), F32),
            pltpu.VMEM((4, N_HEADS, KEY_CHUNK, Q_BLOCK), F32),
            pltpu.VMEM((KEY_CHUNK, KEY_CHUNK), MXU_DTYPE),
            pltpu.VMEM((SUBLANES, N_HEADS * Q_BLOCK), F32),
            pltpu.VMEM((SUBLANES, N_HEADS * Q_BLOCK), F32),
            pltpu.VMEM((N_KV_HEADS, HEAD_DIM, GROUP * Q_BLOCK), F32),
            pltpu.VMEM((N_HEADS * Q_BLOCK, D_KV), MXU_DTYPE),
        ],
        compiler_params=_cparams("arbitrary", "arbitrary"),
        name="prompt_attn",
    )(rel_bias, q, qi, wit, kibf, kbf, vtbf)


def _sample_attn_body(pt_ref, rb_ref, q_ref, qi_ref, kiwi_ref, knew_ref, vnew_ref,
                      ck_hbm, cv_hbm, cki_hbm, att_ref,
                      ktbuf, vtbuf, kitbuf, sem, *, topk, n_pages, layer, t_new):
    b = pl.program_id(0)
    past = n_pages * PAGE_SIZE

    def page_copies(p):
        phys = pt_ref[b, p]
        cols = pl.ds(pl.multiple_of(p * PAGE_SIZE, PAGE_SIZE), PAGE_SIZE)
        return (pltpu.make_async_copy(ck_hbm.at[layer, phys], ktbuf.at[:, cols], sem.at[0]),
                pltpu.make_async_copy(cv_hbm.at[layer, phys], vtbuf.at[:, cols], sem.at[1]),
                pltpu.make_async_copy(cki_hbm.at[layer, phys], kitbuf.at[:, cols], sem.at[2]))

    def start_page(p, carry):
        for cp in page_copies(p):
            cp.start()
        return carry

    def wait_page(p, carry):
        for cp in page_copies(p):
            cp.wait()
        return carry

    lax.fori_loop(0, n_pages, start_page, 0)

    qi = qi_ref[...]
    qis = jnp.concatenate([qi[:, h * IDX_DIM:(h + 1) * IDX_DIM] for h in range(N_IDX_HEADS)], axis=0)
    kiwi = kiwi_ref[...]
    w_col = jnp.concatenate([kiwi[:, IDX_DIM + h:IDX_DIM + h + 1] for h in range(N_IDX_HEADS)], axis=0)
    q = q_ref[...]
    zeros = jnp.zeros((t_new, HEAD_DIM), MXU_DTYPE)
    qpad = jnp.concatenate(
        [jnp.concatenate([q[:, h * HEAD_DIM:(h + 1) * HEAD_DIM], zeros] if h < GROUP else
                         [zeros, q[:, h * HEAD_DIM:(h + 1) * HEAD_DIM]], axis=1)
         for h in range(N_HEADS)], axis=0)

    lax.fori_loop(0, n_pages, wait_page, 0)

    def idx_score(qk):
        s4 = jnp.maximum(qk, 0.0) * w_col
        s = s4[0:t_new]
        for h in range(1, N_IDX_HEADS):
            s = s + s4[h * t_new:(h + 1) * t_new]
        return s

    key_p = _sortable_key(idx_score(_mm(qis, kitbuf[...].astype(MXU_DTYPE))))
    ki_new = kiwi[:, :IDX_DIM].astype(MXU_DTYPE)
    ki_new = jnp.concatenate([ki_new, jnp.zeros((LANES - t_new, IDX_DIM), MXU_DTYPE)], axis=0)
    rown = lax.broadcasted_iota(I32, (t_new, LANES), 0)
    lanen = lax.broadcasted_iota(I32, (t_new, LANES), 1)
    valid_n = lanen <= rown
    key_n = jnp.where(valid_n, _sortable_key(idx_score(_mm_nt(qis, ki_new))), INT_MIN)

    def count(pred_fn):
        return (jnp.sum(jnp.where(pred_fn(key_p), 1.0, 0.0), axis=1, keepdims=True)
                + jnp.sum(jnp.where(pred_fn(key_n), 1.0, 0.0), axis=1, keepdims=True))

    def search_step(step, t):
        cand = t + lax.shift_left(jnp.int32(1), 31 - step)
        return jnp.where(count(lambda k: k >= cand) >= topk, cand, t)

    t = lax.fori_loop(0, 32, search_step, jnp.full((t_new, 1), INT_MIN, I32))
    need = topk - count(lambda k: k > t)

    r = lax.broadcasted_iota(I32, (KEY_CHUNK, KEY_CHUNK), 0)
    c = lax.broadcasted_iota(I32, (KEY_CHUNK, KEY_CHUNK), 1)
    tri = jnp.where(r <= c, 1.0, 0.0).astype(MXU_DTYPE)
    tie_p = key_p == t
    carry = jnp.zeros((t_new, 1), F32)
    mask_parts = []
    for ch in range(past // KEY_CHUNK):
        cols = slice(ch * KEY_CHUNK, (ch + 1) * KEY_CHUNK)
        tie_c = tie_p[:, cols]
        tie_f = jnp.where(tie_c, 1.0, 0.0)
        incl = _mm(tie_f.astype(MXU_DTYPE), tri)
        sel = (key_p[:, cols] > t) | (tie_c & ((carry + incl - tie_f) < need))
        mask_parts.append(jnp.where(sel, 0.0, -jnp.inf))
        carry = carry + incl[:, KEY_CHUNK - 1:KEY_CHUNK]
    mb_p = jnp.concatenate(mask_parts, axis=1)
    tie_n = (key_n == t) & valid_n
    tie_nf = jnp.where(tie_n, 1.0, 0.0)
    incl_n = _mm(tie_nf.astype(MXU_DTYPE), tri[:LANES, :LANES])
    sel_n = (key_n > t) | (tie_n & ((carry + incl_n - tie_nf) < need))
    mb_n = jnp.where(sel_n & valid_n, 0.0, -jnp.inf)

    lg_p = _mm(qpad, ktbuf[...].astype(MXU_DTYPE))
    k_new = jnp.concatenate([knew_ref[...].astype(MXU_DTYPE),
                             jnp.zeros((LANES - t_new, D_KV), MXU_DTYPE)], axis=0)
    lg_n = _mm_nt(qpad, k_new)
    near = slice(past - LANES, past)
    lp_rows, ln_rows = [], []
    for h in range(N_HEADS):
        rows = slice(h * t_new, (h + 1) * t_new)
        far = rb_ref[NUM_BUCKETS - 1, h]

        def bias(d):
            val = jnp.full(d.shape, rb_ref[0, h] - far, F32)
            for bk in range(1, NUM_BUCKETS - 1):
                val = jnp.where(d >= BUCKET_LO[bk], rb_ref[bk, h] - far, val)
            return jnp.where(d >= BUCKET_LO[NUM_BUCKETS - 1], 0.0, val)

        lp = lg_p[rows] + mb_p
        lp_near = lp[:, near] + bias(rown + (LANES - lanen))
        lp_rows.append(jnp.concatenate([lp[:, :past - LANES], lp_near], axis=1))
        ln_rows.append(lg_n[rows] + mb_n + bias(rown - lanen))
    lp_all = jnp.concatenate(lp_rows, axis=0)
    ln_all = jnp.concatenate(ln_rows, axis=0)
    m = jnp.maximum(jnp.max(lp_all, axis=1, keepdims=True), jnp.max(ln_all, axis=1, keepdims=True))
    p_p = jnp.exp(lp_all - m)
    p_n = jnp.exp(ln_all - m)
    denom = jnp.sum(p_p, axis=1, keepdims=True) + jnp.sum(p_n, axis=1, keepdims=True)
    v_new = jnp.concatenate([vnew_ref[...].astype(MXU_DTYPE),
                             jnp.zeros((LANES - t_new, D_KV), MXU_DTYPE)], axis=0)
    o = (_mm_nt(p_p.astype(MXU_DTYPE), vtbuf[...].astype(MXU_DTYPE))
         + _mm(p_n.astype(MXU_DTYPE), v_new)) / denom
    for h in range(N_HEADS):
        g = h // GROUP
        att_ref[:, h * HEAD_DIM:(h + 1) * HEAD_DIM] = (
            o[h * t_new:(h + 1) * t_new, g * HEAD_DIM:(g + 1) * HEAD_DIM].astype(att_ref.dtype))


def _sample_attention(page_table, rel_bias, q, qi, kiwi, k_new, v_new, cache_k, cache_v, cache_kidx,
                      layer, batch, t_new):
    n_pages = page_table.shape[1]
    past = n_pages * PAGE_SIZE
    topk = float(min(TOPK_MAX, (past + t_new) // 4))
    depth, n_pool = cache_k.shape[:2]
    ck = cache_k.transpose(0, 1, 3, 4, 2).reshape(depth, n_pool, D_KV, PAGE_SIZE)
    cv = cache_v.transpose(0, 1, 3, 4, 2).reshape(depth, n_pool, D_KV, PAGE_SIZE)
    cki = cache_kidx.transpose(0, 1, 3, 2)
    qrow = lambda n: pl.BlockSpec((t_new, n), lambda b, pt: (b, 0))
    hbm = pl.BlockSpec(memory_space=pl.ANY)
    grid_spec = pltpu.PrefetchScalarGridSpec(
        num_scalar_prefetch=1,
        grid=(batch,),
        in_specs=[pl.BlockSpec(memory_space=pltpu.SMEM),
                  qrow(D_ATT), qrow(D_QI), qrow(LANES), qrow(D_KV), qrow(D_KV), hbm, hbm, hbm],
        out_specs=qrow(D_ATT),
        scratch_shapes=[
            pltpu.VMEM((D_KV, past), F32),
            pltpu.VMEM((D_KV, past), F32),
            pltpu.VMEM((IDX_DIM, past), F32),
            pltpu.SemaphoreType.DMA((3,)),
        ],
    )
    return pl.pallas_call(
        functools.partial(_sample_attn_body, topk=topk, n_pages=n_pages, layer=layer, t_new=t_new),
        grid_spec=grid_spec,
        out_shape=jax.ShapeDtypeStruct((batch * t_new, D_ATT), MXU_DTYPE),
        compiler_params=_cparams("arbitrary"),
        name="sample_attn",
    )(page_table, rel_bias, q, qi, kiwi, k_new, v_new, ck, cv, cki)


def _rec_body(rec_ref, hist_ref, h0_ref, cw_ref, cb_ref, wa_ref, ba_ref, wx_ref, bx_ref, lam_ref,
              out_ref, hlast_ref, cstate_ref, hcar, xprev):
    tt = rec_ref.shape[0]

    @pl.when(pl.program_id(1) == 0)
    def _load_state():
        hcar[...] = h0_ref[0]
        xprev[...] = hist_ref[0]

    rx = rec_ref[:, :D_REC]
    rg = rec_ref[:, D_REC:]
    ext = jnp.concatenate([xprev[...], rx], axis=0)
    xc = cb_ref[...]
    for j in range(REC_CONV_W - 1):
        xc = xc + pltpu.roll(ext, REC_CONV_W - 1 - j, 0)[SUBLANES:] * cw_ref[j:j + 1, :]
    xc = xc + rx * cw_ref[REC_CONV_W - 1:REC_CONV_W, :]
    xcb = xc.astype(MXU_DTYPE)
    r = _sigmoid(_mm(xcb, wa_ref[...]) + ba_ref[...])
    gi = _sigmoid(_mm(xcb, wx_ref[...]) + bx_ref[...])
    nl = -lam_ref[...]
    softplus = jnp.maximum(nl, 0.0) + jnp.log1p(jnp.exp(-jnp.abs(nl)))
    log_a = -LRU_C * r * softplus
    a = jnp.exp(log_a)
    u = jnp.sqrt(-jnp.tanh(log_a) * (a * a + 1.0)) * (gi * xc)
    row = lax.broadcasted_iota(I32, (tt, D_REC), 0)
    s = 1
    while s < tt:
        keep = row >= s
        u = jnp.where(keep, u + a * pltpu.roll(u, s, 0), u)
        a = jnp.where(keep, a * pltpu.roll(a, s, 0), a)
        s *= 2
    hs = u + a * hcar[0:1, :]
    hcar[...] = jnp.broadcast_to(hs[tt - 1:tt, :], hcar.shape)
    xprev[...] = rx[tt - SUBLANES:, :]
    out_ref[...] = (_gelu(rg) * hs).astype(out_ref.dtype)
    hlast_ref[0] = hs[tt - SUBLANES:, :]
    cstate_ref[0] = rx[tt - SUBLANES:, :]


def _rec_branch(rec, hist8, h08, cw, cb, wa, ba, wx, bx, lam, batch, t_len, tt):
    nt = t_len // tt
    state = pl.BlockSpec((1, SUBLANES, D_REC), lambda b, t: (b, 0, 0))
    vec = _const_spec((1, D_REC))
    return pl.pallas_call(
        _rec_body,
        grid=(batch, nt),
        in_specs=[pl.BlockSpec((tt, 2 * D_REC), lambda b, t: (b * nt + t, 0)), state, state,
                  _const_spec((REC_CONV_W, D_REC)), vec, _const_spec((D_REC, D_REC)), vec,
                  _const_spec((D_REC, D_REC)), vec, vec],
        out_specs=(pl.BlockSpec((tt, D_REC), lambda b, t: (b * nt + t, 0)), state, state),
        out_shape=(jax.ShapeDtypeStruct((batch * t_len, D_REC), MXU_DTYPE),
                   jax.ShapeDtypeStruct((batch, SUBLANES, D_REC), F32),
                   jax.ShapeDtypeStruct((batch, SUBLANES, D_REC), F32)),
        scratch_shapes=[pltpu.VMEM((SUBLANES, D_REC), F32), pltpu.VMEM((SUBLANES, D_REC), F32)],
        compiler_params=_cparams("arbitrary", "arbitrary"),
        name="rec_branch",
    )(rec, hist8, h08, cw, cb, wa, ba, wx, bx, lam)


def _gmlp_body(gm_ref, wmix_ref, bias_ref, lng_ref, lnb_ref, out_ref, vn_ref):
    rows = wmix_ref.shape[1]
    n_sub = gm_ref.shape[0] // rows
    head_of_lane = lax.broadcasted_iota(I32, (rows, D_GM), 1) // GM_HEAD_DIM
    for sb in range(n_sub):
        sl = slice(sb * rows, (sb + 1) * rows)
        u = _gelu(gm_ref[sl, :D_GM])
        gv = _gelu(gm_ref[sl, D_GM:])
        xc = gv - jnp.mean(gv, axis=-1, keepdims=True)
        var = jnp.mean(xc * xc, axis=-1, keepdims=True)
        vn = xc * lax.rsqrt(var + EPS) * lng_ref[...] + lnb_ref[...]
        vn_ref[sl, :] = vn
        vnb = vn.astype(MXU_DTYPE)
        mix = jnp.zeros((rows, D_GM), F32)
        for g in range(N_GM_HEADS):
            mix = jnp.where(head_of_lane == g, _mm(wmix_ref[g], vnb), mix)
        out_ref[sl, :] = (u * (mix + bias_ref[...])).astype(out_ref.dtype)


def _gmlp(gm, wmix, bias, lng, lnb, tm):
    m = gm.shape[0]
    rows = wmix.shape[1]
    return pl.pallas_call(
        _gmlp_body,
        grid=(m // tm,),
        in_specs=[pl.BlockSpec((tm, 2 * D_GM), lambda i: (i, 0)),
                  _const_spec((N_GM_HEADS, rows, rows)), _const_spec((rows, D_GM)),
                  _const_spec((1, D_GM)), _const_spec((1, D_GM))],
        out_specs=(pl.BlockSpec((tm, D_GM), lambda i: (i, 0)), pl.BlockSpec((tm, D_GM), lambda i: (i, 0))),
        out_shape=(jax.ShapeDtypeStruct((m, D_GM), MXU_DTYPE), jax.ShapeDtypeStruct((m, D_GM), F32)),
        compiler_params=_cparams("arbitrary"),
        name="gmlp",
    )(gm, wmix, bias, lng, lnb)


def _out_proj_body(x_ref, att_ref, rec_ref, gm_ref, w_ref, g_ref, x1_ref, h2_ref):
    y = (_mm(att_ref[...], w_ref[0:D_ATT, :])
         + _mm(rec_ref[...], w_ref[D_ATT:D_ATT + D_REC, :])
         + _mm(gm_ref[...], w_ref[D_ATT + D_REC:, :]))
    x1 = x_ref[...] + y
    x1_ref[...] = x1
    h2_ref[...] = _rmsnorm(x1, g_ref[...]).astype(h2_ref.dtype)


def _out_proj(x, att, rec, gm, w, g, tm):
    m = x.shape[0]
    row = lambda n: pl.BlockSpec((tm, n), lambda i: (i, 0))
    return pl.pallas_call(
        _out_proj_body,
        grid=(m // tm,),
        in_specs=[row(D_MODEL), row(D_ATT), row(D_REC), row(D_GM),
                  _const_spec((D_MODEL, D_MODEL)), _const_spec((1, D_MODEL))],
        out_specs=(row(D_MODEL), row(D_MODEL)),
        out_shape=(jax.ShapeDtypeStruct((m, D_MODEL), F32), jax.ShapeDtypeStruct((m, D_MODEL), MXU_DTYPE)),
        compiler_params=_cparams("arbitrary"),
        name="out_proj",
    )(x, att, rec, gm, w, g)


FF_TILE = 256
HALO = 16


def _ffn_finish(x1, acc, gf_ref, out_ref, final):
    x2 = x1 + acc
    out_ref[...] = _rmsnorm(x2, gf_ref[...]) if final else x2


def _ffn_prompt_body(h_ref, halo_ref, x1_ref, hist_ref, wup_ref, cw_ref, cb_ref, wdn_ref, gf_ref,
                     out_ref, upst_ref, *, tiles_per_seq, final):
    tm = h_ref.shape[0]
    first = (pl.program_id(0) % tiles_per_seq) == 0
    hm = h_ref[...]
    hh = halo_ref[...]
    acc = jnp.zeros((tm, D_MODEL), F32)
    for j in range(D_FF // FF_TILE):
        halves = []
        for base in (0, D_FF):
            cols = slice(base + j * FF_TILE, base + (j + 1) * FF_TILE)
            up_m = _mm(hm, wup_ref[:, cols])
            up_h = jnp.where(first, hist_ref[0, :, cols], _mm(hh, wup_ref[:, cols]))
            upst_ref[0, :, cols] = up_m[tm - HALO:, :]
            ext = jnp.concatenate([up_h, up_m], axis=0)
            uc = cb_ref[:, cols]
            for jj in range(FFN_CONV_W - 1):
                uc = uc + pltpu.roll(ext, FFN_CONV_W - 1 - jj, 0)[HALO:] * cw_ref[jj:jj + 1, cols]
            halves.append(uc + up_m * cw_ref[FFN_CONV_W - 1:FFN_CONV_W, cols])
        act = (_gelu(halves[0]) * halves[1]).astype(MXU_DTYPE)
        acc = acc + _mm(act, wdn_ref[j * FF_TILE:(j + 1) * FF_TILE, :])
    _ffn_finish(x1_ref[...], acc, gf_ref, out_ref, final)


def _ffn_prompt(h2, x1, hist16, wup, cw, cb, wdn, gf, batch, seq, tm, final):
    m = h2.shape[0]
    tps = seq // tm
    row = lambda n: pl.BlockSpec((tm, n), lambda i: (i, 0))
    state = pl.BlockSpec((1, HALO, 2 * D_FF), lambda i: (i // tps, 0, 0))
    return pl.pallas_call(
        functools.partial(_ffn_prompt_body, tiles_per_seq=tps, final=final),
        grid=(m // tm,),
        in_specs=[row(D_MODEL),
                  pl.BlockSpec((HALO, D_MODEL), lambda i: (jnp.maximum(i * (tm // HALO) - 1, 0), 0)),
                  row(D_MODEL), state,
                  _const_spec((D_MODEL, 2 * D_FF)), _const_spec((FFN_CONV_W, 2 * D_FF)),
                  _const_spec((1, 2 * D_FF)), _const_spec((D_FF, D_MODEL)), _const_spec((1, D_MODEL))],
        out_specs=(row(D_MODEL), state),
        out_shape=(jax.ShapeDtypeStruct((m, D_MODEL), F32),
                   jax.ShapeDtypeStruct((batch, HALO, 2 * D_FF), F32)),
        compiler_params=_cparams("arbitrary"),
        name="ffn_prompt",
    )(h2, h2, x1, hist16, wup, cw, cb, wdn, gf)


def _ffn_sample_body(h_ref, x1_ref, p1_ref, p2_ref, wup_ref, cw_ref, cb_ref, wdn_ref, gf_ref,
                     out_ref, up_ref, *, t_new, final):
    tm = h_ref.shape[0]
    hm = h_ref[...]
    pos = lax.broadcasted_iota(I32, (tm, FF_TILE), 0) % t_new
    acc = jnp.zeros((tm, D_MODEL), F32)
    for j in range(D_FF // FF_TILE):
        halves = []
        for base in (0, D_FF):
            cols = slice(base + j * FF_TILE, base + (j + 1) * FF_TILE)
            up = _mm(hm, wup_ref[:, cols])
            up_ref[:, cols] = up
            tap2 = jnp.where(pos < 2, p2_ref[:, cols], pltpu.roll(up, 2, 0))
            tap1 = jnp.where(pos < 1, p1_ref[:, cols], pltpu.roll(up, 1, 0))
            halves.append(cb_ref[:, cols] + tap2 * cw_ref[0:1, cols] + tap1 * cw_ref[1:2, cols]
                          + up * cw_ref[2:3, cols])
        act = (_gelu(halves[0]) * halves[1]).astype(MXU_DTYPE)
        acc = acc + _mm(act, wdn_ref[j * FF_TILE:(j + 1) * FF_TILE, :])
    _ffn_finish(x1_ref[...], acc, gf_ref, out_ref, final)


def _ffn_sample(h2, x1, p1, p2, wup, cw, cb, wdn, gf, t_new, final):
    m = h2.shape[0]
    full = lambda n: _const_spec((m, n))
    return pl.pallas_call(
        functools.partial(_ffn_sample_body, t_new=t_new, final=final),
        grid=(1,),
        in_specs=[full(D_MODEL), full(D_MODEL), full(2 * D_FF), full(2 * D_FF),
                  _const_spec((D_MODEL, 2 * D_FF)), _const_spec((FFN_CONV_W, 2 * D_FF)),
                  _const_spec((1, 2 * D_FF)), _const_spec((D_FF, D_MODEL)), _const_spec((1, D_MODEL))],
        out_specs=(full(D_MODEL), full(2 * D_FF)),
        out_shape=(jax.ShapeDtypeStruct((m, D_MODEL), F32), jax.ShapeDtypeStruct((m, 2 * D_FF), F32)),
        compiler_params=_cparams("arbitrary"),
        name="ffn_sample",
    )(h2, x1, p1, p2, wup, cw, cb, wdn, gf)


def _block_diag(w):
    n, blk, _ = w.shape
    out = jnp.zeros((n * blk, n * blk), w.dtype)
    for i in range(n):
        out = out.at[i * blk:(i + 1) * blk, i * blk:(i + 1) * blk].set(w[i])
    return out


def _pad_rows_front(a, rows):
    pad = rows - a.shape[1]
    return jnp.pad(a, ((0, 0), (pad, 0), (0, 0)))


def _layer_weights(l, g_mix, w_in, rec_conv_w, rec_conv_b, lru_wa, lru_ba, lru_wx, lru_bx, lru_lam,
                   gm_ln_g, gm_ln_b, gm_ws, gm_bs, w_out, g_ffn, w_up, ffn_conv_w, ffn_conv_b, w_down):
    w = w_in[l]
    c_ki = D_ATT + 2 * D_KV + D_QI
    c_rx = c_ki + IDX_DIM + N_IDX_HEADS
    w_r = jnp.concatenate([w[:, :c_ki], w[:, c_rx:], w[:, c_ki:c_rx],
                           jnp.zeros((D_MODEL, LANES - IDX_DIM - N_IDX_HEADS), w.dtype)], axis=1)
    tril = jnp.tril(jnp.ones((CHUNK, CHUNK), gm_ws.dtype))
    row2 = lambda a: a.reshape(1, -1)
    return dict(
        g_mix=row2(g_mix[l]), w_in=w_r.astype(MXU_DTYPE),
        rec_cw=rec_conv_w[l], rec_cb=row2(rec_conv_b[l]),
        wa=_block_diag(lru_wa[l]).astype(MXU_DTYPE), ba=row2(lru_ba[l]),
        wx=_block_diag(lru_wx[l]).astype(MXU_DTYPE), bx=row2(lru_bx[l]), lam=row2(lru_lam[l]),
        ln_g=row2(gm_ln_g[l]), ln_b=row2(gm_ln_b[l]), gm_w=gm_ws[l] * tril, gm_b=gm_bs[l],
        w_out=w_out[l].astype(MXU_DTYPE), g_ffn=row2(g_ffn[l]),
        w_up=w_up[l].astype(MXU_DTYPE), ffn_cw=ffn_conv_w[l], ffn_cb=row2(ffn_conv_b[l]),
        w_down=w_down[l].astype(MXU_DTYPE))


def _gmlp_mix_weights(p, c, n_seq):
    w = p['gm_w'][:, :c, :c]
    if n_seq > 1:
        w = jnp.einsum('ab,gts->gatbs', jnp.eye(n_seq, dtype=w.dtype), w).reshape(
            N_GM_HEADS, n_seq * c, n_seq * c)
    bias = jnp.repeat(p['gm_b'][:, :c].T, GM_HEAD_DIM, axis=1)
    return w.astype(MXU_DTYPE), jnp.tile(bias, (n_seq, 1))


def _prompt_layer(x, p, rel_bias, g_final, batch, seq, final):
    tm = 512
    (q, k, v, ki, kbf, vtbf, kibf, qi, _, wit, rec, gm) = _in_proj(x, p['g_mix'], p['w_in'], tm)
    att = _prompt_attention(rel_bias, q, qi, wit, kibf, kbf, vtbf, batch, seq)
    zeros8 = jnp.zeros((batch, SUBLANES, D_REC), F32)
    rec_out, hlast, cstate = _rec_branch(rec, zeros8, zeros8, p['rec_cw'], p['rec_cb'], p['wa'], p['ba'],
                                         p['wx'], p['bx'], p['lam'], batch, seq, 256)
    c = min(CHUNK, seq)
    wmix, bias = _gmlp_mix_weights(p, c, 1)
    gm_out, _ = _gmlp(gm, wmix, bias, p['ln_g'], p['ln_b'], tm)
    x1, h2 = _out_proj(x, att, rec_out, gm_out, p['w_out'], p['g_ffn'], tm)
    hist16 = jnp.zeros((batch, HALO, 2 * D_FF), F32)
    x2, upst = _ffn_prompt(h2, x1, hist16, p['w_up'], p['ffn_cw'], p['ffn_cb'], p['w_down'], g_final,
                           batch, seq, tm, final)
    state = (k.reshape(batch, seq, N_KV_HEADS, HEAD_DIM), v.reshape(batch, seq, N_KV_HEADS, HEAD_DIM),
             ki.reshape(batch, seq, IDX_DIM), hlast[:, SUBLANES - 1, :],
             cstate[:, SUBLANES - (REC_CONV_W - 1):, :], upst[:, HALO - (FFN_CONV_W - 1):, :])
    return x2, state


def _sample_layer(x, p, rel_bias, g_final, l, batch, t_new, final, page_table, cache_k, cache_v, cache_kidx,
                  h0, rec_hist, ffn_hist):
    m = batch * t_new
    (q, k, v, ki, _, _, _, qi, kiwi, _, rec, gm) = _in_proj(x, p['g_mix'], p['w_in'], m)
    att = _sample_attention(page_table, rel_bias, q, qi, kiwi, k, v, cache_k, cache_v, cache_kidx,
                            l, batch, t_new)
    hist8 = _pad_rows_front(rec_hist, SUBLANES)
    h08 = jnp.broadcast_to(h0[:, None, :], (batch, SUBLANES, D_REC))
    rec_out, hlast, cstate = _rec_branch(rec, hist8, h08, p['rec_cw'], p['rec_cb'], p['wa'], p['ba'],
                                         p['wx'], p['bx'], p['lam'], batch, t_new, t_new)
    c = min(CHUNK, t_new)
    wmix, bias = _gmlp_mix_weights(p, c, m // c)
    gm_out, vn = _gmlp(gm, wmix, bias, p['ln_g'], p['ln_b'], m)
    x1, h2 = _out_proj(x, att, rec_out, gm_out, p['w_out'], p['g_ffn'], m)
    zrow = jnp.zeros((batch, t_new - 2, 2 * D_FF), F32)
    p2 = jnp.concatenate([ffn_hist, zrow], axis=1).reshape(m, 2 * D_FF)
    p1 = jnp.concatenate([ffn_hist[:, 1:], zrow, zrow[:, :1]], axis=1).reshape(m, 2 * D_FF)
    x2, up = _ffn_sample(h2, x1, p1, p2, p['w_up'], p['ffn_cw'], p['ffn_cb'], p['w_down'], g_final,
                         t_new, final)
    state = (k.reshape(batch, t_new, N_KV_HEADS, HEAD_DIM), v.reshape(batch, t_new, N_KV_HEADS, HEAD_DIM),
             ki.reshape(batch, t_new, IDX_DIM), hlast[:, SUBLANES - 1, :],
             cstate[:, SUBLANES - (REC_CONV_W - 1):, :],
             up.reshape(batch, t_new, 2 * D_FF)[:, t_new - (FFN_CONV_W - 1):, :],
             vn.reshape(batch, t_new, D_GM))
    return x2, state


def kernel(x_prompt, x_sample, cache_k, cache_v, cache_kidx, page_table, state_lru_h, state_conv_rec,
           state_conv_ffn, rel_bias, g_mix, w_in, rec_conv_w, rec_conv_b, lru_wa, lru_ba, lru_wx, lru_bx,
           lru_lam, gm_ln_g, gm_ln_b, gm_ws, gm_bs, w_out, g_ffn, w_up, ffn_conv_w, ffn_conv_b, w_down,
           g_final):
    batch, seq, _ = x_prompt.shape
    dec_batch, t_new, _ = x_sample.shape
    depth = w_in.shape[0]
    assert seq % 512 == 0 and t_new == SUBLANES and (dec_batch * t_new) % KEY_CHUNK == 0
    xp = x_prompt.reshape(batch * seq, D_MODEL)
    xs = x_sample.reshape(dec_batch * t_new, D_MODEL)
    gf = g_final.reshape(1, D_MODEL)
    p_states, s_states = [], []
    for l in range(depth):
        p = _layer_weights(l, g_mix, w_in, rec_conv_w, rec_conv_b, lru_wa, lru_ba, lru_wx, lru_bx, lru_lam,
                           gm_ln_g, gm_ln_b, gm_ws, gm_bs, w_out, g_ffn, w_up, ffn_conv_w, ffn_conv_b, w_down)
        final = l == depth - 1
        xp, st = _prompt_layer(xp, p, rel_bias, gf, batch, seq, final)
        p_states.append(st)
        xs, st = _sample_layer(xs, p, rel_bias, gf, l, dec_batch, t_new, final, page_table, cache_k, cache_v,
                               cache_kidx, state_lru_h[l], state_conv_rec[l], state_conv_ffn[l])
        s_states.append(st)
    stack = lambda states, i: jnp.stack([s[i] for s in states])
    return ((xp.reshape(batch, seq, D_MODEL), xs.reshape(dec_batch, t_new, D_MODEL))
            + tuple(stack(p_states, i) for i in range(6))
            + tuple(stack(s_states, i) for i in range(7)))
```

```python
import functools
import math

import jax
import jax.numpy as jnp
from jax import lax
from jax.experimental import pallas as pl
from jax.experimental.pallas import tpu as pltpu

F32 = jnp.float32
I32 = jnp.int32
I16 = jnp.int16
MXU_DTYPE = jnp.bfloat16

D_MODEL = 1024
N_HEADS = 8
HEAD_DIM = 64
D_ATT = N_HEADS * HEAD_DIM
N_KV_HEADS = 2
GROUP = N_HEADS // N_KV_HEADS
D_KV = N_KV_HEADS * HEAD_DIM
N_IDX_HEADS = 4
IDX_DIM = 64
D_QI = N_IDX_HEADS * IDX_DIM
TOPK_MAX = 256
NUM_BUCKETS = 32
MAX_EXACT = NUM_BUCKETS // 2
MAX_DISTANCE = 128
D_REC = 256
N_REC_BLOCKS = 4
REC_CONV_W = 4
LRU_C = 8.0
D_GM = 256
N_GM_HEADS = 4
GM_HEAD_DIM = D_GM // N_GM_HEADS
CHUNK = 128
D_FF = 2816
FFN_CONV_W = 3
EPS = 1e-6
PAGE_SIZE = 128
Q_SCALE = HEAD_DIM ** -0.5

LANES = 128
SUBLANES = 8
VMEM_LIMIT_BYTES = 56 * 1024 * 1024

C_Q = 0
C_KV = C_Q + D_ATT
C_QI = C_KV + 2 * D_KV
C_REC = C_QI + D_QI
C_GM = C_REC + 2 * D_REC
C_KIWI = C_GM + 2 * D_GM
D_IN_PAD = C_KIWI + LANES

KEY_CHUNK = 256
Q_BLOCK = 128
INT_MIN = -2 ** 31
MIN16 = -2 ** 15
PACKED_ROWS = 2 * SUBLANES
NEG_BIG = -2.0 ** 100
SEARCH_CLASS_PAIRS = 2

BUCKET_LO = tuple(
    b if b <= MAX_EXACT else math.ceil(MAX_EXACT * (MAX_DISTANCE / MAX_EXACT) ** ((b - MAX_EXACT) / (NUM_BUCKETS - MAX_EXACT)))
    for b in range(NUM_BUCKETS))


def _cparams(*sem):
    return pltpu.CompilerParams(dimension_semantics=sem, vmem_limit_bytes=VMEM_LIMIT_BYTES)


def _const_spec(shape):
    nd = len(shape)
    return pl.BlockSpec(shape, lambda *_: (0,) * nd, pipeline_mode=pl.Buffered(1))


def _rmsnorm(x, g):
    return x * lax.rsqrt(jnp.mean(x * x, axis=-1, keepdims=True) + EPS) * g


def _gelu(x):
    return 0.5 * x * (1.0 + jnp.tanh(math.sqrt(2.0 / math.pi) * (x + 0.044715 * (x * x * x))))


def _sigmoid(x):
    return 1.0 / (1.0 + jnp.exp(-x))


def _mm(a, b):
    return jnp.dot(a, b, preferred_element_type=F32)


def _mm_nt(a, b):
    return lax.dot_general(a, b, (((1,), (1,)), ((), ())), preferred_element_type=F32)


def _sortable_key(s):
    bits = pltpu.bitcast(s, I32)
    key = bits ^ ((bits >> 31) & 0x7FFFFFFF)
    return jnp.where(s == 0.0, 0, key)


def _split_key(key):
    hi = (key >> 16).astype(I16)
    lo = ((key & 0xFFFF) + MIN16).astype(I16)
    return hi, lo


def _row16(t):
    return jnp.broadcast_to(t, (PACKED_ROWS, t.shape[1])).astype(I16)


def _tree_sum(xs):
    while len(xs) > 1:
        xs = [a + b for a, b in zip(xs[0::2], xs[1::2])] + ([xs[-1]] if len(xs) % 2 else [])
    return xs[0]


def _pair_loop(n_pairs, chunk_fn, init):
    def body(p, carry):
        return chunk_fn(2 * p + 1, chunk_fn(2 * p, carry))
    return lax.fori_loop(0, n_pairs, body, init)


def _count16(ref, n_chunks, pred):
    one = jnp.ones((PACKED_ROWS, LANES), I16)
    zero = jnp.zeros((PACKED_ROWS, LANES), I16)
    hits = []
    for c in range(n_chunks):
        x = ref[c]
        for j in range(KEY_CHUNK // PACKED_ROWS):
            hits.append(jnp.where(pred(x[j * PACKED_ROWS:(j + 1) * PACKED_ROWS]), one, zero))
    return jnp.sum(_tree_sum(hits).astype(F32), axis=0, keepdims=True)


def _search16(ref, n_chunks, need):
    def step(s, t):
        cand = t + lax.shift_left(jnp.int32(1), 15 - s)
        cand16 = _row16(cand)
        cnt = _count16(ref, n_chunks, lambda x: x >= cand16)
        return jnp.where(cnt >= need, cand, t)

    return lax.fori_loop(0, 16, step, jnp.full((1, LANES), MIN16, I32))


def _in_proj_body(x_ref, g_ref, w_ref, q_ref, k_ref, v_ref, ki_ref, kbf_ref, vt_ref, kibf_ref,
                  qi_ref, kiwi_ref, wit_ref, rec_ref, gm_ref):
    n_chunks = kbf_ref.shape[0]
    h = _rmsnorm(x_ref[...], g_ref[...]).astype(MXU_DTYPE)

    def proj(lo, hi):
        return _mm(h, w_ref[:, lo:hi])

    q_ref[...] = (proj(C_Q, C_KV) * Q_SCALE).astype(MXU_DTYPE)
    kv = proj(C_KV, C_QI)
    k = kv[:, :D_KV]
    v = kv[:, D_KV:]
    k_ref[...] = k
    v_ref[...] = v
    vt = v.T
    kb = k.astype(MXU_DTYPE)
    qi_ref[...] = proj(C_QI, C_REC).astype(MXU_DTYPE)
    rec_ref[...] = proj(C_REC, C_GM)
    gm_ref[...] = proj(C_GM, C_KIWI)
    kiwi = proj(C_KIWI, D_IN_PAD)
    ki = kiwi[:, :IDX_DIM]
    ki_ref[...] = ki
    kiwi_ref[...] = kiwi
    kib = ki.astype(MXU_DTYPE)
    wit_ref[...] = kiwi.T[IDX_DIM:IDX_DIM + SUBLANES, :]
    for c in range(n_chunks):
        rows = slice(c * KEY_CHUNK, (c + 1) * KEY_CHUNK)
        kbf_ref[c] = kb[rows]
        kibf_ref[c] = kib[rows]
        vt_ref[c] = vt[:, rows].astype(MXU_DTYPE)


def _in_proj(x, g, w, tm):
    m = x.shape[0]
    nc = tm // KEY_CHUNK
    row = lambda n: pl.BlockSpec((tm, n), lambda i: (i, 0))
    chunked = lambda a, b: pl.BlockSpec((nc, a, b), lambda i: (i, 0, 0))
    out_shape = (
        jax.ShapeDtypeStruct((m, D_ATT), MXU_DTYPE),
        jax.ShapeDtypeStruct((m, D_KV), F32),
        jax.ShapeDtypeStruct((m, D_KV), F32),
        jax.ShapeDtypeStruct((m, IDX_DIM), F32),
        jax.ShapeDtypeStruct((m // KEY_CHUNK, KEY_CHUNK, D_KV), MXU_DTYPE),
        jax.ShapeDtypeStruct((m // KEY_CHUNK, D_KV, KEY_CHUNK), MXU_DTYPE),
        jax.ShapeDtypeStruct((m // KEY_CHUNK, KEY_CHUNK, IDX_DIM), MXU_DTYPE),
        jax.ShapeDtypeStruct((m, D_QI), MXU_DTYPE),
        jax.ShapeDtypeStruct((m, LANES), F32),
        jax.ShapeDtypeStruct((SUBLANES, m), F32),
        jax.ShapeDtypeStruct((m, 2 * D_REC), F32),
        jax.ShapeDtypeStruct((m, 2 * D_GM), F32),
    )
    out_specs = (
        row(D_ATT), row(D_KV), row(D_KV), row(IDX_DIM),
        chunked(KEY_CHUNK, D_KV), chunked(D_KV, KEY_CHUNK), chunked(KEY_CHUNK, IDX_DIM),
        row(D_QI), row(LANES), pl.BlockSpec((SUBLANES, tm), lambda i: (0, i)),
        row(2 * D_REC), row(2 * D_GM),
    )
    return pl.pallas_call(
        _in_proj_body,
        grid=(m // tm,),
        in_specs=[row(D_MODEL), _const_spec((1, D_MODEL)), _const_spec((D_MODEL, D_IN_PAD))],
        out_specs=out_specs,
        out_shape=out_shape,
        compiler_params=_cparams("arbitrary"),
        name="in_proj",
    )(x, g, w)


def _bias_table(rb_ref, head, delta, shape):
    row = lax.broadcasted_iota(I32, shape, 0)
    lane = lax.broadcasted_iota(I32, shape, 1)
    d = delta + lane - row
    far = rb_ref[NUM_BUCKETS - 1, head]
    val = jnp.full(shape, rb_ref[0, head] - far, F32)
    for b in range(1, NUM_BUCKETS - 1):
        val = jnp.where(d >= BUCKET_LO[b], rb_ref[b, head] - far, val)
    return jnp.where(d >= BUCKET_LO[NUM_BUCKETS - 1], 0.0, val)


def _prompt_attn_body(rb_ref, q_ref, qi_ref, wit_ref, ki_ref, k_ref, vt_ref, att_ref,
                      key_s, hi_s, lo_s, mb_s, tab_s, tri_s, thr_s, need_s, m_s, acc_s, qpad_s, *, topk):
    b = pl.program_id(0)
    i = pl.program_id(1)
    n_pairs = i // 4 + 1
    ck = (KEY_CHUNK, Q_BLOCK)

    @pl.when((b == 0) & (i == 0))
    def _init_tables():
        r = lax.broadcasted_iota(I32, (KEY_CHUNK, KEY_CHUNK), 0)
        c = lax.broadcasted_iota(I32, (KEY_CHUNK, KEY_CHUNK), 1)
        tri_s[...] = jnp.where(c <= r, 1.0, 0.0).astype(MXU_DTYPE)

        def per_head(h, carry):
            for ti in range(4):
                tab_s[ti, h] = _bias_table(rb_ref, h, ti * Q_BLOCK, ck)
            return carry

        lax.fori_loop(0, N_HEADS, per_head, 0)

    row = lax.broadcasted_iota(I32, ck, 0)
    lane = lax.broadcasted_iota(I32, ck, 1)
    q_pos = i * Q_BLOCK + lane

    qi = qi_ref[...]
    qis = jnp.concatenate([qi[:, h * IDX_DIM:(h + 1) * IDX_DIM] for h in range(N_IDX_HEADS)], axis=0)
    wit = wit_ref[...]
    w_row = jnp.concatenate([wit[h:h + 1, :] for h in range(N_IDX_HEADS)], axis=1)

    def score_chunk(c, carry):
        s4 = jnp.maximum(_mm_nt(ki_ref[c], qis), 0.0) * w_row
        s = _tree_sum([s4[:, h * Q_BLOCK:(h + 1) * Q_BLOCK] for h in range(N_IDX_HEADS)])
        valid = (c * KEY_CHUNK + row) <= q_pos
        key = jnp.where(valid, _sortable_key(s), INT_MIN)
        key_s[c] = key
        hi_s[c], lo_s[c] = _split_key(key)
        return carry

    _pair_loop(n_pairs, score_chunk, 0)

    max_pairs = key_s.shape[0] // 2
    odd = n_pairs % SEARCH_CLASS_PAIRS

    @pl.when((odd != 0) & (n_pairs < max_pairs))
    def _fill_class_padding():
        for c in (2 * n_pairs, 2 * n_pairs + 1):
            hi_s[c] = jnp.full(ck, MIN16, I16)
            lo_s[c] = jnp.full(ck, MIN16, I16)

    def search(n_chunks):
        t_hi = _search16(hi_s, n_chunks, topk)
        t_hi16 = _row16(t_hi)
        need_lo = topk - _count16(hi_s, n_chunks, lambda x: x > t_hi16)
        t_hi_chunk = jnp.broadcast_to(t_hi, ck).astype(I16)
        for c in range(n_chunks):
            lo_s[c] = jnp.where(hi_s[c] == t_hi_chunk, lo_s[c], MIN16)
        t_lo = _search16(lo_s, n_chunks, need_lo)
        t_lo16 = _row16(t_lo)
        thr_s[0:1, :] = t_hi * 65536 + (t_lo - MIN16)
        need_s[0:1, :] = need_lo - _count16(lo_s, n_chunks, lambda x: x > t_lo16)

    class_pairs = list(range(SEARCH_CLASS_PAIRS, max_pairs, SEARCH_CLASS_PAIRS)) + [max_pairs]
    for k, pairs in enumerate(class_pairs):
        lo_bound = class_pairs[k - 1] if k else 0
        pl.when((n_pairs > lo_bound) & (n_pairs <= pairs))(functools.partial(search, 2 * pairs))
    t = thr_s[0:1, :]
    need = need_s[0:1, :]

    tri = tri_s[...]

    def mask_chunk(c, carry):
        key = key_s[c]
        tie = (key == t) & (key > INT_MIN)
        tie_f = jnp.where(tie, 1.0, 0.0)
        incl = _mm(tri, tie_f.astype(MXU_DTYPE))
        rank = carry + incl - tie_f
        sel = (key > t) | (tie & (rank < need))
        mb_s[c] = jnp.where(sel, 0.0, -jnp.inf).astype(mb_s.dtype)
        return carry + incl[KEY_CHUNK - 1:KEY_CHUNK, :]

    _pair_loop(n_pairs, mask_chunk, jnp.zeros((1, Q_BLOCK), F32))

    q = q_ref[...]
    zeros = jnp.zeros((Q_BLOCK, HEAD_DIM), MXU_DTYPE)
    for h in range(N_HEADS):
        qh = q[:, h * HEAD_DIM:(h + 1) * HEAD_DIM]
        parts = [qh, zeros] if h < GROUP else [zeros, qh]
        qpad_s[h * Q_BLOCK:(h + 1) * Q_BLOCK, :] = jnp.concatenate(parts, axis=1)
    m_s[...] = jnp.full(m_s.shape, NEG_BIG, F32)
    acc_s[...] = jnp.zeros(acc_s.shape, F32)
    ones_rows = jnp.ones((PACKED_ROWS, 2 * KEY_CHUNK), MXU_DTYPE)

    def attend_pair(p, near):
        c0, c1 = 2 * p, 2 * p + 1
        keys = jnp.concatenate([k_ref[c0], k_ref[c1]], axis=0)
        logits = _mm_nt(keys, qpad_s[...])
        mb = jnp.concatenate([mb_s[c0], mb_s[c1]], axis=0)
        if near:
            t0 = jnp.clip(i - 2 * c0, 0, 3)
            t1 = jnp.clip(i - 2 * c1, 0, 3)
        for g in range(N_KV_HEADS):
            ps, alphas = [], []
            for hh in range(GROUP):
                h = g * GROUP + hh
                cols = slice(h * Q_BLOCK, (h + 1) * Q_BLOCK)
                lt = logits[:, cols]
                if near:
                    lt = lt + jnp.concatenate([tab_s[t0, h], tab_s[t1, h]], axis=0)
                lt = lt.astype(MXU_DTYPE) + mb
                m_old = m_s[0:1, cols]
                m_new = jnp.maximum(m_old, jnp.max(lt, axis=0, keepdims=True).astype(F32))
                alphas.append(jnp.exp(m_old - m_new))
                ps.append(jnp.exp(lt - m_new.astype(MXU_DTYPE)))
                m_s[0:1, cols] = m_new
            pg = jnp.concatenate(ps, axis=1)
            ag = jnp.concatenate(alphas, axis=1)
            rows = slice(g * HEAD_DIM, (g + 1) * HEAD_DIM)
            vt = jnp.concatenate([jnp.concatenate([vt_ref[c0, rows, :], vt_ref[c1, rows, :]], axis=1), ones_rows],
                                 axis=0)
            acc_s[g] = acc_s[g] * ag + _mm(vt, pg)

    def far_pair(p, carry):
        attend_pair(p, False)
        return carry

    lax.fori_loop(0, jnp.maximum(n_pairs - 2, 0), far_pair, 0)

    @pl.when(n_pairs >= 2)
    def _second_last():
        attend_pair(n_pairs - 2, True)

    attend_pair(n_pairs - 1, True)

    for h in range(N_HEADS):
        g, hh = divmod(h, GROUP)
        cols = slice(hh * Q_BLOCK, (hh + 1) * Q_BLOCK)
        o = acc_s[g][:HEAD_DIM, cols] / acc_s[g][HEAD_DIM:HEAD_DIM + 1, cols]
        att_ref[:, h * HEAD_DIM:(h + 1) * HEAD_DIM] = o.T.astype(att_ref.dtype)


def _prompt_attention(rel_bias, q, qi, wit, kibf, kbf, vtbf, batch, seq):
    nq = seq // Q_BLOCK
    nc = seq // KEY_CHUNK
    topk = float(min(TOPK_MAX, seq // 4))
    qrow = lambda n: pl.BlockSpec((Q_BLOCK, n), lambda b, i: (b * nq + i, 0))
    per_batch = lambda a, c: pl.BlockSpec((nc, a, c), lambda b, i: (b, 0, 0))
    return pl.pallas_call(
        functools.partial(_prompt_attn_body, topk=topk),
        grid=(batch, nq),
        in_specs=[
            pl.BlockSpec(memory_space=pltpu.SMEM),
            qrow(D_ATT), qrow(D_QI),
            pl.BlockSpec((SUBLANES, Q_BLOCK), lambda b, i: (0, b * nq + i)),
            per_batch(KEY_CHUNK, IDX_DIM), per_batch(KEY_CHUNK, D_KV), per_batch(D_KV, KEY_CHUNK),
        ],
        out_specs=qrow(D_ATT),
        out_shape=jax.ShapeDtypeStruct((batch * seq, D_ATT), MXU_DTYPE),
        scratch_shapes=[
            pltpu.VMEM((nc, KEY_CHUNK, Q_BLOCK), I32),
            pltpu.VMEM((nc, KEY_CHUNK, Q_BLOCK), I16),
            pltpu.VMEM((nc, KEY_CHUNK, Q_BLOCK), I16),
            pltpu.VMEM((nc, KEY_CHUNK, Q_BLOCK), MXU_DTYPE),
            pltpu.VMEM((4, N_HEADS, KEY_CHUNK, Q_BLOCK), F32),
            pltpu.VMEM((KEY_CHUNK, KEY_CHUNK), MXU_DTYPE),
            pltpu.VMEM((SUBLANES, Q_BLOCK), I32),
            pltpu.VMEM((SUBLANES, Q_BLOCK), F32),
            pltpu.VMEM((SUBLANES, N_HEADS * Q_BLOCK), F32),
            pltpu.VMEM((N_KV_HEADS, HEAD_DIM + PACKED_ROWS, GROUP * Q_BLOCK), F32),
            pltpu.VMEM((N_HEADS * Q_BLOCK, D_KV), MXU_DTYPE),
        ],
        compiler_params=_cparams("arbitrary", "arbitrary"),
        name="prompt_attn",
    )(rel_bias, q, qi, wit, kibf, kbf, vtbf)


def _sample_attn_body(pt_ref, rb_ref, q_ref, qi_ref, kiwi_ref, knew_ref, vnew_ref,
                      ck_hbm, cv_hbm, cki_hbm, att_ref,
                      ktbuf, vtbuf, kitbuf, sem, *, topk, n_pages, layer, t_new):
    b = pl.program_id(0)
    past = n_pages * PAGE_SIZE

    def page_copies(p):
        phys = pt_ref[b, p]
        cols = pl.ds(pl.multiple_of(p * PAGE_SIZE, PAGE_SIZE), PAGE_SIZE)
        return (pltpu.make_async_copy(ck_hbm.at[layer, phys], ktbuf.at[:, cols], sem.at[0]),
                pltpu.make_async_copy(cv_hbm.at[layer, phys], vtbuf.at[:, cols], sem.at[1]),
                pltpu.make_async_copy(cki_hbm.at[layer, phys], kitbuf.at[:, cols], sem.at[2]))

    def start_page(p, carry):
        for cp in page_copies(p):
            cp.start()
        return carry

    def wait_page(p, carry):
        for cp in page_copies(p):
            cp.wait()
        return carry

    lax.fori_loop(0, n_pages, start_page, 0)

    qi = qi_ref[...]
    qis = jnp.concatenate([qi[:, h * IDX_DIM:(h + 1) * IDX_DIM] for h in range(N_IDX_HEADS)], axis=0)
    kiwi = kiwi_ref[...]
    w_col = jnp.concatenate([kiwi[:, IDX_DIM + h:IDX_DIM + h + 1] for h in range(N_IDX_HEADS)], axis=0)
    q = q_ref[...]
    zeros = jnp.zeros((t_new, HEAD_DIM), MXU_DTYPE)
    qpad = jnp.concatenate(
        [jnp.concatenate([q[:, h * HEAD_DIM:(h + 1) * HEAD_DIM], zeros] if h < GROUP else
                         [zeros, q[:, h * HEAD_DIM:(h + 1) * HEAD_DIM]], axis=1)
         for h in range(N_HEADS)], axis=0)

    lax.fori_loop(0, n_pages, wait_page, 0)

    def idx_score(qk):
        s4 = jnp.maximum(qk, 0.0) * w_col
        s = s4[0:t_new]
        for h in range(1, N_IDX_HEADS):
            s = s + s4[h * t_new:(h + 1) * t_new]
        return s

    key_p = _sortable_key(idx_score(_mm(qis, kitbuf[...].astype(MXU_DTYPE))))
    ki_new = kiwi[:, :IDX_DIM].astype(MXU_DTYPE)
    ki_new = jnp.concatenate([ki_new, jnp.zeros((LANES - t_new, IDX_DIM), MXU_DTYPE)], axis=0)
    rown = lax.broadcasted_iota(I32, (t_new, LANES), 0)
    lanen = lax.broadcasted_iota(I32, (t_new, LANES), 1)
    valid_n = lanen <= rown
    key_n = jnp.where(valid_n, _sortable_key(idx_score(_mm_nt(qis, ki_new))), INT_MIN)

    def count(pred_fn):
        return (jnp.sum(jnp.where(pred_fn(key_p), 1.0, 0.0), axis=1, keepdims=True)
                + jnp.sum(jnp.where(pred_fn(key_n), 1.0, 0.0), axis=1, keepdims=True))

    def search_step(step, t):
        cand = t + lax.shift_left(jnp.int32(1), 31 - step)
        return jnp.where(count(lambda k: k >= cand) >= topk, cand, t)

    t = lax.fori_loop(0, 32, search_step, jnp.full((t_new, 1), INT_MIN, I32))
    need = topk - count(lambda k: k > t)

    r = lax.broadcasted_iota(I32, (KEY_CHUNK, KEY_CHUNK), 0)
    c = lax.broadcasted_iota(I32, (KEY_CHUNK, KEY_CHUNK), 1)
    tri = jnp.where(r <= c, 1.0, 0.0).astype(MXU_DTYPE)
    tie_p = key_p == t
    carry = jnp.zeros((t_new, 1), F32)
    mask_parts = []
    for ch in range(past // KEY_CHUNK):
        cols = slice(ch * KEY_CHUNK, (ch + 1) * KEY_CHUNK)
        tie_c = tie_p[:, cols]
        tie_f = jnp.where(tie_c, 1.0, 0.0)
        incl = _mm(tie_f.astype(MXU_DTYPE), tri)
        sel = (key_p[:, cols] > t) | (tie_c & ((carry + incl - tie_f) < need))
        mask_parts.append(jnp.where(sel, 0.0, -jnp.inf))
        carry = carry + incl[:, KEY_CHUNK - 1:KEY_CHUNK]
    mb_p = jnp.concatenate(mask_parts, axis=1)
    tie_n = (key_n == t) & valid_n
    tie_nf = jnp.where(tie_n, 1.0, 0.0)
    incl_n = _mm(tie_nf.astype(MXU_DTYPE), tri[:LANES, :LANES])
    sel_n = (key_n > t) | (tie_n & ((carry + incl_n - tie_nf) < need))
    mb_n = jnp.where(sel_n & valid_n, 0.0, -jnp.inf)

    lg_p = _mm(qpad, ktbuf[...].astype(MXU_DTYPE))
    k_new = jnp.concatenate([knew_ref[...].astype(MXU_DTYPE),
                             jnp.zeros((LANES - t_new, D_KV), MXU_DTYPE)], axis=0)
    lg_n = _mm_nt(qpad, k_new)
    near = slice(past - LANES, past)
    lp_rows, ln_rows = [], []
    for h in range(N_HEADS):
        rows = slice(h * t_new, (h + 1) * t_new)
        far = rb_ref[NUM_BUCKETS - 1, h]

        def bias(d):
            val = jnp.full(d.shape, rb_ref[0, h] - far, F32)
            for bk in range(1, NUM_BUCKETS - 1):
                val = jnp.where(d >= BUCKET_LO[bk], rb_ref[bk, h] - far, val)
            return jnp.where(d >= BUCKET_LO[NUM_BUCKETS - 1], 0.0, val)

        lp = lg_p[rows] + mb_p
        lp_near = lp[:, near] + bias(rown + (LANES - lanen))
        lp_rows.append(jnp.concatenate([lp[:, :past - LANES], lp_near], axis=1))
        ln_rows.append(lg_n[rows] + mb_n + bias(rown - lanen))
    lp_all = jnp.concatenate(lp_rows, axis=0)
    ln_all = jnp.concatenate(ln_rows, axis=0)
    m = jnp.maximum(jnp.max(lp_all, axis=1, keepdims=True), jnp.max(ln_all, axis=1, keepdims=True))
    p_p = jnp.exp(lp_all - m)
    p_n = jnp.exp(ln_all - m)
    denom = jnp.sum(p_p, axis=1, keepdims=True) + jnp.sum(p_n, axis=1, keepdims=True)
    v_new = jnp.concatenate([vnew_ref[...].astype(MXU_DTYPE),
                             jnp.zeros((LANES - t_new, D_KV), MXU_DTYPE)], axis=0)
    o = (_mm_nt(p_p.astype(MXU_DTYPE), vtbuf[...].astype(MXU_DTYPE))
         + _mm(p_n.astype(MXU_DTYPE), v_new)) / denom
    for h in range(N_HEADS):
        g = h // GROUP
        att_ref[:, h * HEAD_DIM:(h + 1) * HEAD_DIM] = (
            o[h * t_new:(h + 1) * t_new, g * HEAD_DIM:(g + 1) * HEAD_DIM].astype(att_ref.dtype))


def _sample_attention(page_table, rel_bias, q, qi, kiwi, k_new, v_new, cache_k, cache_v, cache_kidx,
                      layer, batch, t_new):
    n_pages = page_table.shape[1]
    past = n_pages * PAGE_SIZE
    topk = float(min(TOPK_MAX, (past + t_new) // 4))
    depth, n_pool = cache_k.shape[:2]
    ck = cache_k.transpose(0, 1, 3, 4, 2).reshape(depth, n_pool, D_KV, PAGE_SIZE)
    cv = cache_v.transpose(0, 1, 3, 4, 2).reshape(depth, n_pool, D_KV, PAGE_SIZE)
    cki = cache_kidx.transpose(0, 1, 3, 2)
    qrow = lambda n: pl.BlockSpec((t_new, n), lambda b, pt: (b, 0))
    hbm = pl.BlockSpec(memory_space=pl.ANY)
    grid_spec = pltpu.PrefetchScalarGridSpec(
        num_scalar_prefetch=1,
        grid=(batch,),
        in_specs=[pl.BlockSpec(memory_space=pltpu.SMEM),
                  qrow(D_ATT), qrow(D_QI), qrow(LANES), qrow(D_KV), qrow(D_KV), hbm, hbm, hbm],
        out_specs=qrow(D_ATT),
        scratch_shapes=[
            pltpu.VMEM((D_KV, past), F32),
            pltpu.VMEM((D_KV, past), F32),
            pltpu.VMEM((IDX_DIM, past), F32),
            pltpu.SemaphoreType.DMA((3,)),
        ],
    )
    return pl.pallas_call(
        functools.partial(_sample_attn_body, topk=topk, n_pages=n_pages, layer=layer, t_new=t_new),
        grid_spec=grid_spec,
        out_shape=jax.ShapeDtypeStruct((batch * t_new, D_ATT), MXU_DTYPE),
        compiler_params=_cparams("arbitrary"),
        name="sample_attn",
    )(page_table, rel_bias, q, qi, kiwi, k_new, v_new, ck, cv, cki)


def _rec_body(rec_ref, hist_ref, h0_ref, cw_ref, cb_ref, wa_ref, ba_ref, wx_ref, bx_ref, lam_ref,
              out_ref, hlast_ref, cstate_ref, hcar, xprev):
    tt = rec_ref.shape[0]

    @pl.when(pl.program_id(1) == 0)
    def _load_state():
        hcar[...] = h0_ref[0]
        xprev[...] = hist_ref[0]

    rx = rec_ref[:, :D_REC]
    rg = rec_ref[:, D_REC:]
    ext = jnp.concatenate([xprev[...], rx], axis=0)
    xc = cb_ref[...]
    for j in range(REC_CONV_W - 1):
        xc = xc + pltpu.roll(ext, REC_CONV_W - 1 - j, 0)[SUBLANES:] * cw_ref[j:j + 1, :]
    xc = xc + rx * cw_ref[REC_CONV_W - 1:REC_CONV_W, :]
    xcb = xc.astype(MXU_DTYPE)
    r = _sigmoid(_mm(xcb, wa_ref[...]) + ba_ref[...])
    gi = _sigmoid(_mm(xcb, wx_ref[...]) + bx_ref[...])
    nl = -lam_ref[...]
    softplus = jnp.maximum(nl, 0.0) + jnp.log1p(jnp.exp(-jnp.abs(nl)))
    log_a = -LRU_C * r * softplus
    a = jnp.exp(log_a)
    u = jnp.sqrt(-jnp.tanh(log_a) * (a * a + 1.0)) * (gi * xc)
    row = lax.broadcasted_iota(I32, (tt, D_REC), 0)
    s = 1
    while s < tt:
        keep = row >= s
        u = jnp.where(keep, u + a * pltpu.roll(u, s, 0), u)
        a = jnp.where(keep, a * pltpu.roll(a, s, 0), a)
        s *= 2
    hs = u + a * hcar[0:1, :]
    hcar[...] = jnp.broadcast_to(hs[tt - 1:tt, :], hcar.shape)
    xprev[...] = rx[tt - SUBLANES:, :]
    out_ref[...] = (_gelu(rg) * hs).astype(out_ref.dtype)
    hlast_ref[0] = hs[tt - SUBLANES:, :]
    cstate_ref[0] = rx[tt - SUBLANES:, :]


def _rec_branch(rec, hist8, h08, cw, cb, wa, ba, wx, bx, lam, batch, t_len, tt):
    nt = t_len // tt
    state = pl.BlockSpec((1, SUBLANES, D_REC), lambda b, t: (b, 0, 0))
    vec = _const_spec((1, D_REC))
    return pl.pallas_call(
        _rec_body,
        grid=(batch, nt),
        in_specs=[pl.BlockSpec((tt, 2 * D_REC), lambda b, t: (b * nt + t, 0)), state, state,
                  _const_spec((REC_CONV_W, D_REC)), vec, _const_spec((D_REC, D_REC)), vec,
                  _const_spec((D_REC, D_REC)), vec, vec],
        out_specs=(pl.BlockSpec((tt, D_REC), lambda b, t: (b * nt + t, 0)), state, state),
        out_shape=(jax.ShapeDtypeStruct((batch * t_len, D_REC), MXU_DTYPE),
                   jax.ShapeDtypeStruct((batch, SUBLANES, D_REC), F32),
                   jax.ShapeDtypeStruct((batch, SUBLANES, D_REC), F32)),
        scratch_shapes=[pltpu.VMEM((SUBLANES, D_REC), F32), pltpu.VMEM((SUBLANES, D_REC), F32)],
        compiler_params=_cparams("arbitrary", "arbitrary"),
        name="rec_branch",
    )(rec, hist8, h08, cw, cb, wa, ba, wx, bx, lam)


def _gmlp_body(gm_ref, wmix_ref, bias_ref, lng_ref, lnb_ref, out_ref, vn_ref):
    rows = wmix_ref.shape[1]
    n_sub = gm_ref.shape[0] // rows
    head_of_lane = lax.broadcasted_iota(I32, (rows, D_GM), 1) // GM_HEAD_DIM
    for sb in range(n_sub):
        sl = slice(sb * rows, (sb + 1) * rows)
        u = _gelu(gm_ref[sl, :D_GM])
        gv = _gelu(gm_ref[sl, D_GM:])
        xc = gv - jnp.mean(gv, axis=-1, keepdims=True)
        var = jnp.mean(xc * xc, axis=-1, keepdims=True)
        vn = xc * lax.rsqrt(var + EPS) * lng_ref[...] + lnb_ref[...]
        vn_ref[sl, :] = vn
        vnb = vn.astype(MXU_DTYPE)
        mix = jnp.zeros((rows, D_GM), F32)
        for g in range(N_GM_HEADS):
            mix = jnp.where(head_of_lane == g, _mm(wmix_ref[g], vnb), mix)
        out_ref[sl, :] = (u * (mix + bias_ref[...])).astype(out_ref.dtype)


def _gmlp(gm, wmix, bias, lng, lnb, tm):
    m = gm.shape[0]
    rows = wmix.shape[1]
    return pl.pallas_call(
        _gmlp_body,
        grid=(m // tm,),
        in_specs=[pl.BlockSpec((tm, 2 * D_GM), lambda i: (i, 0)),
                  _const_spec((N_GM_HEADS, rows, rows)), _const_spec((rows, D_GM)),
                  _const_spec((1, D_GM)), _const_spec((1, D_GM))],
        out_specs=(pl.BlockSpec((tm, D_GM), lambda i: (i, 0)), pl.BlockSpec((tm, D_GM), lambda i: (i, 0))),
        out_shape=(jax.ShapeDtypeStruct((m, D_GM), MXU_DTYPE), jax.ShapeDtypeStruct((m, D_GM), F32)),
        compiler_params=_cparams("arbitrary"),
        name="gmlp",
    )(gm, wmix, bias, lng, lnb)


def _out_proj_body(x_ref, att_ref, rec_ref, gm_ref, w_ref, g_ref, x1_ref, h2_ref):
    y = (_mm(att_ref[...], w_ref[0:D_ATT, :])
         + _mm(rec_ref[...], w_ref[D_ATT:D_ATT + D_REC, :])
         + _mm(gm_ref[...], w_ref[D_ATT + D_REC:, :]))
    x1 = x_ref[...] + y
    x1_ref[...] = x1
    h2_ref[...] = _rmsnorm(x1, g_ref[...]).astype(h2_ref.dtype)


def _out_proj(x, att, rec, gm, w, g, tm):
    m = x.shape[0]
    row = lambda n: pl.BlockSpec((tm, n), lambda i: (i, 0))
    return pl.pallas_call(
        _out_proj_body,
        grid=(m // tm,),
        in_specs=[row(D_MODEL), row(D_ATT), row(D_REC), row(D_GM),
                  _const_spec((D_MODEL, D_MODEL)), _const_spec((1, D_MODEL))],
        out_specs=(row(D_MODEL), row(D_MODEL)),
        out_shape=(jax.ShapeDtypeStruct((m, D_MODEL), F32), jax.ShapeDtypeStruct((m, D_MODEL), MXU_DTYPE)),
        compiler_params=_cparams("arbitrary"),
        name="out_proj",
    )(x, att, rec, gm, w, g)


FF_TILE = 256
HALO = 16


def _ffn_finish(x1, acc, gf_ref, out_ref, final):
    x2 = x1 + acc
    out_ref[...] = _rmsnorm(x2, gf_ref[...]) if final else x2


def _ffn_prompt_body(h_ref, halo_ref, x1_ref, hist_ref, wup_ref, cw_ref, cb_ref, wdn_ref, gf_ref,
                     out_ref, upst_ref, act_s, *, tiles_per_seq, final):
    tm = h_ref.shape[0]
    first = (pl.program_id(0) % tiles_per_seq) == 0
    h_ext = jnp.concatenate([halo_ref[...], h_ref[...]], axis=0)
    for j in range(D_FF // FF_TILE):
        halves = []
        for base in (0, D_FF):
            cols = slice(base + j * FF_TILE, base + (j + 1) * FF_TILE)
            up = _mm(h_ext, wup_ref[:, cols])
            up_m = up[HALO:]
            upst_ref[0, :, cols] = up_m[tm - HALO:, :]
            ext = jnp.concatenate([jnp.where(first, hist_ref[0, :, cols], up[:HALO]), up_m], axis=0)
            uc = cb_ref[:, cols]
            for jj in range(FFN_CONV_W - 1):
                uc = uc + pltpu.roll(ext, FFN_CONV_W - 1 - jj, 0)[HALO:] * cw_ref[jj:jj + 1, cols]
            halves.append(uc + up_m * cw_ref[FFN_CONV_W - 1:FFN_CONV_W, cols])
        act_s[:, j * FF_TILE:(j + 1) * FF_TILE] = (_gelu(halves[0]) * halves[1]).astype(MXU_DTYPE)
    _ffn_finish(x1_ref[...], _mm(act_s[...], wdn_ref[...]), gf_ref, out_ref, final)


def _ffn_prompt(h2, x1, hist16, wup, cw, cb, wdn, gf, batch, seq, tm, final):
    m = h2.shape[0]
    tps = seq // tm
    row = lambda n: pl.BlockSpec((tm, n), lambda i: (i, 0))
    state = pl.BlockSpec((1, HALO, 2 * D_FF), lambda i: (i // tps, 0, 0))
    return pl.pallas_call(
        functools.partial(_ffn_prompt_body, tiles_per_seq=tps, final=final),
        grid=(m // tm,),
        in_specs=[row(D_MODEL),
                  pl.BlockSpec((HALO, D_MODEL), lambda i: (jnp.maximum(i * (tm // HALO) - 1, 0), 0)),
                  row(D_MODEL), state,
                  _const_spec((D_MODEL, 2 * D_FF)), _const_spec((FFN_CONV_W, 2 * D_FF)),
                  _const_spec((1, 2 * D_FF)), _const_spec((D_FF, D_MODEL)), _const_spec((1, D_MODEL))],
        out_specs=(row(D_MODEL), state),
        out_shape=(jax.ShapeDtypeStruct((m, D_MODEL), F32),
                   jax.ShapeDtypeStruct((batch, HALO, 2 * D_FF), F32)),
        scratch_shapes=[pltpu.VMEM((tm, D_FF), MXU_DTYPE)],
        compiler_params=_cparams("arbitrary"),
        name="ffn_prompt",
    )(h2, h2, x1, hist16, wup, cw, cb, wdn, gf)


def _ffn_sample_body(h_ref, x1_ref, p1_ref, p2_ref, wup_ref, cw_ref, cb_ref, wdn_ref, gf_ref,
                     out_ref, up_ref, *, t_new, final):
    tm = h_ref.shape[0]
    hm = h_ref[...]
    pos = lax.broadcasted_iota(I32, (tm, FF_TILE), 0) % t_new
    acc = jnp.zeros((tm, D_MODEL), F32)
    for j in range(D_FF // FF_TILE):
        halves = []
        for base in (0, D_FF):
            cols = slice(base + j * FF_TILE, base + (j + 1) * FF_TILE)
            up = _mm(hm, wup_ref[:, cols])
            up_ref[:, cols] = up
            tap2 = jnp.where(pos < 2, p2_ref[:, cols], pltpu.roll(up, 2, 0))
            tap1 = jnp.where(pos < 1, p1_ref[:, cols], pltpu.roll(up, 1, 0))
            halves.append(cb_ref[:, cols] + tap2 * cw_ref[0:1, cols] + tap1 * cw_ref[1:2, cols]
                          + up * cw_ref[2:3, cols])
        act = (_gelu(halves[0]) * halves[1]).astype(MXU_DTYPE)
        acc = acc + _mm(act, wdn_ref[j * FF_TILE:(j + 1) * FF_TILE, :])
    _ffn_finish(x1_ref[...], acc, gf_ref, out_ref, final)


def _ffn_sample(h2, x1, p1, p2, wup, cw, cb, wdn, gf, t_new, final):
    m = h2.shape[0]
    full = lambda n: _const_spec((m, n))
    return pl.pallas_call(
        functools.partial(_ffn_sample_body, t_new=t_new, final=final),
        grid=(1,),
        in_specs=[full(D_MODEL), full(D_MODEL), full(2 * D_FF), full(2 * D_FF),
                  _const_spec((D_MODEL, 2 * D_FF)), _const_spec((FFN_CONV_W, 2 * D_FF)),
                  _const_spec((1, 2 * D_FF)), _const_spec((D_FF, D_MODEL)), _const_spec((1, D_MODEL))],
        out_specs=(full(D_MODEL), full(2 * D_FF)),
        out_shape=(jax.ShapeDtypeStruct((m, D_MODEL), F32), jax.ShapeDtypeStruct((m, 2 * D_FF), F32)),
        compiler_params=_cparams("arbitrary"),
        name="ffn_sample",
    )(h2, x1, p1, p2, wup, cw, cb, wdn, gf)


def _block_diag(w):
    n, blk, _ = w.shape
    out = jnp.zeros((n * blk, n * blk), w.dtype)
    for i in range(n):
        out = out.at[i * blk:(i + 1) * blk, i * blk:(i + 1) * blk].set(w[i])
    return out


def _pad_rows_front(a, rows):
    pad = rows - a.shape[1]
    return jnp.pad(a, ((0, 0), (pad, 0), (0, 0)))


def _layer_weights(l, g_mix, w_in, rec_conv_w, rec_conv_b, lru_wa, lru_ba, lru_wx, lru_bx, lru_lam,
                   gm_ln_g, gm_ln_b, gm_ws, gm_bs, w_out, g_ffn, w_up, ffn_conv_w, ffn_conv_b, w_down):
    w = w_in[l]
    c_ki = D_ATT + 2 * D_KV + D_QI
    c_rx = c_ki + IDX_DIM + N_IDX_HEADS
    w_r = jnp.concatenate([w[:, :c_ki], w[:, c_rx:], w[:, c_ki:c_rx],
                           jnp.zeros((D_MODEL, LANES - IDX_DIM - N_IDX_HEADS), w.dtype)], axis=1)
    tril = jnp.tril(jnp.ones((CHUNK, CHUNK), gm_ws.dtype))
    row2 = lambda a: a.reshape(1, -1)
    return dict(
        g_mix=row2(g_mix[l]), w_in=w_r.astype(MXU_DTYPE),
        rec_cw=rec_conv_w[l], rec_cb=row2(rec_conv_b[l]),
        wa=_block_diag(lru_wa[l]).astype(MXU_DTYPE), ba=row2(lru_ba[l]),
        wx=_block_diag(lru_wx[l]).astype(MXU_DTYPE), bx=row2(lru_bx[l]), lam=row2(lru_lam[l]),
        ln_g=row2(gm_ln_g[l]), ln_b=row2(gm_ln_b[l]), gm_w=gm_ws[l] * tril, gm_b=gm_bs[l],
        w_out=w_out[l].astype(MXU_DTYPE), g_ffn=row2(g_ffn[l]),
        w_up=w_up[l].astype(MXU_DTYPE), ffn_cw=ffn_conv_w[l], ffn_cb=row2(ffn_conv_b[l]),
        w_down=w_down[l].astype(MXU_DTYPE))


def _gmlp_mix_weights(p, c, n_seq):
    w = p['gm_w'][:, :c, :c]
    if n_seq > 1:
        w = jnp.einsum('ab,gts->gatbs', jnp.eye(n_seq, dtype=w.dtype), w).reshape(
            N_GM_HEADS, n_seq * c, n_seq * c)
    bias = jnp.repeat(p['gm_b'][:, :c].T, GM_HEAD_DIM, axis=1)
    return w.astype(MXU_DTYPE), jnp.tile(bias, (n_seq, 1))


def _prompt_layer(x, p, rel_bias, g_final, batch, seq, final):
    tm = 512
    (q, k, v, ki, kbf, vtbf, kibf, qi, _, wit, rec, gm) = _in_proj(x, p['g_mix'], p['w_in'], tm)
    att = _prompt_attention(rel_bias, q, qi, wit, kibf, kbf, vtbf, batch, seq)
    zeros8 = jnp.zeros((batch, SUBLANES, D_REC), F32)
    rec_out, hlast, cstate = _rec_branch(rec, zeros8, zeros8, p['rec_cw'], p['rec_cb'], p['wa'], p['ba'],
                                         p['wx'], p['bx'], p['lam'], batch, seq, 256)
    c = min(CHUNK, seq)
    wmix, bias = _gmlp_mix_weights(p, c, 1)
    gm_out, _ = _gmlp(gm, wmix, bias, p['ln_g'], p['ln_b'], tm)
    x1, h2 = _out_proj(x, att, rec_out, gm_out, p['w_out'], p['g_ffn'], tm)
    hist16 = jnp.zeros((batch, HALO, 2 * D_FF), F32)
    x2, upst = _ffn_prompt(h2, x1, hist16, p['w_up'], p['ffn_cw'], p['ffn_cb'], p['w_down'], g_final,
                           batch, seq, tm, final)
    state = (k.reshape(batch, seq, N_KV_HEADS, HEAD_DIM), v.reshape(batch, seq, N_KV_HEADS, HEAD_DIM),
             ki.reshape(batch, seq, IDX_DIM), hlast[:, SUBLANES - 1, :],
             cstate[:, SUBLANES - (REC_CONV_W - 1):, :], upst[:, HALO - (FFN_CONV_W - 1):, :])
    return x2, state


def _sample_layer(x, p, rel_bias, g_final, l, batch, t_new, final, page_table, cache_k, cache_v, cache_kidx,
                  h0, rec_hist, ffn_hist):
    m = batch * t_new
    (q, k, v, ki, _, _, _, qi, kiwi, _, rec, gm) = _in_proj(x, p['g_mix'], p['w_in'], m)
    att = _sample_attention(page_table, rel_bias, q, qi, kiwi, k, v, cache_k, cache_v, cache_kidx,
                            l, batch, t_new)
    hist8 = _pad_rows_front(rec_hist, SUBLANES)
    h08 = jnp.broadcast_to(h0[:, None, :], (batch, SUBLANES, D_REC))
    rec_out, hlast, cstate = _rec_branch(rec, hist8, h08, p['rec_cw'], p['rec_cb'], p['wa'], p['ba'],
                                         p['wx'], p['bx'], p['lam'], batch, t_new, t_new)
    c = min(CHUNK, t_new)
    wmix, bias = _gmlp_mix_weights(p, c, m // c)
    gm_out, vn = _gmlp(gm, wmix, bias, p['ln_g'], p['ln_b'], m)
    x1, h2 = _out_proj(x, att, rec_out, gm_out, p['w_out'], p['g_ffn'], m)
    zrow = jnp.zeros((batch, t_new - 2, 2 * D_FF), F32)
    p2 = jnp.concatenate([ffn_hist, zrow], axis=1).reshape(m, 2 * D_FF)
    p1 = jnp.concatenate([ffn_hist[:, 1:], zrow, zrow[:, :1]], axis=1).reshape(m, 2 * D_FF)
    x2, up = _ffn_sample(h2, x1, p1, p2, p['w_up'], p['ffn_cw'], p['ffn_cb'], p['w_down'], g_final,
                         t_new, final)
    state = (k.reshape(batch, t_new, N_KV_HEADS, HEAD_DIM), v.reshape(batch, t_new, N_KV_HEADS, HEAD_DIM),
             ki.reshape(batch, t_new, IDX_DIM), hlast[:, SUBLANES - 1, :],
             cstate[:, SUBLANES - (REC_CONV_W - 1):, :],
             up.reshape(batch, t_new, 2 * D_FF)[:, t_new - (FFN_CONV_W - 1):, :],
             vn.reshape(batch, t_new, D_GM))
    return x2, state


def kernel(x_prompt, x_sample, cache_k, cache_v, cache_kidx, page_table, state_lru_h, state_conv_rec,
           state_conv_ffn, rel_bias, g_mix, w_in, rec_conv_w, rec_conv_b, lru_wa, lru_ba, lru_wx, lru_bx,
           lru_lam, gm_ln_g, gm_ln_b, gm_ws, gm_bs, w_out, g_ffn, w_up, ffn_conv_w, ffn_conv_b, w_down,
           g_final):
    batch, seq, _ = x_prompt.shape
    dec_batch, t_new, _ = x_sample.shape
    depth = w_in.shape[0]
    assert seq % 512 == 0 and t_new == SUBLANES and (dec_batch * t_new) % KEY_CHUNK == 0
    xp = x_prompt.reshape(batch * seq, D_MODEL)
    xs = x_sample.reshape(dec_batch * t_new, D_MODEL)
    gf = g_final.reshape(1, D_MODEL)
    p_states, s_states = [], []
    for l in range(depth):
        p = _layer_weights(l, g_mix, w_in, rec_conv_w, rec_conv_b, lru_wa, lru_ba, lru_wx, lru_bx, lru_lam,
                           gm_ln_g, gm_ln_b, gm_ws, gm_bs, w_out, g_ffn, w_up, ffn_conv_w, ffn_conv_b, w_down)
        final = l == depth - 1
        xp, st = _prompt_layer(xp, p, rel_bias, gf, batch, seq, final)
        p_states.append(st)
        xs, st = _sample_layer(xs, p, rel_bias, gf, l, dec_batch, t_new, final, page_table, cache_k, cache_v,
                               cache_kidx, state_lru_h[l], state_conv_rec[l], state_conv_ffn[l])
        s_states.append(st)
    stack = lambda states, i: jnp.stack([s[i] for s in states])
    return ((xp.reshape(batch, seq, D_MODEL), xs.reshape(dec_batch, t_new, D_MODEL))
            + tuple(stack(p_states, i) for i in range(6))
            + tuple(stack(s_states, i) for i in range(7)))
```

```python
import functools
import math

import jax
import jax.numpy as jnp
from jax import lax
from jax.experimental import pallas as pl
from jax.experimental.pallas import tpu as pltpu

F32 = jnp.float32
I32 = jnp.int32
MXU_DTYPE = jnp.bfloat16

D_MODEL = 1024
N_HEADS = 8
HEAD_DIM = 64
D_ATT = N_HEADS * HEAD_DIM
N_KV_HEADS = 2
GROUP = N_HEADS // N_KV_HEADS
D_KV = N_KV_HEADS * HEAD_DIM
N_IDX_HEADS = 4
IDX_DIM = 64
D_QI = N_IDX_HEADS * IDX_DIM
TOPK_MAX = 256
NUM_BUCKETS = 32
MAX_EXACT = NUM_BUCKETS // 2
MAX_DISTANCE = 128
D_REC = 256
N_REC_BLOCKS = 4
REC_CONV_W = 4
LRU_C = 8.0
D_GM = 256
N_GM_HEADS = 4
GM_HEAD_DIM = D_GM // N_GM_HEADS
CHUNK = 128
D_FF = 2816
FFN_CONV_W = 3
EPS = 1e-6
PAGE_SIZE = 128
Q_SCALE = HEAD_DIM ** -0.5

LANES = 128
SUBLANES = 8
VMEM_LIMIT_BYTES = 56 * 1024 * 1024

C_Q = 0
C_KV = C_Q + D_ATT
C_QI = C_KV + 2 * D_KV
C_REC = C_QI + D_QI
C_GM = C_REC + 2 * D_REC
C_KIWI = C_GM + 2 * D_GM
D_IN_PAD = C_KIWI + LANES

KEY_CHUNK = 256
Q_BLOCK = 128
INT_MIN = -2 ** 31
PACKED_ROWS = 2 * SUBLANES
NEG_BIG = -2.0 ** 100
SEARCH_CLASS_PAIRS = 2
COUNT_ACCUMULATORS = 8

BUCKET_LO = tuple(
    b if b <= MAX_EXACT else math.ceil(MAX_EXACT * (MAX_DISTANCE / MAX_EXACT) ** ((b - MAX_EXACT) / (NUM_BUCKETS - MAX_EXACT)))
    for b in range(NUM_BUCKETS))


def _cparams(*sem):
    return pltpu.CompilerParams(dimension_semantics=sem, vmem_limit_bytes=VMEM_LIMIT_BYTES)


def _const_spec(shape):
    nd = len(shape)
    return pl.BlockSpec(shape, lambda *_: (0,) * nd, pipeline_mode=pl.Buffered(1))


def _rmsnorm(x, g):
    return x * lax.rsqrt(jnp.mean(x * x, axis=-1, keepdims=True) + EPS) * g


def _gelu(x):
    return 0.5 * x * (1.0 + jnp.tanh(math.sqrt(2.0 / math.pi) * (x + 0.044715 * (x * x * x))))


def _sigmoid(x):
    return 1.0 / (1.0 + jnp.exp(-x))


def _mm(a, b):
    return jnp.dot(a, b, preferred_element_type=F32)


def _mm_nt(a, b):
    return lax.dot_general(a, b, (((1,), (1,)), ((), ())), preferred_element_type=F32)


def _pattern_to_f32(c):
    return pltpu.bitcast(jnp.where(c >= 0, c, c ^ 0x7FFFFFFF), F32)


def _tree_sum(xs):
    while len(xs) > 1:
        xs = [a + b for a, b in zip(xs[0::2], xs[1::2])] + ([xs[-1]] if len(xs) % 2 else [])
    return xs[0]


def _pair_loop(n_pairs, chunk_fn, init):
    def body(p, carry):
        return chunk_fn(2 * p + 1, chunk_fn(2 * p, carry))
    return lax.fori_loop(0, n_pairs, body, init)


def _count_above(ref, n_chunks, cand, strict):
    cb = jnp.broadcast_to(cand, (SUBLANES, LANES))
    accs = [jnp.zeros((SUBLANES, LANES), F32)] * COUNT_ACCUMULATORS
    for c in range(n_chunks):
        x = ref[c]
        for j in range(KEY_CHUNK // SUBLANES):
            blk = x[j * SUBLANES:(j + 1) * SUBLANES]
            k = j % COUNT_ACCUMULATORS
            accs[k] = accs[k] + jnp.where((blk > cb) if strict else (blk >= cb), 1.0, 0.0)
    return jnp.sum(_tree_sum(accs), axis=0, keepdims=True)


def _kth_largest(count_ge, need, shape):
    def step(i, t):
        cand = t + lax.shift_left(jnp.int32(1), 31 - i)
        return jnp.where(count_ge(_pattern_to_f32(cand)) >= need, cand, t)

    t = lax.fori_loop(0, 32, step, jnp.full(shape, INT_MIN, I32))
    return jnp.where(t == INT_MIN, -jnp.inf, _pattern_to_f32(t))


def _in_proj_body(x_ref, g_ref, w_ref, q_ref, k_ref, v_ref, ki_ref, kbf_ref, vt_ref, kibf_ref,
                  qi_ref, kiwi_ref, wit_ref, rec_ref, gm_ref):
    n_chunks = kbf_ref.shape[0]
    h = _rmsnorm(x_ref[...], g_ref[...]).astype(MXU_DTYPE)

    def proj(lo, hi):
        return _mm(h, w_ref[:, lo:hi])

    q_ref[...] = (proj(C_Q, C_KV) * Q_SCALE).astype(MXU_DTYPE)
    kv = proj(C_KV, C_QI)
    k = kv[:, :D_KV]
    v = kv[:, D_KV:]
    k_ref[...] = k
    v_ref[...] = v
    vt = v.T
    kb = k.astype(MXU_DTYPE)
    qi_ref[...] = proj(C_QI, C_REC).astype(MXU_DTYPE)
    rec_ref[...] = proj(C_REC, C_GM)
    gm_ref[...] = proj(C_GM, C_KIWI)
    kiwi = proj(C_KIWI, D_IN_PAD)
    ki = kiwi[:, :IDX_DIM]
    ki_ref[...] = ki
    kiwi_ref[...] = kiwi
    kib = ki.astype(MXU_DTYPE)
    wit_ref[...] = kiwi.T[IDX_DIM:IDX_DIM + SUBLANES, :]
    for c in range(n_chunks):
        rows = slice(c * KEY_CHUNK, (c + 1) * KEY_CHUNK)
        kbf_ref[c] = kb[rows]
        kibf_ref[c] = kib[rows]
        vt_ref[c] = vt[:, rows].astype(MXU_DTYPE)


def _in_proj(x, g, w, tm):
    m = x.shape[0]
    nc = tm // KEY_CHUNK
    row = lambda n: pl.BlockSpec((tm, n), lambda i: (i, 0))
    chunked = lambda a, b: pl.BlockSpec((nc, a, b), lambda i: (i, 0, 0))
    out_shape = (
        jax.ShapeDtypeStruct((m, D_ATT), MXU_DTYPE),
        jax.ShapeDtypeStruct((m, D_KV), F32),
        jax.ShapeDtypeStruct((m, D_KV), F32),
        jax.ShapeDtypeStruct((m, IDX_DIM), F32),
        jax.ShapeDtypeStruct((m // KEY_CHUNK, KEY_CHUNK, D_KV), MXU_DTYPE),
        jax.ShapeDtypeStruct((m // KEY_CHUNK, D_KV, KEY_CHUNK), MXU_DTYPE),
        jax.ShapeDtypeStruct((m // KEY_CHUNK, KEY_CHUNK, IDX_DIM), MXU_DTYPE),
        jax.ShapeDtypeStruct((m, D_QI), MXU_DTYPE),
        jax.ShapeDtypeStruct((m, LANES), F32),
        jax.ShapeDtypeStruct((SUBLANES, m), F32),
        jax.ShapeDtypeStruct((m, 2 * D_REC), F32),
        jax.ShapeDtypeStruct((m, 2 * D_GM), F32),
    )
    out_specs = (
        row(D_ATT), row(D_KV), row(D_KV), row(IDX_DIM),
        chunked(KEY_CHUNK, D_KV), chunked(D_KV, KEY_CHUNK), chunked(KEY_CHUNK, IDX_DIM),
        row(D_QI), row(LANES), pl.BlockSpec((SUBLANES, tm), lambda i: (0, i)),
        row(2 * D_REC), row(2 * D_GM),
    )
    return pl.pallas_call(
        _in_proj_body,
        grid=(m // tm,),
        in_specs=[row(D_MODEL), _const_spec((1, D_MODEL)), _const_spec((D_MODEL, D_IN_PAD))],
        out_specs=out_specs,
        out_shape=out_shape,
        compiler_params=_cparams("arbitrary"),
        name="in_proj",
    )(x, g, w)


def _bias_table(rb_ref, head, delta, shape):
    row = lax.broadcasted_iota(I32, shape, 0)
    lane = lax.broadcasted_iota(I32, shape, 1)
    d = delta + lane - row
    far = rb_ref[NUM_BUCKETS - 1, head]
    val = jnp.full(shape, rb_ref[0, head] - far, F32)
    for b in range(1, NUM_BUCKETS - 1):
        val = jnp.where(d >= BUCKET_LO[b], rb_ref[b, head] - far, val)
    return jnp.where(d >= BUCKET_LO[NUM_BUCKETS - 1], 0.0, val)


def _prompt_attn_body(rb_ref, q_ref, qi_ref, wit_ref, ki_ref, k_ref, vt_ref, att_ref,
                      s_s, mb_s, tab_s, tri_s, thr_s, need_s, m_s, acc_s, qpad_s, *, topk):
    b = pl.program_id(0)
    i = pl.program_id(1)
    n_pairs = i // 4 + 1
    ck = (KEY_CHUNK, Q_BLOCK)

    @pl.when((b == 0) & (i == 0))
    def _init_tables():
        r = lax.broadcasted_iota(I32, (KEY_CHUNK, KEY_CHUNK), 0)
        c = lax.broadcasted_iota(I32, (KEY_CHUNK, KEY_CHUNK), 1)
        tri_s[...] = jnp.where(c <= r, 1.0, 0.0).astype(MXU_DTYPE)

        def per_head(h, carry):
            for ti in range(4):
                tab_s[ti, h] = _bias_table(rb_ref, h, ti * Q_BLOCK, ck)
            return carry

        lax.fori_loop(0, N_HEADS, per_head, 0)

    row = lax.broadcasted_iota(I32, ck, 0)
    lane = lax.broadcasted_iota(I32, ck, 1)
    q_pos = i * Q_BLOCK + lane

    qi = qi_ref[...]
    qis = jnp.concatenate([qi[:, h * IDX_DIM:(h + 1) * IDX_DIM] for h in range(N_IDX_HEADS)], axis=0)
    wit = wit_ref[...]
    w_row = jnp.concatenate([wit[h:h + 1, :] for h in range(N_IDX_HEADS)], axis=1)

    def score_chunk(c, carry):
        s4 = jnp.maximum(_mm_nt(ki_ref[c], qis), 0.0) * w_row
        s = _tree_sum([s4[:, h * Q_BLOCK:(h + 1) * Q_BLOCK] for h in range(N_IDX_HEADS)])
        valid = (c * KEY_CHUNK + row) <= q_pos
        s_s[c] = jnp.where(valid, s, -jnp.inf)
        return carry

    _pair_loop(n_pairs, score_chunk, 0)

    max_pairs = s_s.shape[0] // 2
    odd = n_pairs % SEARCH_CLASS_PAIRS

    @pl.when((odd != 0) & (n_pairs < max_pairs))
    def _fill_class_padding():
        for c in (2 * n_pairs, 2 * n_pairs + 1):
            s_s[c] = jnp.full(ck, -jnp.inf, F32)

    def search(n_chunks):
        t = _kth_largest(lambda cand: _count_above(s_s, n_chunks, cand, False), topk, (1, Q_BLOCK))
        thr_s[0:1, :] = t
        need_s[0:1, :] = topk - _count_above(s_s, n_chunks, t, True)

    class_pairs = list(range(SEARCH_CLASS_PAIRS, max_pairs, SEARCH_CLASS_PAIRS)) + [max_pairs]
    for k, pairs in enumerate(class_pairs):
        lo_bound = class_pairs[k - 1] if k else 0
        pl.when((n_pairs > lo_bound) & (n_pairs <= pairs))(functools.partial(search, 2 * pairs))
    t = thr_s[0:1, :]
    need = need_s[0:1, :]

    tri = tri_s[...]

    def mask_chunk(c, carry):
        sc = s_s[c]
        tie = (sc == t) & (sc > -jnp.inf)
        tie_f = jnp.where(tie, 1.0, 0.0)
        incl = _mm(tri, tie_f.astype(MXU_DTYPE))
        rank = carry + incl - tie_f
        sel = (sc > t) | (tie & (rank < need))
        mb_s[c] = jnp.where(sel, 0.0, -jnp.inf).astype(mb_s.dtype)
        return carry + incl[KEY_CHUNK - 1:KEY_CHUNK, :]

    _pair_loop(n_pairs, mask_chunk, jnp.zeros((1, Q_BLOCK), F32))

    q = q_ref[...]
    zeros = jnp.zeros((Q_BLOCK, HEAD_DIM), MXU_DTYPE)
    for h in range(N_HEADS):
        qh = q[:, h * HEAD_DIM:(h + 1) * HEAD_DIM]
        parts = [qh, zeros] if h < GROUP else [zeros, qh]
        qpad_s[h * Q_BLOCK:(h + 1) * Q_BLOCK, :] = jnp.concatenate(parts, axis=1)
    m_s[...] = jnp.full(m_s.shape, NEG_BIG, F32)
    acc_s[...] = jnp.zeros(acc_s.shape, F32)
    ones_rows = jnp.ones((PACKED_ROWS, 2 * KEY_CHUNK), MXU_DTYPE)

    def attend_pair(p, near):
        c0, c1 = 2 * p, 2 * p + 1
        keys = jnp.concatenate([k_ref[c0], k_ref[c1]], axis=0)
        logits = _mm_nt(keys, qpad_s[...])
        mb = jnp.concatenate([mb_s[c0], mb_s[c1]], axis=0)
        if near:
            t0 = jnp.clip(i - 2 * c0, 0, 3)
            t1 = jnp.clip(i - 2 * c1, 0, 3)
        for g in range(N_KV_HEADS):
            ps, alphas = [], []
            for hh in range(GROUP):
                h = g * GROUP + hh
                cols = slice(h * Q_BLOCK, (h + 1) * Q_BLOCK)
                lt = logits[:, cols]
                if near:
                    lt = lt + jnp.concatenate([tab_s[t0, h], tab_s[t1, h]], axis=0)
                lt = lt.astype(MXU_DTYPE) + mb
                m_old = m_s[0:1, cols]
                m_new = jnp.maximum(m_old, jnp.max(lt, axis=0, keepdims=True).astype(F32))
                alphas.append(jnp.exp(m_old - m_new))
                ps.append(jnp.exp(lt - m_new.astype(MXU_DTYPE)))
                m_s[0:1, cols] = m_new
            pg = jnp.concatenate(ps, axis=1)
            ag = jnp.concatenate(alphas, axis=1)
            rows = slice(g * HEAD_DIM, (g + 1) * HEAD_DIM)
            vt = jnp.concatenate([jnp.concatenate([vt_ref[c0, rows, :], vt_ref[c1, rows, :]], axis=1), ones_rows],
                                 axis=0)
            acc_s[g] = acc_s[g] * ag + _mm(vt, pg)

    def far_pair(p, carry):
        attend_pair(2 * p, False)
        attend_pair(2 * p + 1, False)
        return carry

    n_far = jnp.maximum(n_pairs - 2, 0)
    lax.fori_loop(0, n_far // 2, far_pair, 0)

    @pl.when(n_far % 2 == 1)
    def _odd_far_pair():
        attend_pair(n_far - 1, False)

    @pl.when(n_pairs >= 2)
    def _near_pairs():
        attend_pair(n_pairs - 2, True)
        attend_pair(n_pairs - 1, True)

    @pl.when(n_pairs == 1)
    def _only_pair():
        attend_pair(0, True)

    for h in range(N_HEADS):
        g, hh = divmod(h, GROUP)
        cols = slice(hh * Q_BLOCK, (hh + 1) * Q_BLOCK)
        o = acc_s[g][:HEAD_DIM, cols] / acc_s[g][HEAD_DIM:HEAD_DIM + 1, cols]
        att_ref[:, h * HEAD_DIM:(h + 1) * HEAD_DIM] = o.T.astype(att_ref.dtype)


def _prompt_attention(rel_bias, q, qi, wit, kibf, kbf, vtbf, batch, seq):
    nq = seq // Q_BLOCK
    nc = seq // KEY_CHUNK
    topk = float(min(TOPK_MAX, seq // 4))
    qrow = lambda n: pl.BlockSpec((Q_BLOCK, n), lambda b, i: (b * nq + i, 0))
    per_batch = lambda a, c: pl.BlockSpec((nc, a, c), lambda b, i: (b, 0, 0))
    return pl.pallas_call(
        functools.partial(_prompt_attn_body, topk=topk),
        grid=(batch, nq),
        in_specs=[
            pl.BlockSpec(memory_space=pltpu.SMEM),
            qrow(D_ATT), qrow(D_QI),
            pl.BlockSpec((SUBLANES, Q_BLOCK), lambda b, i: (0, b * nq + i)),
            per_batch(KEY_CHUNK, IDX_DIM), per_batch(KEY_CHUNK, D_KV), per_batch(D_KV, KEY_CHUNK),
        ],
        out_specs=qrow(D_ATT),
        out_shape=jax.ShapeDtypeStruct((batch * seq, D_ATT), MXU_DTYPE),
        scratch_shapes=[
            pltpu.VMEM((nc, KEY_CHUNK, Q_BLOCK), F32),
            pltpu.VMEM((nc, KEY_CHUNK, Q_BLOCK), MXU_DTYPE),
            pltpu.VMEM((4, N_HEADS, KEY_CHUNK, Q_BLOCK), F32),
            pltpu.VMEM((KEY_CHUNK, KEY_CHUNK), MXU_DTYPE),
            pltpu.VMEM((SUBLANES, Q_BLOCK), F32),
            pltpu.VMEM((SUBLANES, Q_BLOCK), F32),
            pltpu.VMEM((SUBLANES, N_HEADS * Q_BLOCK), F32),
            pltpu.VMEM((N_KV_HEADS, HEAD_DIM + PACKED_ROWS, GROUP * Q_BLOCK), F32),
            pltpu.VMEM((N_HEADS * Q_BLOCK, D_KV), MXU_DTYPE),
        ],
        compiler_params=_cparams("arbitrary", "arbitrary"),
        name="prompt_attn",
    )(rel_bias, q, qi, wit, kibf, kbf, vtbf)


def _sample_attn_body(pt_ref, rb_ref, q_ref, qi_ref, kiwi_ref, knew_ref, vnew_ref,
                      ck_hbm, cv_hbm, cki_hbm, att_ref,
                      ktbuf, vtbuf, kitbuf, sem, *, topk, n_pages, layer, t_new):
    b = pl.program_id(0)
    past = n_pages * PAGE_SIZE
    slot = b % 2

    def page_copies(seq, buf, p):
        phys = pt_ref[seq, p]
        cols = pl.ds(pl.multiple_of(p * PAGE_SIZE, PAGE_SIZE), PAGE_SIZE)
        return (pltpu.make_async_copy(ck_hbm.at[layer, phys], ktbuf.at[buf, :, cols], sem.at[0, buf]),
                pltpu.make_async_copy(cv_hbm.at[layer, phys], vtbuf.at[buf, :, cols], sem.at[1, buf]),
                pltpu.make_async_copy(cki_hbm.at[layer, phys], kitbuf.at[buf, :, cols], sem.at[2, buf]))

    def start_pages(seq, buf):
        def body(p, carry):
            for cp in page_copies(seq, buf, p):
                cp.start()
            return carry
        lax.fori_loop(0, n_pages, body, 0)

    def wait_pages(seq, buf):
        def body(p, carry):
            for cp in page_copies(seq, buf, p):
                cp.wait()
            return carry
        lax.fori_loop(0, n_pages, body, 0)

    @pl.when(b == 0)
    def _first_fetch():
        start_pages(0, 0)

    @pl.when(b + 1 < pl.num_programs(0))
    def _prefetch_next():
        start_pages(b + 1, 1 - slot)

    qi = qi_ref[...]
    qis = jnp.concatenate([qi[:, h * IDX_DIM:(h + 1) * IDX_DIM] for h in range(N_IDX_HEADS)], axis=0)
    kiwi = kiwi_ref[...]
    w_col = jnp.concatenate([kiwi[:, IDX_DIM + h:IDX_DIM + h + 1] for h in range(N_IDX_HEADS)], axis=0)
    q = q_ref[...]
    zeros = jnp.zeros((t_new, HEAD_DIM), MXU_DTYPE)
    qpad = jnp.concatenate(
        [jnp.concatenate([q[:, h * HEAD_DIM:(h + 1) * HEAD_DIM], zeros] if h < GROUP else
                         [zeros, q[:, h * HEAD_DIM:(h + 1) * HEAD_DIM]], axis=1)
         for h in range(N_HEADS)], axis=0)

    wait_pages(b, slot)

    def idx_score(qk):
        s4 = jnp.maximum(qk, 0.0) * w_col
        s = s4[0:t_new]
        for h in range(1, N_IDX_HEADS):
            s = s + s4[h * t_new:(h + 1) * t_new]
        return s

    s_p = idx_score(_mm(qis, kitbuf[slot].astype(MXU_DTYPE)))
    ki_new = kiwi[:, :IDX_DIM].astype(MXU_DTYPE)
    ki_new = jnp.concatenate([ki_new, jnp.zeros((LANES - t_new, IDX_DIM), MXU_DTYPE)], axis=0)
    rown = lax.broadcasted_iota(I32, (t_new, LANES), 0)
    lanen = lax.broadcasted_iota(I32, (t_new, LANES), 1)
    valid_n = lanen <= rown
    s_n = jnp.where(valid_n, idx_score(_mm_nt(qis, ki_new)), -jnp.inf)

    def count(cand, strict):
        cmp = (lambda x: x > cand) if strict else (lambda x: x >= cand)
        return (jnp.sum(jnp.where(cmp(s_p), 1.0, 0.0), axis=1, keepdims=True)
                + jnp.sum(jnp.where(cmp(s_n), 1.0, 0.0), axis=1, keepdims=True))

    t = _kth_largest(lambda cand: count(cand, False), topk, (t_new, 1))
    need = topk - count(t, True)

    r = lax.broadcasted_iota(I32, (KEY_CHUNK, KEY_CHUNK), 0)
    c = lax.broadcasted_iota(I32, (KEY_CHUNK, KEY_CHUNK), 1)
    tri = jnp.where(r <= c, 1.0, 0.0).astype(MXU_DTYPE)
    tie_p = s_p == t
    carry = jnp.zeros((t_new, 1), F32)
    mask_parts = []
    for ch in range(past // KEY_CHUNK):
        cols = slice(ch * KEY_CHUNK, (ch + 1) * KEY_CHUNK)
        tie_c = tie_p[:, cols]
        tie_f = jnp.where(tie_c, 1.0, 0.0)
        incl = _mm(tie_f.astype(MXU_DTYPE), tri)
        sel = (s_p[:, cols] > t) | (tie_c & ((carry + incl - tie_f) < need))
        mask_parts.append(jnp.where(sel, 0.0, -jnp.inf))
        carry = carry + incl[:, KEY_CHUNK - 1:KEY_CHUNK]
    mb_p = jnp.concatenate(mask_parts, axis=1)
    tie_n = (s_n == t) & valid_n
    tie_nf = jnp.where(tie_n, 1.0, 0.0)
    incl_n = _mm(tie_nf.astype(MXU_DTYPE), tri[:LANES, :LANES])
    sel_n = (s_n > t) | (tie_n & ((carry + incl_n - tie_nf) < need))
    mb_n = jnp.where(sel_n & valid_n, 0.0, -jnp.inf)

    lg_p = _mm(qpad, ktbuf[slot].astype(MXU_DTYPE))
    k_new = jnp.concatenate([knew_ref[...].astype(MXU_DTYPE),
                             jnp.zeros((LANES - t_new, D_KV), MXU_DTYPE)], axis=0)
    lg_n = _mm_nt(qpad, k_new)
    near = slice(past - LANES, past)
    lp_rows, ln_rows = [], []
    for h in range(N_HEADS):
        rows = slice(h * t_new, (h + 1) * t_new)
        far = rb_ref[NUM_BUCKETS - 1, h]

        def bias(d):
            val = jnp.full(d.shape, rb_ref[0, h] - far, F32)
            for bk in range(1, NUM_BUCKETS - 1):
                val = jnp.where(d >= BUCKET_LO[bk], rb_ref[bk, h] - far, val)
            return jnp.where(d >= BUCKET_LO[NUM_BUCKETS - 1], 0.0, val)

        lp = lg_p[rows] + mb_p
        lp_near = lp[:, near] + bias(rown + (LANES - lanen))
        lp_rows.append(jnp.concatenate([lp[:, :past - LANES], lp_near], axis=1))
        ln_rows.append(lg_n[rows] + mb_n + bias(rown - lanen))
    lp_all = jnp.concatenate(lp_rows, axis=0)
    ln_all = jnp.concatenate(ln_rows, axis=0)
    m = jnp.maximum(jnp.max(lp_all, axis=1, keepdims=True), jnp.max(ln_all, axis=1, keepdims=True))
    p_p = jnp.exp(lp_all - m)
    p_n = jnp.exp(ln_all - m)
    denom = jnp.sum(p_p, axis=1, keepdims=True) + jnp.sum(p_n, axis=1, keepdims=True)
    v_new = jnp.concatenate([vnew_ref[...].astype(MXU_DTYPE),
                             jnp.zeros((LANES - t_new, D_KV), MXU_DTYPE)], axis=0)
    o = (_mm_nt(p_p.astype(MXU_DTYPE), vtbuf[slot].astype(MXU_DTYPE))
         + _mm(p_n.astype(MXU_DTYPE), v_new)) / denom
    for h in range(N_HEADS):
        g = h // GROUP
        att_ref[:, h * HEAD_DIM:(h + 1) * HEAD_DIM] = (
            o[h * t_new:(h + 1) * t_new, g * HEAD_DIM:(g + 1) * HEAD_DIM].astype(att_ref.dtype))


def _sample_attention(page_table, rel_bias, q, qi, kiwi, k_new, v_new, cache_k, cache_v, cache_kidx,
                      layer, batch, t_new):
    n_pages = page_table.shape[1]
    past = n_pages * PAGE_SIZE
    topk = float(min(TOPK_MAX, (past + t_new) // 4))
    depth, n_pool = cache_k.shape[:2]
    ck = cache_k.transpose(0, 1, 3, 4, 2).reshape(depth, n_pool, D_KV, PAGE_SIZE)
    cv = cache_v.transpose(0, 1, 3, 4, 2).reshape(depth, n_pool, D_KV, PAGE_SIZE)
    cki = cache_kidx.transpose(0, 1, 3, 2)
    qrow = lambda n: pl.BlockSpec((t_new, n), lambda b, pt: (b, 0))
    hbm = pl.BlockSpec(memory_space=pl.ANY)
    grid_spec = pltpu.PrefetchScalarGridSpec(
        num_scalar_prefetch=1,
        grid=(batch,),
        in_specs=[pl.BlockSpec(memory_space=pltpu.SMEM),
                  qrow(D_ATT), qrow(D_QI), qrow(LANES), qrow(D_KV), qrow(D_KV), hbm, hbm, hbm],
        out_specs=qrow(D_ATT),
        scratch_shapes=[
            pltpu.VMEM((2, D_KV, past), F32),
            pltpu.VMEM((2, D_KV, past), F32),
            pltpu.VMEM((2, IDX_DIM, past), F32),
            pltpu.SemaphoreType.DMA((3, 2)),
        ],
    )
    return pl.pallas_call(
        functools.partial(_sample_attn_body, topk=topk, n_pages=n_pages, layer=layer, t_new=t_new),
        grid_spec=grid_spec,
        out_shape=jax.ShapeDtypeStruct((batch * t_new, D_ATT), MXU_DTYPE),
        compiler_params=_cparams("arbitrary"),
        name="sample_attn",
    )(page_table, rel_bias, q, qi, kiwi, k_new, v_new, ck, cv, cki)


def _rec_body(rec_ref, hist_ref, h0_ref, cw_ref, cb_ref, wa_ref, ba_ref, wx_ref, bx_ref, lam_ref,
              out_ref, hlast_ref, cstate_ref, hcar, xprev):
    tt = rec_ref.shape[0]

    @pl.when(pl.program_id(1) == 0)
    def _load_state():
        hcar[...] = h0_ref[0]
        xprev[...] = hist_ref[0]

    rx = rec_ref[:, :D_REC]
    rg = rec_ref[:, D_REC:]
    ext = jnp.concatenate([xprev[...], rx], axis=0)
    xc = cb_ref[...]
    for j in range(REC_CONV_W - 1):
        xc = xc + pltpu.roll(ext, REC_CONV_W - 1 - j, 0)[SUBLANES:] * cw_ref[j:j + 1, :]
    xc = xc + rx * cw_ref[REC_CONV_W - 1:REC_CONV_W, :]
    xcb = xc.astype(MXU_DTYPE)
    r = _sigmoid(_mm(xcb, wa_ref[...]) + ba_ref[...])
    gi = _sigmoid(_mm(xcb, wx_ref[...]) + bx_ref[...])
    nl = -lam_ref[...]
    softplus = jnp.maximum(nl, 0.0) + jnp.log1p(jnp.exp(-jnp.abs(nl)))
    log_a = -LRU_C * r * softplus
    a = jnp.exp(log_a)
    u = jnp.sqrt(-jnp.tanh(log_a) * (a * a + 1.0)) * (gi * xc)
    row = lax.broadcasted_iota(I32, (tt, D_REC), 0)
    s = 1
    while s < tt:
        keep = row >= s
        u = jnp.where(keep, u + a * pltpu.roll(u, s, 0), u)
        a = jnp.where(keep, a * pltpu.roll(a, s, 0), a)
        s *= 2
    hs = u + a * hcar[0:1, :]
    hcar[...] = jnp.broadcast_to(hs[tt - 1:tt, :], hcar.shape)
    xprev[...] = rx[tt - SUBLANES:, :]
    out_ref[...] = (_gelu(rg) * hs).astype(out_ref.dtype)
    hlast_ref[0] = hs[tt - SUBLANES:, :]
    cstate_ref[0] = rx[tt - SUBLANES:, :]


def _rec_branch(rec, hist8, h08, cw, cb, wa, ba, wx, bx, lam, batch, t_len, tt):
    nt = t_len // tt
    state = pl.BlockSpec((1, SUBLANES, D_REC), lambda b, t: (b, 0, 0))
    vec = _const_spec((1, D_REC))
    return pl.pallas_call(
        _rec_body,
        grid=(batch, nt),
        in_specs=[pl.BlockSpec((tt, 2 * D_REC), lambda b, t: (b * nt + t, 0)), state, state,
                  _const_spec((REC_CONV_W, D_REC)), vec, _const_spec((D_REC, D_REC)), vec,
                  _const_spec((D_REC, D_REC)), vec, vec],
        out_specs=(pl.BlockSpec((tt, D_REC), lambda b, t: (b * nt + t, 0)), state, state),
        out_shape=(jax.ShapeDtypeStruct((batch * t_len, D_REC), MXU_DTYPE),
                   jax.ShapeDtypeStruct((batch, SUBLANES, D_REC), F32),
                   jax.ShapeDtypeStruct((batch, SUBLANES, D_REC), F32)),
        scratch_shapes=[pltpu.VMEM((SUBLANES, D_REC), F32), pltpu.VMEM((SUBLANES, D_REC), F32)],
        compiler_params=_cparams("arbitrary", "arbitrary"),
        name="rec_branch",
    )(rec, hist8, h08, cw, cb, wa, ba, wx, bx, lam)


def _gmlp_body(gm_ref, wmix_ref, bias_ref, lng_ref, lnb_ref, out_ref, vn_ref):
    rows = wmix_ref.shape[1]
    n_sub = gm_ref.shape[0] // rows
    head_of_lane = lax.broadcasted_iota(I32, (rows, D_GM), 1) // GM_HEAD_DIM
    for sb in range(n_sub):
        sl = slice(sb * rows, (sb + 1) * rows)
        u = _gelu(gm_ref[sl, :D_GM])
        gv = _gelu(gm_ref[sl, D_GM:])
        xc = gv - jnp.mean(gv, axis=-1, keepdims=True)
        var = jnp.mean(xc * xc, axis=-1, keepdims=True)
        vn = xc * lax.rsqrt(var + EPS) * lng_ref[...] + lnb_ref[...]
        vn_ref[sl, :] = vn
        vnb = vn.astype(MXU_DTYPE)
        mix = jnp.zeros((rows, D_GM), F32)
        for g in range(N_GM_HEADS):
            mix = jnp.where(head_of_lane == g, _mm(wmix_ref[g], vnb), mix)
        out_ref[sl, :] = (u * (mix + bias_ref[...])).astype(out_ref.dtype)


def _gmlp(gm, wmix, bias, lng, lnb, tm):
    m = gm.shape[0]
    rows = wmix.shape[1]
    return pl.pallas_call(
        _gmlp_body,
        grid=(m // tm,),
        in_specs=[pl.BlockSpec((tm, 2 * D_GM), lambda i: (i, 0)),
                  _const_spec((N_GM_HEADS, rows, rows)), _const_spec((rows, D_GM)),
                  _const_spec((1, D_GM)), _const_spec((1, D_GM))],
        out_specs=(pl.BlockSpec((tm, D_GM), lambda i: (i, 0)), pl.BlockSpec((tm, D_GM), lambda i: (i, 0))),
        out_shape=(jax.ShapeDtypeStruct((m, D_GM), MXU_DTYPE), jax.ShapeDtypeStruct((m, D_GM), F32)),
        compiler_params=_cparams("arbitrary"),
        name="gmlp",
    )(gm, wmix, bias, lng, lnb)


def _out_proj_body(x_ref, att_ref, rec_ref, gm_ref, w_ref, g_ref, x1_ref, h2_ref):
    y = (_mm(att_ref[...], w_ref[0:D_ATT, :])
         + _mm(rec_ref[...], w_ref[D_ATT:D_ATT + D_REC, :])
         + _mm(gm_ref[...], w_ref[D_ATT + D_REC:, :]))
    x1 = x_ref[...] + y
    x1_ref[...] = x1
    h2_ref[...] = _rmsnorm(x1, g_ref[...]).astype(h2_ref.dtype)


def _out_proj(x, att, rec, gm, w, g, tm):
    m = x.shape[0]
    row = lambda n: pl.BlockSpec((tm, n), lambda i: (i, 0))
    return pl.pallas_call(
        _out_proj_body,
        grid=(m // tm,),
        in_specs=[row(D_MODEL), row(D_ATT), row(D_REC), row(D_GM),
                  _const_spec((D_MODEL, D_MODEL)), _const_spec((1, D_MODEL))],
        out_specs=(row(D_MODEL), row(D_MODEL)),
        out_shape=(jax.ShapeDtypeStruct((m, D_MODEL), F32), jax.ShapeDtypeStruct((m, D_MODEL), MXU_DTYPE)),
        compiler_params=_cparams("arbitrary"),
        name="out_proj",
    )(x, att, rec, gm, w, g)


FF_TILE = 256
HALO = 16


def _ffn_finish(x1, acc, gf_ref, out_ref, final):
    x2 = x1 + acc
    out_ref[...] = _rmsnorm(x2, gf_ref[...]) if final else x2


def _ffn_prompt_body(h_ref, halo_ref, x1_ref, hist_ref, wup_ref, cw_ref, cb_ref, wdn_ref, gf_ref,
                     out_ref, upst_ref, act_s, *, tiles_per_seq, final):
    tm = h_ref.shape[0]
    first = (pl.program_id(0) % tiles_per_seq) == 0
    h_ext = jnp.concatenate([halo_ref[...], h_ref[...]], axis=0)
    for j in range(D_FF // FF_TILE):
        halves = []
        for base in (0, D_FF):
            cols = slice(base + j * FF_TILE, base + (j + 1) * FF_TILE)
            up = _mm(h_ext, wup_ref[:, cols])
            up_m = up[HALO:]
            upst_ref[0, :, cols] = up_m[tm - HALO:, :]
            ext = jnp.concatenate([jnp.where(first, hist_ref[0, :, cols], up[:HALO]), up_m], axis=0)
            uc = cb_ref[:, cols]
            for jj in range(FFN_CONV_W - 1):
                uc = uc + pltpu.roll(ext, FFN_CONV_W - 1 - jj, 0)[HALO:] * cw_ref[jj:jj + 1, cols]
            halves.append(uc + up_m * cw_ref[FFN_CONV_W - 1:FFN_CONV_W, cols])
        act_s[:, j * FF_TILE:(j + 1) * FF_TILE] = (_gelu(halves[0]) * halves[1]).astype(MXU_DTYPE)
    _ffn_finish(x1_ref[...], _mm(act_s[...], wdn_ref[...]), gf_ref, out_ref, final)


def _ffn_prompt(h2, x1, hist16, wup, cw, cb, wdn, gf, batch, seq, tm, final):
    m = h2.shape[0]
    tps = seq // tm
    row = lambda n: pl.BlockSpec((tm, n), lambda i: (i, 0))
    state = pl.BlockSpec((1, HALO, 2 * D_FF), lambda i: (i // tps, 0, 0))
    return pl.pallas_call(
        functools.partial(_ffn_prompt_body, tiles_per_seq=tps, final=final),
        grid=(m // tm,),
        in_specs=[row(D_MODEL),
                  pl.BlockSpec((HALO, D_MODEL), lambda i: (jnp.maximum(i * (tm // HALO) - 1, 0), 0)),
                  row(D_MODEL), state,
                  _const_spec((D_MODEL, 2 * D_FF)), _const_spec((FFN_CONV_W, 2 * D_FF)),
                  _const_spec((1, 2 * D_FF)), _const_spec((D_FF, D_MODEL)), _const_spec((1, D_MODEL))],
        out_specs=(row(D_MODEL), state),
        out_shape=(jax.ShapeDtypeStruct((m, D_MODEL), F32),
                   jax.ShapeDtypeStruct((batch, HALO, 2 * D_FF), F32)),
        scratch_shapes=[pltpu.VMEM((tm, D_FF), MXU_DTYPE)],
        compiler_params=_cparams("arbitrary"),
        name="ffn_prompt",
    )(h2, h2, x1, hist16, wup, cw, cb, wdn, gf)


def _ffn_sample_body(h_ref, x1_ref, p1_ref, p2_ref, wup_ref, cw_ref, cb_ref, wdn_ref, gf_ref,
                     out_ref, up_ref, *, t_new, final):
    tm = h_ref.shape[0]
    hm = h_ref[...]
    pos = lax.broadcasted_iota(I32, (tm, FF_TILE), 0) % t_new
    acc = jnp.zeros((tm, D_MODEL), F32)
    for j in range(D_FF // FF_TILE):
        halves = []
        for base in (0, D_FF):
            cols = slice(base + j * FF_TILE, base + (j + 1) * FF_TILE)
            up = _mm(hm, wup_ref[:, cols])
            up_ref[:, cols] = up
            tap2 = jnp.where(pos < 2, p2_ref[:, cols], pltpu.roll(up, 2, 0))
            tap1 = jnp.where(pos < 1, p1_ref[:, cols], pltpu.roll(up, 1, 0))
            halves.append(cb_ref[:, cols] + tap2 * cw_ref[0:1, cols] + tap1 * cw_ref[1:2, cols]
                          + up * cw_ref[2:3, cols])
        act = (_gelu(halves[0]) * halves[1]).astype(MXU_DTYPE)
        acc = acc + _mm(act, wdn_ref[j * FF_TILE:(j + 1) * FF_TILE, :])
    _ffn_finish(x1_ref[...], acc, gf_ref, out_ref, final)


def _ffn_sample(h2, x1, p1, p2, wup, cw, cb, wdn, gf, t_new, final):
    m = h2.shape[0]
    full = lambda n: _const_spec((m, n))
    return pl.pallas_call(
        functools.partial(_ffn_sample_body, t_new=t_new, final=final),
        grid=(1,),
        in_specs=[full(D_MODEL), full(D_MODEL), full(2 * D_FF), full(2 * D_FF),
                  _const_spec((D_MODEL, 2 * D_FF)), _const_spec((FFN_CONV_W, 2 * D_FF)),
                  _const_spec((1, 2 * D_FF)), _const_spec((D_FF, D_MODEL)), _const_spec((1, D_MODEL))],
        out_specs=(full(D_MODEL), full(2 * D_FF)),
        out_shape=(jax.ShapeDtypeStruct((m, D_MODEL), F32), jax.ShapeDtypeStruct((m, 2 * D_FF), F32)),
        compiler_params=_cparams("arbitrary"),
        name="ffn_sample",
    )(h2, x1, p1, p2, wup, cw, cb, wdn, gf)


def _block_diag(w):
    n, blk, _ = w.shape
    out = jnp.zeros((n * blk, n * blk), w.dtype)
    for i in range(n):
        out = out.at[i * blk:(i + 1) * blk, i * blk:(i + 1) * blk].set(w[i])
    return out


def _pad_rows_front(a, rows):
    pad = rows - a.shape[1]
    return jnp.pad(a, ((0, 0), (pad, 0), (0, 0)))


def _layer_weights(l, g_mix, w_in, rec_conv_w, rec_conv_b, lru_wa, lru_ba, lru_wx, lru_bx, lru_lam,
                   gm_ln_g, gm_ln_b, gm_ws, gm_bs, w_out, g_ffn, w_up, ffn_conv_w, ffn_conv_b, w_down):
    w = w_in[l]
    c_ki = D_ATT + 2 * D_KV + D_QI
    c_rx = c_ki + IDX_DIM + N_IDX_HEADS
    w_r = jnp.concatenate([w[:, :c_ki], w[:, c_rx:], w[:, c_ki:c_rx],
                           jnp.zeros((D_MODEL, LANES - IDX_DIM - N_IDX_HEADS), w.dtype)], axis=1)
    tril = jnp.tril(jnp.ones((CHUNK, CHUNK), gm_ws.dtype))
    row2 = lambda a: a.reshape(1, -1)
    return dict(
        g_mix=row2(g_mix[l]), w_in=w_r.astype(MXU_DTYPE),
        rec_cw=rec_conv_w[l], rec_cb=row2(rec_conv_b[l]),
        wa=_block_diag(lru_wa[l]).astype(MXU_DTYPE), ba=row2(lru_ba[l]),
        wx=_block_diag(lru_wx[l]).astype(MXU_DTYPE), bx=row2(lru_bx[l]), lam=row2(lru_lam[l]),
        ln_g=row2(gm_ln_g[l]), ln_b=row2(gm_ln_b[l]), gm_w=gm_ws[l] * tril, gm_b=gm_bs[l],
        w_out=w_out[l].astype(MXU_DTYPE), g_ffn=row2(g_ffn[l]),
        w_up=w_up[l].astype(MXU_DTYPE), ffn_cw=ffn_conv_w[l], ffn_cb=row2(ffn_conv_b[l]),
        w_down=w_down[l].astype(MXU_DTYPE))


def _gmlp_mix_weights(p, c, n_seq):
    w = p['gm_w'][:, :c, :c]
    if n_seq > 1:
        w = jnp.einsum('ab,gts->gatbs', jnp.eye(n_seq, dtype=w.dtype), w).reshape(
            N_GM_HEADS, n_seq * c, n_seq * c)
    bias = jnp.repeat(p['gm_b'][:, :c].T, GM_HEAD_DIM, axis=1)
    return w.astype(MXU_DTYPE), jnp.tile(bias, (n_seq, 1))


def _prompt_layer(x, p, rel_bias, g_final, batch, seq, final):
    tm = 512
    (q, k, v, ki, kbf, vtbf, kibf, qi, _, wit, rec, gm) = _in_proj(x, p['g_mix'], p['w_in'], tm)
    att = _prompt_attention(rel_bias, q, qi, wit, kibf, kbf, vtbf, batch, seq)
    zeros8 = jnp.zeros((batch, SUBLANES, D_REC), F32)
    rec_out, hlast, cstate = _rec_branch(rec, zeros8, zeros8, p['rec_cw'], p['rec_cb'], p['wa'], p['ba'],
                                         p['wx'], p['bx'], p['lam'], batch, seq, 256)
    c = min(CHUNK, seq)
    wmix, bias = _gmlp_mix_weights(p, c, 1)
    gm_out, _ = _gmlp(gm, wmix, bias, p['ln_g'], p['ln_b'], tm)
    x1, h2 = _out_proj(x, att, rec_out, gm_out, p['w_out'], p['g_ffn'], tm)
    hist16 = jnp.zeros((batch, HALO, 2 * D_FF), F32)
    x2, upst = _ffn_prompt(h2, x1, hist16, p['w_up'], p['ffn_cw'], p['ffn_cb'], p['w_down'], g_final,
                           batch, seq, tm, final)
    state = (k.reshape(batch, seq, N_KV_HEADS, HEAD_DIM), v.reshape(batch, seq, N_KV_HEADS, HEAD_DIM),
             ki.reshape(batch, seq, IDX_DIM), hlast[:, SUBLANES - 1, :],
             cstate[:, SUBLANES - (REC_CONV_W - 1):, :], upst[:, HALO - (FFN_CONV_W - 1):, :])
    return x2, state


def _sample_layer(x, p, rel_bias, g_final, l, batch, t_new, final, page_table, cache_k, cache_v, cache_kidx,
                  h0, rec_hist, ffn_hist):
    m = batch * t_new
    (q, k, v, ki, _, _, _, qi, kiwi, _, rec, gm) = _in_proj(x, p['g_mix'], p['w_in'], m)
    att = _sample_attention(page_table, rel_bias, q, qi, kiwi, k, v, cache_k, cache_v, cache_kidx,
                            l, batch, t_new)
    hist8 = _pad_rows_front(rec_hist, SUBLANES)
    h08 = jnp.broadcast_to(h0[:, None, :], (batch, SUBLANES, D_REC))
    rec_out, hlast, cstate = _rec_branch(rec, hist8, h08, p['rec_cw'], p['rec_cb'], p['wa'], p['ba'],
                                         p['wx'], p['bx'], p['lam'], batch, t_new, t_new)
    c = min(CHUNK, t_new)
    wmix, bias = _gmlp_mix_weights(p, c, m // c)
    gm_out, vn = _gmlp(gm, wmix, bias, p['ln_g'], p['ln_b'], m)
    x1, h2 = _out_proj(x, att, rec_out, gm_out, p['w_out'], p['g_ffn'], m)
    zrow = jnp.zeros((batch, t_new - 2, 2 * D_FF), F32)
    p2 = jnp.concatenate([ffn_hist, zrow], axis=1).reshape(m, 2 * D_FF)
    p1 = jnp.concatenate([ffn_hist[:, 1:], zrow, zrow[:, :1]], axis=1).reshape(m, 2 * D_FF)
    x2, up = _ffn_sample(h2, x1, p1, p2, p['w_up'], p['ffn_cw'], p['ffn_cb'], p['w_down'], g_final,
                         t_new, final)
    state = (k.reshape(batch, t_new, N_KV_HEADS, HEAD_DIM), v.reshape(batch, t_new, N_KV_HEADS, HEAD_DIM),
             ki.reshape(batch, t_new, IDX_DIM), hlast[:, SUBLANES - 1, :],
             cstate[:, SUBLANES - (REC_CONV_W - 1):, :],
             up.reshape(batch, t_new, 2 * D_FF)[:, t_new - (FFN_CONV_W - 1):, :],
             vn.reshape(batch, t_new, D_GM))
    return x2, state


def kernel(x_prompt, x_sample, cache_k, cache_v, cache_kidx, page_table, state_lru_h, state_conv_rec,
           state_conv_ffn, rel_bias, g_mix, w_in, rec_conv_w, rec_conv_b, lru_wa, lru_ba, lru_wx, lru_bx,
           lru_lam, gm_ln_g, gm_ln_b, gm_ws, gm_bs, w_out, g_ffn, w_up, ffn_conv_w, ffn_conv_b, w_down,
           g_final):
    batch, seq, _ = x_prompt.shape
    dec_batch, t_new, _ = x_sample.shape
    depth = w_in.shape[0]
    assert seq % 512 == 0 and t_new == SUBLANES and (dec_batch * t_new) % KEY_CHUNK == 0
    xp = x_prompt.reshape(batch * seq, D_MODEL)
    xs = x_sample.reshape(dec_batch * t_new, D_MODEL)
    gf = g_final.reshape(1, D_MODEL)
    p_states, s_states = [], []
    for l in range(depth):
        p = _layer_weights(l, g_mix, w_in, rec_conv_w, rec_conv_b, lru_wa, lru_ba, lru_wx, lru_bx, lru_lam,
                           gm_ln_g, gm_ln_b, gm_ws, gm_bs, w_out, g_ffn, w_up, ffn_conv_w, ffn_conv_b, w_down)
        final = l == depth - 1
        xp, st = _prompt_layer(xp, p, rel_bias, gf, batch, seq, final)
        p_states.append(st)
        xs, st = _sample_layer(xs, p, rel_bias, gf, l, dec_batch, t_new, final, page_table, cache_k, cache_v,
                               cache_kidx, state_lru_h[l], state_conv_rec[l], state_conv_ffn[l])
        s_states.append(st)
    stack = lambda states, i: jnp.stack([s[i] for s in states])
    return ((xp.reshape(batch, seq, D_MODEL), xs.reshape(dec_batch, t_new, D_MODEL))
            + tuple(stack(p_states, i) for i in range(6))
            + tuple(stack(s_states, i) for i in range(7)))
```

```python
import functools
import math

import jax
import jax.numpy as jnp
from jax import lax
from jax.experimental import pallas as pl
from jax.experimental.pallas import tpu as pltpu

F32 = jnp.float32
I32 = jnp.int32
MXU_DTYPE = jnp.bfloat16

D_MODEL = 1024
N_HEADS = 8
HEAD_DIM = 64
D_ATT = N_HEADS * HEAD_DIM
N_KV_HEADS = 2
GROUP = N_HEADS // N_KV_HEADS
D_KV = N_KV_HEADS * HEAD_DIM
N_IDX_HEADS = 4
IDX_DIM = 64
D_QI = N_IDX_HEADS * IDX_DIM
TOPK_MAX = 256
NUM_BUCKETS = 32
MAX_EXACT = NUM_BUCKETS // 2
MAX_DISTANCE = 128
D_REC = 256
N_REC_BLOCKS = 4
REC_CONV_W = 4
LRU_C = 8.0
D_GM = 256
N_GM_HEADS = 4
GM_HEAD_DIM = D_GM // N_GM_HEADS
CHUNK = 128
D_FF = 2816
FFN_CONV_W = 3
EPS = 1e-6
PAGE_SIZE = 128
Q_SCALE = HEAD_DIM ** -0.5

LANES = 128
SUBLANES = 8
VMEM_LIMIT_BYTES = 56 * 1024 * 1024

C_Q = 0
C_KV = C_Q + D_ATT
C_QI = C_KV + 2 * D_KV
C_REC = C_QI + D_QI
C_GM = C_REC + 2 * D_REC
C_KIWI = C_GM + 2 * D_GM
D_IN_PAD = C_KIWI + LANES

KEY_CHUNK = 256
Q_BLOCK = 128
INT_MIN = -2 ** 31
F32_MIN_NORMAL = 2.0 ** -126
PACKED_ROWS = 2 * SUBLANES
NEG_BIG = -2.0 ** 100
BF16_PATTERN_MIN = -2 ** 15
COARSE_WINDOW_BITS = 18
SEARCH_CLASS_PAIRS = 2
COUNT_ACCUMULATORS = 8

BUCKET_LO = tuple(
    b if b <= MAX_EXACT else math.ceil(MAX_EXACT * (MAX_DISTANCE / MAX_EXACT) ** ((b - MAX_EXACT) / (NUM_BUCKETS - MAX_EXACT)))
    for b in range(NUM_BUCKETS))


def _cparams(*sem):
    return pltpu.CompilerParams(dimension_semantics=sem, vmem_limit_bytes=VMEM_LIMIT_BYTES)


def _const_spec(shape):
    nd = len(shape)
    return pl.BlockSpec(shape, lambda *_: (0,) * nd, pipeline_mode=pl.Buffered(1))


def _rmsnorm(x, g):
    return x * lax.rsqrt(jnp.mean(x * x, axis=-1, keepdims=True) + EPS) * g


def _gelu(x):
    return 0.5 * x * (1.0 + jnp.tanh(math.sqrt(2.0 / math.pi) * (x + 0.044715 * (x * x * x))))


def _sigmoid(x):
    return 1.0 / (1.0 + jnp.exp(-x))


def _mm(a, b):
    return jnp.dot(a, b, preferred_element_type=F32)


def _mm_nt(a, b):
    return lax.dot_general(a, b, (((1,), (1,)), ((), ())), preferred_element_type=F32)


def _flush_subnormal(f):
    return jnp.where(jnp.abs(f) < F32_MIN_NORMAL, 0.0, f)


def _pattern_to_f32(c):
    return _flush_subnormal(pltpu.bitcast(jnp.where(c >= 0, c, c ^ 0x7FFFFFFF), F32))


def _tree_sum(xs):
    while len(xs) > 1:
        xs = [a + b for a, b in zip(xs[0::2], xs[1::2])] + ([xs[-1]] if len(xs) % 2 else [])
    return xs[0]


def _pair_loop(n_pairs, chunk_fn, init):
    def body(p, carry):
        return chunk_fn(2 * p + 1, chunk_fn(2 * p, carry))
    return lax.fori_loop(0, n_pairs, body, init)


def _count_above(ref, n_chunks, cand, strict):
    cb = jnp.broadcast_to(cand, (SUBLANES, LANES))
    accs = [jnp.zeros((SUBLANES, LANES), F32)] * COUNT_ACCUMULATORS
    for c in range(n_chunks):
        x = ref[c]
        for j in range(KEY_CHUNK // SUBLANES):
            blk = x[j * SUBLANES:(j + 1) * SUBLANES]
            k = j % COUNT_ACCUMULATORS
            accs[k] = accs[k] + jnp.where((blk > cb) if strict else (blk >= cb), 1.0, 0.0)
    return jnp.sum(_tree_sum(accs), axis=0, keepdims=True)


def _kth_largest(count_ge, need, start, bits=32):
    def step(i, t):
        cand = t + lax.shift_left(jnp.int32(1), bits - 1 - i)
        return jnp.where(count_ge(_pattern_to_f32(cand)) >= need, cand, t)

    t = lax.fori_loop(0, bits, step, start)
    return jnp.where(t == INT_MIN, -jnp.inf, _pattern_to_f32(t))


def _bf16_pattern_to_f32(c):
    return _flush_subnormal(pltpu.bitcast(lax.shift_left(jnp.where(c >= 0, c, c ^ 0x7FFF), 16), F32))


def _count_above_packed(ref, n_chunks, cand):
    one = jnp.ones((PACKED_ROWS, LANES), MXU_DTYPE)
    zero = jnp.zeros((PACKED_ROWS, LANES), MXU_DTYPE)
    accs = [zero] * COUNT_ACCUMULATORS
    for c in range(n_chunks):
        x = ref[c]
        for j in range(KEY_CHUNK // PACKED_ROWS):
            k = j % COUNT_ACCUMULATORS
            accs[k] = accs[k] + jnp.where(x[j * PACKED_ROWS:(j + 1) * PACKED_ROWS] >= cand, one, zero)
    return jnp.sum(_tree_sum(accs).astype(F32), axis=0, keepdims=True)


def _in_proj_body(x_ref, g_ref, w_ref, q_ref, k_ref, v_ref, ki_ref, kbf_ref, vt_ref, kibf_ref,
                  qi_ref, kiwi_ref, wit_ref, rec_ref, gm_ref, *, transposed_state):
    n_chunks = kbf_ref.shape[0]
    h = _rmsnorm(x_ref[...], g_ref[...]).astype(MXU_DTYPE)

    def proj(lo, hi):
        return _mm(h, w_ref[:, lo:hi])

    q_ref[...] = (proj(C_Q, C_KV) * Q_SCALE).astype(MXU_DTYPE)
    kv = proj(C_KV, C_QI)
    k = kv[:, :D_KV]
    v = kv[:, D_KV:]
    vt = v.T
    kb = k.astype(MXU_DTYPE)
    qi_ref[...] = proj(C_QI, C_REC).astype(MXU_DTYPE)
    rec_ref[...] = proj(C_REC, C_GM)
    gm_ref[...] = proj(C_GM, C_KIWI)
    kiwi = proj(C_KIWI, D_IN_PAD)
    ki = kiwi[:, :IDX_DIM]
    kiwi_ref[...] = kiwi
    kib = ki.astype(MXU_DTYPE)
    kiwi_t = kiwi.T
    wit_ref[...] = kiwi_t[IDX_DIM:IDX_DIM + SUBLANES, :]
    if transposed_state:
        k_ref[0] = k.T
        v_ref[0] = vt
        ki_ref[0] = kiwi_t[:IDX_DIM, :]
    else:
        k_ref[...] = k
        v_ref[...] = v
        ki_ref[...] = ki
    for c in range(n_chunks):
        rows = slice(c * KEY_CHUNK, (c + 1) * KEY_CHUNK)
        kbf_ref[c] = kb[rows]
        kibf_ref[c] = kib[rows]
        vt_ref[c] = vt[:, rows].astype(MXU_DTYPE)


def _in_proj(x, g, w, tm, seq=None):
    m = x.shape[0]
    nc = tm // KEY_CHUNK
    row = lambda n: pl.BlockSpec((tm, n), lambda i: (i, 0))
    chunked = lambda a, b: pl.BlockSpec((nc, a, b), lambda i: (i, 0, 0))
    if seq is None:
        state_shape = lambda n: jax.ShapeDtypeStruct((m, n), F32)
        state_spec = row
    else:
        tps = seq // tm
        state_shape = lambda n: jax.ShapeDtypeStruct((m // seq, n, seq), F32)
        state_spec = lambda n: pl.BlockSpec((1, n, tm), lambda i: (i // tps, 0, i % tps))
    out_shape = (
        jax.ShapeDtypeStruct((m, D_ATT), MXU_DTYPE),
        state_shape(D_KV),
        state_shape(D_KV),
        state_shape(IDX_DIM),
        jax.ShapeDtypeStruct((m // KEY_CHUNK, KEY_CHUNK, D_KV), MXU_DTYPE),
        jax.ShapeDtypeStruct((m // KEY_CHUNK, D_KV, KEY_CHUNK), MXU_DTYPE),
        jax.ShapeDtypeStruct((m // KEY_CHUNK, KEY_CHUNK, IDX_DIM), MXU_DTYPE),
        jax.ShapeDtypeStruct((m, D_QI), MXU_DTYPE),
        jax.ShapeDtypeStruct((m, LANES), F32),
        jax.ShapeDtypeStruct((SUBLANES, m), F32),
        jax.ShapeDtypeStruct((m, 2 * D_REC), F32),
        jax.ShapeDtypeStruct((m, 2 * D_GM), F32),
    )
    out_specs = (
        row(D_ATT), state_spec(D_KV), state_spec(D_KV), state_spec(IDX_DIM),
        chunked(KEY_CHUNK, D_KV), chunked(D_KV, KEY_CHUNK), chunked(KEY_CHUNK, IDX_DIM),
        row(D_QI), row(LANES), pl.BlockSpec((SUBLANES, tm), lambda i: (0, i)),
        row(2 * D_REC), row(2 * D_GM),
    )
    return pl.pallas_call(
        functools.partial(_in_proj_body, transposed_state=seq is not None),
        grid=(m // tm,),
        in_specs=[row(D_MODEL), _const_spec((1, D_MODEL)), _const_spec((D_MODEL, D_IN_PAD))],
        out_specs=out_specs,
        out_shape=out_shape,
        compiler_params=_cparams("arbitrary"),
        name="in_proj",
    )(x, g, w)


def _bias_table(rb_ref, head, delta, shape):
    row = lax.broadcasted_iota(I32, shape, 0)
    lane = lax.broadcasted_iota(I32, shape, 1)
    d = delta + lane - row
    far = rb_ref[NUM_BUCKETS - 1, head]
    val = jnp.full(shape, rb_ref[0, head] - far, F32)
    for b in range(1, NUM_BUCKETS - 1):
        val = jnp.where(d >= BUCKET_LO[b], rb_ref[b, head] - far, val)
    return jnp.where(d >= BUCKET_LO[NUM_BUCKETS - 1], 0.0, val)


def _prompt_attn_body(rb_ref, q_ref, qi_ref, wit_ref, ki_ref, k_ref, vt_ref, att_ref,
                      s_s, sb_s, mb_s, tab_s, tri_s, thr_s, need_s, m_s, acc_s, qpad_s, *, topk):
    b = pl.program_id(0)
    i = pl.program_id(1)
    n_pairs = i // 4 + 1
    ck = (KEY_CHUNK, Q_BLOCK)

    @pl.when((b == 0) & (i == 0))
    def _init_tables():
        r = lax.broadcasted_iota(I32, (KEY_CHUNK, KEY_CHUNK), 0)
        c = lax.broadcasted_iota(I32, (KEY_CHUNK, KEY_CHUNK), 1)
        tri_s[...] = jnp.where(c <= r, 1.0, 0.0).astype(MXU_DTYPE)

        def per_head(h, carry):
            for ti in range(4):
                tab_s[ti, h] = _bias_table(rb_ref, h, ti * Q_BLOCK, ck)
            return carry

        lax.fori_loop(0, N_HEADS, per_head, 0)

    row = lax.broadcasted_iota(I32, ck, 0)
    lane = lax.broadcasted_iota(I32, ck, 1)
    q_pos = i * Q_BLOCK + lane

    qi = qi_ref[...]
    qis = jnp.concatenate([qi[:, h * IDX_DIM:(h + 1) * IDX_DIM] for h in range(N_IDX_HEADS)], axis=0)
    wit = wit_ref[...]
    w_row = jnp.concatenate([wit[h:h + 1, :] for h in range(N_IDX_HEADS)], axis=1)

    def score_chunk(c, carry):
        s4 = jnp.maximum(_mm_nt(ki_ref[c], qis), 0.0) * w_row
        s = _tree_sum([s4[:, h * Q_BLOCK:(h + 1) * Q_BLOCK] for h in range(N_IDX_HEADS)])
        valid = (c * KEY_CHUNK + row) <= q_pos
        s = jnp.where(valid, s, -jnp.inf)
        s_s[c] = s
        sb_s[c] = s.astype(sb_s.dtype)
        return carry

    _pair_loop(n_pairs, score_chunk, 0)

    max_pairs = s_s.shape[0] // 2
    odd = n_pairs % SEARCH_CLASS_PAIRS

    @pl.when((odd != 0) & (n_pairs < max_pairs))
    def _fill_class_padding():
        for c in (2 * n_pairs, 2 * n_pairs + 1):
            s_s[c] = jnp.full(ck, -jnp.inf, F32)
            sb_s[c] = jnp.full(ck, -jnp.inf, sb_s.dtype)

    def search(n_chunks):
        def coarse_step(i, c):
            cand = c + lax.shift_left(jnp.int32(1), 15 - i)
            cand_b = jnp.broadcast_to(_bf16_pattern_to_f32(cand), (PACKED_ROWS, Q_BLOCK)).astype(sb_s.dtype)
            return jnp.where(_count_above_packed(sb_s, n_chunks, cand_b) >= topk, cand, c)

        coarse = lax.fori_loop(0, 16, coarse_step, jnp.full((1, Q_BLOCK), BF16_PATTERN_MIN, I32))
        start = jnp.where(coarse == BF16_PATTERN_MIN, INT_MIN, lax.shift_left(coarse - 1, 16))
        t = _kth_largest(lambda cand: _count_above(s_s, n_chunks, cand, False), topk, start, COARSE_WINDOW_BITS)
        thr_s[0:1, :] = t
        need_s[0:1, :] = topk - _count_above(s_s, n_chunks, t, True)

    class_pairs = list(range(SEARCH_CLASS_PAIRS, max_pairs, SEARCH_CLASS_PAIRS)) + [max_pairs]
    for k, pairs in enumerate(class_pairs):
        lo_bound = class_pairs[k - 1] if k else 0
        pl.when((n_pairs > lo_bound) & (n_pairs <= pairs))(functools.partial(search, 2 * pairs))
    t = thr_s[0:1, :]
    need = need_s[0:1, :]

    tri = tri_s[...]

    def mask_chunk(c, carry):
        sc = s_s[c]
        tie = (sc == t) & (sc > -jnp.inf)
        tie_f = jnp.where(tie, 1.0, 0.0)
        incl = _mm(tri, tie_f.astype(MXU_DTYPE))
        rank = carry + incl - tie_f
        sel = (sc > t) | (tie & (rank < need))
        mb_s[c] = jnp.where(sel, 0.0, -jnp.inf).astype(mb_s.dtype)
        return carry + incl[KEY_CHUNK - 1:KEY_CHUNK, :]

    _pair_loop(n_pairs, mask_chunk, jnp.zeros((1, Q_BLOCK), F32))

    q = q_ref[...]
    zeros = jnp.zeros((Q_BLOCK, HEAD_DIM), MXU_DTYPE)
    for h in range(N_HEADS):
        qh = q[:, h * HEAD_DIM:(h + 1) * HEAD_DIM]
        parts = [qh, zeros] if h < GROUP else [zeros, qh]
        qpad_s[h * Q_BLOCK:(h + 1) * Q_BLOCK, :] = jnp.concatenate(parts, axis=1)
    m_s[...] = jnp.full(m_s.shape, NEG_BIG, F32)
    acc_s[...] = jnp.zeros(acc_s.shape, F32)
    ones_rows = jnp.ones((PACKED_ROWS, 2 * KEY_CHUNK), MXU_DTYPE)

    def attend_pair(p, near):
        c0, c1 = 2 * p, 2 * p + 1
        keys = jnp.concatenate([k_ref[c0], k_ref[c1]], axis=0)
        logits = _mm_nt(keys, qpad_s[...])
        mb = jnp.concatenate([mb_s[c0], mb_s[c1]], axis=0)
        if near:
            t0 = jnp.clip(i - 2 * c0, 0, 3)
            t1 = jnp.clip(i - 2 * c1, 0, 3)
        for g in range(N_KV_HEADS):
            ps, alphas = [], []
            for hh in range(GROUP):
                h = g * GROUP + hh
                cols = slice(h * Q_BLOCK, (h + 1) * Q_BLOCK)
                lt = logits[:, cols]
                if near:
                    lt = lt + jnp.concatenate([tab_s[t0, h], tab_s[t1, h]], axis=0)
                lt = lt.astype(MXU_DTYPE) + mb
                m_old = m_s[0:1, cols]
                m_new = jnp.maximum(m_old, jnp.max(lt, axis=0, keepdims=True).astype(F32))
                alphas.append(jnp.exp(m_old - m_new))
                ps.append(jnp.exp(lt - m_new.astype(MXU_DTYPE)))
                m_s[0:1, cols] = m_new
            pg = jnp.concatenate(ps, axis=1)
            ag = jnp.concatenate(alphas, axis=1)
            rows = slice(g * HEAD_DIM, (g + 1) * HEAD_DIM)
            vt = jnp.concatenate([jnp.concatenate([vt_ref[c0, rows, :], vt_ref[c1, rows, :]], axis=1), ones_rows],
                                 axis=0)
            acc_s[g] = acc_s[g] * ag + _mm(vt, pg)

    def far_pair(p, carry):
        attend_pair(2 * p, False)
        attend_pair(2 * p + 1, False)
        return carry

    n_far = jnp.maximum(n_pairs - 2, 0)
    lax.fori_loop(0, n_far // 2, far_pair, 0)

    @pl.when(n_far % 2 == 1)
    def _odd_far_pair():
        attend_pair(n_far - 1, False)

    @pl.when(n_pairs >= 2)
    def _near_pairs():
        attend_pair(n_pairs - 2, True)
        attend_pair(n_pairs - 1, True)

    @pl.when(n_pairs == 1)
    def _only_pair():
        attend_pair(0, True)

    for h in range(N_HEADS):
        g, hh = divmod(h, GROUP)
        cols = slice(hh * Q_BLOCK, (hh + 1) * Q_BLOCK)
        o = acc_s[g][:HEAD_DIM, cols] / acc_s[g][HEAD_DIM:HEAD_DIM + 1, cols]
        att_ref[:, h * HEAD_DIM:(h + 1) * HEAD_DIM] = o.T.astype(att_ref.dtype)


def _prompt_attention(rel_bias, q, qi, wit, kibf, kbf, vtbf, batch, seq):
    nq = seq // Q_BLOCK
    nc = seq // KEY_CHUNK
    topk = float(min(TOPK_MAX, seq // 4))
    qrow = lambda n: pl.BlockSpec((Q_BLOCK, n), lambda b, i: (b * nq + i, 0))
    per_batch = lambda a, c: pl.BlockSpec((nc, a, c), lambda b, i: (b, 0, 0))
    return pl.pallas_call(
        functools.partial(_prompt_attn_body, topk=topk),
        grid=(batch, nq),
        in_specs=[
            pl.BlockSpec(memory_space=pltpu.SMEM),
            qrow(D_ATT), qrow(D_QI),
            pl.BlockSpec((SUBLANES, Q_BLOCK), lambda b, i: (0, b * nq + i)),
            per_batch(KEY_CHUNK, IDX_DIM), per_batch(KEY_CHUNK, D_KV), per_batch(D_KV, KEY_CHUNK),
        ],
        out_specs=qrow(D_ATT),
        out_shape=jax.ShapeDtypeStruct((batch * seq, D_ATT), MXU_DTYPE),
        scratch_shapes=[
            pltpu.VMEM((nc, KEY_CHUNK, Q_BLOCK), F32),
            pltpu.VMEM((nc, KEY_CHUNK, Q_BLOCK), MXU_DTYPE),
            pltpu.VMEM((nc, KEY_CHUNK, Q_BLOCK), MXU_DTYPE),
            pltpu.VMEM((4, N_HEADS, KEY_CHUNK, Q_BLOCK), F32),
            pltpu.VMEM((KEY_CHUNK, KEY_CHUNK), MXU_DTYPE),
            pltpu.VMEM((SUBLANES, Q_BLOCK), F32),
            pltpu.VMEM((SUBLANES, Q_BLOCK), F32),
            pltpu.VMEM((SUBLANES, N_HEADS * Q_BLOCK), F32),
            pltpu.VMEM((N_KV_HEADS, HEAD_DIM + PACKED_ROWS, GROUP * Q_BLOCK), F32),
            pltpu.VMEM((N_HEADS * Q_BLOCK, D_KV), MXU_DTYPE),
        ],
        compiler_params=_cparams("arbitrary", "arbitrary"),
        name="prompt_attn",
    )(rel_bias, q, qi, wit, kibf, kbf, vtbf)


def _sample_attn_body(pt_ref, rb_ref, q_ref, qi_ref, kiwi_ref, knew_ref, vnew_ref,
                      ck_hbm, cv_hbm, cki_hbm, att_ref,
                      ktbuf, vtbuf, kitbuf, sem, *, topk, n_pages, layer, t_new):
    b = pl.program_id(0)
    past = n_pages * PAGE_SIZE
    slot = b % 2

    def page_copies(seq, buf, p):
        phys = pt_ref[seq, p]
        cols = pl.ds(pl.multiple_of(p * PAGE_SIZE, PAGE_SIZE), PAGE_SIZE)
        return (pltpu.make_async_copy(ck_hbm.at[layer, phys], ktbuf.at[buf, :, cols], sem.at[0, buf]),
                pltpu.make_async_copy(cv_hbm.at[layer, phys], vtbuf.at[buf, :, cols], sem.at[1, buf]),
                pltpu.make_async_copy(cki_hbm.at[layer, phys], kitbuf.at[buf, :, cols], sem.at[2, buf]))

    def start_pages(seq, buf):
        def body(p, carry):
            for cp in page_copies(seq, buf, p):
                cp.start()
            return carry
        lax.fori_loop(0, n_pages, body, 0)

    def wait_pages(seq, buf):
        def body(p, carry):
            for cp in page_copies(seq, buf, p):
                cp.wait()
            return carry
        lax.fori_loop(0, n_pages, body, 0)

    @pl.when(b == 0)
    def _first_fetch():
        start_pages(0, 0)

    @pl.when(b + 1 < pl.num_programs(0))
    def _prefetch_next():
        start_pages(b + 1, 1 - slot)

    qi = qi_ref[...]
    qis = jnp.concatenate([qi[:, h * IDX_DIM:(h + 1) * IDX_DIM] for h in range(N_IDX_HEADS)], axis=0)
    kiwi = kiwi_ref[...]
    w_col = jnp.concatenate([kiwi[:, IDX_DIM + h:IDX_DIM + h + 1] for h in range(N_IDX_HEADS)], axis=0)
    q = q_ref[...]
    zeros = jnp.zeros((t_new, HEAD_DIM), MXU_DTYPE)
    qpad = jnp.concatenate(
        [jnp.concatenate([q[:, h * HEAD_DIM:(h + 1) * HEAD_DIM], zeros] if h < GROUP else
                         [zeros, q[:, h * HEAD_DIM:(h + 1) * HEAD_DIM]], axis=1)
         for h in range(N_HEADS)], axis=0)

    wait_pages(b, slot)

    def idx_score(qk):
        s4 = jnp.maximum(qk, 0.0) * w_col
        s = s4[0:t_new]
        for h in range(1, N_IDX_HEADS):
            s = s + s4[h * t_new:(h + 1) * t_new]
        return s

    s_p = idx_score(_mm(qis, kitbuf[slot].astype(MXU_DTYPE)))
    ki_new = kiwi[:, :IDX_DIM].astype(MXU_DTYPE)
    ki_new = jnp.concatenate([ki_new, jnp.zeros((LANES - t_new, IDX_DIM), MXU_DTYPE)], axis=0)
    rown = lax.broadcasted_iota(I32, (t_new, LANES), 0)
    lanen = lax.broadcasted_iota(I32, (t_new, LANES), 1)
    valid_n = lanen <= rown
    s_n = jnp.where(valid_n, idx_score(_mm_nt(qis, ki_new)), -jnp.inf)

    def count(cand, strict):
        cmp = (lambda x: x > cand) if strict else (lambda x: x >= cand)
        return (jnp.sum(jnp.where(cmp(s_p), 1.0, 0.0), axis=1, keepdims=True)
                + jnp.sum(jnp.where(cmp(s_n), 1.0, 0.0), axis=1, keepdims=True))

    t = _kth_largest(lambda cand: count(cand, False), topk, jnp.full((t_new, 1), INT_MIN, I32))
    need = topk - count(t, True)

    r = lax.broadcasted_iota(I32, (KEY_CHUNK, KEY_CHUNK), 0)
    c = lax.broadcasted_iota(I32, (KEY_CHUNK, KEY_CHUNK), 1)
    tri = jnp.where(r <= c, 1.0, 0.0).astype(MXU_DTYPE)
    tie_p = s_p == t
    carry = jnp.zeros((t_new, 1), F32)
    mask_parts = []
    for ch in range(past // KEY_CHUNK):
        cols = slice(ch * KEY_CHUNK, (ch + 1) * KEY_CHUNK)
        tie_c = tie_p[:, cols]
        tie_f = jnp.where(tie_c, 1.0, 0.0)
        incl = _mm(tie_f.astype(MXU_DTYPE), tri)
        sel = (s_p[:, cols] > t) | (tie_c & ((carry + incl - tie_f) < need))
        mask_parts.append(jnp.where(sel, 0.0, -jnp.inf))
        carry = carry + incl[:, KEY_CHUNK - 1:KEY_CHUNK]
    mb_p = jnp.concatenate(mask_parts, axis=1)
    tie_n = (s_n == t) & valid_n
    tie_nf = jnp.where(tie_n, 1.0, 0.0)
    incl_n = _mm(tie_nf.astype(MXU_DTYPE), tri[:LANES, :LANES])
    sel_n = (s_n > t) | (tie_n & ((carry + incl_n - tie_nf) < need))
    mb_n = jnp.where(sel_n & valid_n, 0.0, -jnp.inf)

    lg_p = _mm(qpad, ktbuf[slot].astype(MXU_DTYPE))
    k_new = jnp.concatenate([knew_ref[...].astype(MXU_DTYPE),
                             jnp.zeros((LANES - t_new, D_KV), MXU_DTYPE)], axis=0)
    lg_n = _mm_nt(qpad, k_new)
    near = slice(past - LANES, past)
    lp_rows, ln_rows = [], []
    for h in range(N_HEADS):
        rows = slice(h * t_new, (h + 1) * t_new)
        far = rb_ref[NUM_BUCKETS - 1, h]

        def bias(d):
            val = jnp.full(d.shape, rb_ref[0, h] - far, F32)
            for bk in range(1, NUM_BUCKETS - 1):
                val = jnp.where(d >= BUCKET_LO[bk], rb_ref[bk, h] - far, val)
            return jnp.where(d >= BUCKET_LO[NUM_BUCKETS - 1], 0.0, val)

        lp = lg_p[rows] + mb_p
        lp_near = lp[:, near] + bias(rown + (LANES - lanen))
        lp_rows.append(jnp.concatenate([lp[:, :past - LANES], lp_near], axis=1))
        ln_rows.append(lg_n[rows] + mb_n + bias(rown - lanen))
    lp_all = jnp.concatenate(lp_rows, axis=0)
    ln_all = jnp.concatenate(ln_rows, axis=0)
    m = jnp.maximum(jnp.max(lp_all, axis=1, keepdims=True), jnp.max(ln_all, axis=1, keepdims=True))
    p_p = jnp.exp(lp_all - m)
    p_n = jnp.exp(ln_all - m)
    denom = jnp.sum(p_p, axis=1, keepdims=True) + jnp.sum(p_n, axis=1, keepdims=True)
    v_new = jnp.concatenate([vnew_ref[...].astype(MXU_DTYPE),
                             jnp.zeros((LANES - t_new, D_KV), MXU_DTYPE)], axis=0)
    o = (_mm_nt(p_p.astype(MXU_DTYPE), vtbuf[slot].astype(MXU_DTYPE))
         + _mm(p_n.astype(MXU_DTYPE), v_new)) / denom
    for h in range(N_HEADS):
        g = h // GROUP
        att_ref[:, h * HEAD_DIM:(h + 1) * HEAD_DIM] = (
            o[h * t_new:(h + 1) * t_new, g * HEAD_DIM:(g + 1) * HEAD_DIM].astype(att_ref.dtype))


def _sample_attention(page_table, rel_bias, q, qi, kiwi, k_new, v_new, cache_k, cache_v, cache_kidx,
                      layer, batch, t_new):
    n_pages = page_table.shape[1]
    past = n_pages * PAGE_SIZE
    topk = float(min(TOPK_MAX, (past + t_new) // 4))
    depth, n_pool = cache_k.shape[:2]
    ck = cache_k.transpose(0, 1, 3, 4, 2).reshape(depth, n_pool, D_KV, PAGE_SIZE)
    cv = cache_v.transpose(0, 1, 3, 4, 2).reshape(depth, n_pool, D_KV, PAGE_SIZE)
    cki = cache_kidx.transpose(0, 1, 3, 2)
    qrow = lambda n: pl.BlockSpec((t_new, n), lambda b, pt: (b, 0))
    hbm = pl.BlockSpec(memory_space=pl.ANY)
    grid_spec = pltpu.PrefetchScalarGridSpec(
        num_scalar_prefetch=1,
        grid=(batch,),
        in_specs=[pl.BlockSpec(memory_space=pltpu.SMEM),
                  qrow(D_ATT), qrow(D_QI), qrow(LANES), qrow(D_KV), qrow(D_KV), hbm, hbm, hbm],
        out_specs=qrow(D_ATT),
        scratch_shapes=[
            pltpu.VMEM((2, D_KV, past), F32),
            pltpu.VMEM((2, D_KV, past), F32),
            pltpu.VMEM((2, IDX_DIM, past), F32),
            pltpu.SemaphoreType.DMA((3, 2)),
        ],
    )
    return pl.pallas_call(
        functools.partial(_sample_attn_body, topk=topk, n_pages=n_pages, layer=layer, t_new=t_new),
        grid_spec=grid_spec,
        out_shape=jax.ShapeDtypeStruct((batch * t_new, D_ATT), MXU_DTYPE),
        compiler_params=_cparams("arbitrary"),
        name="sample_attn",
    )(page_table, rel_bias, q, qi, kiwi, k_new, v_new, ck, cv, cki)


def _rec_body(rec_ref, hist_ref, h0_ref, cw_ref, cb_ref, wa_ref, ba_ref, wx_ref, bx_ref, lam_ref,
              out_ref, hlast_ref, cstate_ref, hcar, xprev):
    tt = rec_ref.shape[0]

    @pl.when(pl.program_id(1) == 0)
    def _load_state():
        hcar[...] = h0_ref[0]
        xprev[...] = hist_ref[0]

    rx = rec_ref[:, :D_REC]
    rg = rec_ref[:, D_REC:]
    ext = jnp.concatenate([xprev[...], rx], axis=0)
    xc = cb_ref[...]
    for j in range(REC_CONV_W - 1):
        xc = xc + pltpu.roll(ext, REC_CONV_W - 1 - j, 0)[SUBLANES:] * cw_ref[j:j + 1, :]
    xc = xc + rx * cw_ref[REC_CONV_W - 1:REC_CONV_W, :]
    xcb = xc.astype(MXU_DTYPE)
    r = _sigmoid(_mm(xcb, wa_ref[...]) + ba_ref[...])
    gi = _sigmoid(_mm(xcb, wx_ref[...]) + bx_ref[...])
    nl = -lam_ref[...]
    softplus = jnp.maximum(nl, 0.0) + jnp.log1p(jnp.exp(-jnp.abs(nl)))
    log_a = -LRU_C * r * softplus
    a = jnp.exp(log_a)
    u = jnp.sqrt(-jnp.tanh(log_a) * (a * a + 1.0)) * (gi * xc)
    row = lax.broadcasted_iota(I32, (tt, D_REC), 0)
    s = 1
    while s < tt:
        keep = row >= s
        u = jnp.where(keep, u + a * pltpu.roll(u, s, 0), u)
        a = jnp.where(keep, a * pltpu.roll(a, s, 0), a)
        s *= 2
    hs = u + a * hcar[0:1, :]
    hcar[...] = jnp.broadcast_to(hs[tt - 1:tt, :], hcar.shape)
    xprev[...] = rx[tt - SUBLANES:, :]
    out_ref[...] = (_gelu(rg) * hs).astype(out_ref.dtype)
    hlast_ref[0] = hs[tt - SUBLANES:, :]
    cstate_ref[0] = rx[tt - SUBLANES:, :]


def _rec_branch(rec, hist8, h08, cw, cb, wa, ba, wx, bx, lam, batch, t_len, tt):
    nt = t_len // tt
    state = pl.BlockSpec((1, SUBLANES, D_REC), lambda b, t: (b, 0, 0))
    vec = _const_spec((1, D_REC))
    return pl.pallas_call(
        _rec_body,
        grid=(batch, nt),
        in_specs=[pl.BlockSpec((tt, 2 * D_REC), lambda b, t: (b * nt + t, 0)), state, state,
                  _const_spec((REC_CONV_W, D_REC)), vec, _const_spec((D_REC, D_REC)), vec,
                  _const_spec((D_REC, D_REC)), vec, vec],
        out_specs=(pl.BlockSpec((tt, D_REC), lambda b, t: (b * nt + t, 0)), state, state),
        out_shape=(jax.ShapeDtypeStruct((batch * t_len, D_REC), MXU_DTYPE),
                   jax.ShapeDtypeStruct((batch, SUBLANES, D_REC), F32),
                   jax.ShapeDtypeStruct((batch, SUBLANES, D_REC), F32)),
        scratch_shapes=[pltpu.VMEM((SUBLANES, D_REC), F32), pltpu.VMEM((SUBLANES, D_REC), F32)],
        compiler_params=_cparams("arbitrary", "arbitrary"),
        name="rec_branch",
    )(rec, hist8, h08, cw, cb, wa, ba, wx, bx, lam)


def _gmlp_body(gm_ref, wmix_ref, bias_ref, lng_ref, lnb_ref, out_ref, vn_ref):
    rows = wmix_ref.shape[1]
    n_sub = gm_ref.shape[0] // rows
    head_of_lane = lax.broadcasted_iota(I32, (rows, D_GM), 1) // GM_HEAD_DIM
    for sb in range(n_sub):
        sl = slice(sb * rows, (sb + 1) * rows)
        u = _gelu(gm_ref[sl, :D_GM])
        gv = _gelu(gm_ref[sl, D_GM:])
        xc = gv - jnp.mean(gv, axis=-1, keepdims=True)
        var = jnp.mean(xc * xc, axis=-1, keepdims=True)
        vn = xc * lax.rsqrt(var + EPS) * lng_ref[...] + lnb_ref[...]
        vn_ref[sl, :] = vn
        vnb = vn.astype(MXU_DTYPE)
        mix = jnp.zeros((rows, D_GM), F32)
        for g in range(N_GM_HEADS):
            mix = jnp.where(head_of_lane == g, _mm(wmix_ref[g], vnb), mix)
        out_ref[sl, :] = (u * (mix + bias_ref[...])).astype(out_ref.dtype)


def _gmlp(gm, wmix, bias, lng, lnb, tm):
    m = gm.shape[0]
    rows = wmix.shape[1]
    return pl.pallas_call(
        _gmlp_body,
        grid=(m // tm,),
        in_specs=[pl.BlockSpec((tm, 2 * D_GM), lambda i: (i, 0)),
                  _const_spec((N_GM_HEADS, rows, rows)), _const_spec((rows, D_GM)),
                  _const_spec((1, D_GM)), _const_spec((1, D_GM))],
        out_specs=(pl.BlockSpec((tm, D_GM), lambda i: (i, 0)), pl.BlockSpec((tm, D_GM), lambda i: (i, 0))),
        out_shape=(jax.ShapeDtypeStruct((m, D_GM), MXU_DTYPE), jax.ShapeDtypeStruct((m, D_GM), F32)),
        compiler_params=_cparams("arbitrary"),
        name="gmlp",
    )(gm, wmix, bias, lng, lnb)


def _out_proj_body(x_ref, att_ref, rec_ref, gm_ref, w_ref, g_ref, x1_ref, h2_ref):
    y = (_mm(att_ref[...], w_ref[0:D_ATT, :])
         + _mm(rec_ref[...], w_ref[D_ATT:D_ATT + D_REC, :])
         + _mm(gm_ref[...], w_ref[D_ATT + D_REC:, :]))
    x1 = x_ref[...] + y
    x1_ref[...] = x1
    h2_ref[...] = _rmsnorm(x1, g_ref[...]).astype(h2_ref.dtype)


def _out_proj(x, att, rec, gm, w, g, tm):
    m = x.shape[0]
    row = lambda n: pl.BlockSpec((tm, n), lambda i: (i, 0))
    return pl.pallas_call(
        _out_proj_body,
        grid=(m // tm,),
        in_specs=[row(D_MODEL), row(D_ATT), row(D_REC), row(D_GM),
                  _const_spec((D_MODEL, D_MODEL)), _const_spec((1, D_MODEL))],
        out_specs=(row(D_MODEL), row(D_MODEL)),
        out_shape=(jax.ShapeDtypeStruct((m, D_MODEL), F32), jax.ShapeDtypeStruct((m, D_MODEL), MXU_DTYPE)),
        compiler_params=_cparams("arbitrary"),
        name="out_proj",
    )(x, att, rec, gm, w, g)


FF_TILE = 256
HALO = 16


def _ffn_finish(x1, acc, gf_ref, out_ref, final):
    x2 = x1 + acc
    out_ref[...] = _rmsnorm(x2, gf_ref[...]) if final else x2


def _ffn_prompt_body(h_ref, halo_ref, x1_ref, hist_ref, wup_ref, cw_ref, cb_ref, wdn_ref, gf_ref,
                     out_ref, upst_ref, act_s, *, tiles_per_seq, final):
    tm = h_ref.shape[0]
    first = (pl.program_id(0) % tiles_per_seq) == 0
    h_ext = jnp.concatenate([halo_ref[...], h_ref[...]], axis=0)
    for j in range(D_FF // FF_TILE):
        halves = []
        for base in (0, D_FF):
            cols = slice(base + j * FF_TILE, base + (j + 1) * FF_TILE)
            up = _mm(h_ext, wup_ref[:, cols])
            up_m = up[HALO:]
            upst_ref[0, :, cols] = up_m[tm - HALO:, :]
            ext = jnp.concatenate([jnp.where(first, hist_ref[0, :, cols], up[:HALO]), up_m], axis=0)
            uc = cb_ref[:, cols]
            for jj in range(FFN_CONV_W - 1):
                uc = uc + pltpu.roll(ext, FFN_CONV_W - 1 - jj, 0)[HALO:] * cw_ref[jj:jj + 1, cols]
            halves.append(uc + up_m * cw_ref[FFN_CONV_W - 1:FFN_CONV_W, cols])
        act_s[:, j * FF_TILE:(j + 1) * FF_TILE] = (_gelu(halves[0]) * halves[1]).astype(MXU_DTYPE)
    _ffn_finish(x1_ref[...], _mm(act_s[...], wdn_ref[...]), gf_ref, out_ref, final)


def _ffn_prompt(h2, x1, hist16, wup, cw, cb, wdn, gf, batch, seq, tm, final):
    m = h2.shape[0]
    tps = seq // tm
    row = lambda n: pl.BlockSpec((tm, n), lambda i: (i, 0))
    state = pl.BlockSpec((1, HALO, 2 * D_FF), lambda i: (i // tps, 0, 0))
    return pl.pallas_call(
        functools.partial(_ffn_prompt_body, tiles_per_seq=tps, final=final),
        grid=(m // tm,),
        in_specs=[row(D_MODEL),
                  pl.BlockSpec((HALO, D_MODEL), lambda i: (jnp.maximum(i * (tm // HALO) - 1, 0), 0)),
                  row(D_MODEL), state,
                  _const_spec((D_MODEL, 2 * D_FF)), _const_spec((FFN_CONV_W, 2 * D_FF)),
                  _const_spec((1, 2 * D_FF)), _const_spec((D_FF, D_MODEL)), _const_spec((1, D_MODEL))],
        out_specs=(row(D_MODEL), state),
        out_shape=(jax.ShapeDtypeStruct((m, D_MODEL), F32),
                   jax.ShapeDtypeStruct((batch, HALO, 2 * D_FF), F32)),
        scratch_shapes=[pltpu.VMEM((tm, D_FF), MXU_DTYPE)],
        compiler_params=_cparams("arbitrary"),
        name="ffn_prompt",
    )(h2, h2, x1, hist16, wup, cw, cb, wdn, gf)


def _ffn_sample_body(h_ref, x1_ref, p1_ref, p2_ref, wup_ref, cw_ref, cb_ref, wdn_ref, gf_ref,
                     out_ref, up_ref, *, t_new, final):
    tm = h_ref.shape[0]
    hm = h_ref[...]
    pos = lax.broadcasted_iota(I32, (tm, FF_TILE), 0) % t_new
    acc = jnp.zeros((tm, D_MODEL), F32)
    for j in range(D_FF // FF_TILE):
        halves = []
        for base in (0, D_FF):
            cols = slice(base + j * FF_TILE, base + (j + 1) * FF_TILE)
            up = _mm(hm, wup_ref[:, cols])
            up_ref[:, cols] = up
            tap2 = jnp.where(pos < 2, p2_ref[:, cols], pltpu.roll(up, 2, 0))
            tap1 = jnp.where(pos < 1, p1_ref[:, cols], pltpu.roll(up, 1, 0))
            halves.append(cb_ref[:, cols] + tap2 * cw_ref[0:1, cols] + tap1 * cw_ref[1:2, cols]
                          + up * cw_ref[2:3, cols])
        act = (_gelu(halves[0]) * halves[1]).astype(MXU_DTYPE)
        acc = acc + _mm(act, wdn_ref[j * FF_TILE:(j + 1) * FF_TILE, :])
    _ffn_finish(x1_ref[...], acc, gf_ref, out_ref, final)


def _ffn_sample(h2, x1, p1, p2, wup, cw, cb, wdn, gf, t_new, final):
    m = h2.shape[0]
    full = lambda n: _const_spec((m, n))
    return pl.pallas_call(
        functools.partial(_ffn_sample_body, t_new=t_new, final=final),
        grid=(1,),
        in_specs=[full(D_MODEL), full(D_MODEL), full(2 * D_FF), full(2 * D_FF),
                  _const_spec((D_MODEL, 2 * D_FF)), _const_spec((FFN_CONV_W, 2 * D_FF)),
                  _const_spec((1, 2 * D_FF)), _const_spec((D_FF, D_MODEL)), _const_spec((1, D_MODEL))],
        out_specs=(full(D_MODEL), full(2 * D_FF)),
        out_shape=(jax.ShapeDtypeStruct((m, D_MODEL), F32), jax.ShapeDtypeStruct((m, 2 * D_FF), F32)),
        compiler_params=_cparams("arbitrary"),
        name="ffn_sample",
    )(h2, x1, p1, p2, wup, cw, cb, wdn, gf)


def _block_diag(w):
    n, blk, _ = w.shape
    out = jnp.zeros((n * blk, n * blk), w.dtype)
    for i in range(n):
        out = out.at[i * blk:(i + 1) * blk, i * blk:(i + 1) * blk].set(w[i])
    return out


def _pad_rows_front(a, rows):
    pad = rows - a.shape[1]
    return jnp.pad(a, ((0, 0), (pad, 0), (0, 0)))


def _layer_weights(l, g_mix, w_in, rec_conv_w, rec_conv_b, lru_wa, lru_ba, lru_wx, lru_bx, lru_lam,
                   gm_ln_g, gm_ln_b, gm_ws, gm_bs, w_out, g_ffn, w_up, ffn_conv_w, ffn_conv_b, w_down):
    w = w_in[l]
    c_ki = D_ATT + 2 * D_KV + D_QI
    c_rx = c_ki + IDX_DIM + N_IDX_HEADS
    w_r = jnp.concatenate([w[:, :c_ki], w[:, c_rx:], w[:, c_ki:c_rx],
                           jnp.zeros((D_MODEL, LANES - IDX_DIM - N_IDX_HEADS), w.dtype)], axis=1)
    tril = jnp.tril(jnp.ones((CHUNK, CHUNK), gm_ws.dtype))
    row2 = lambda a: a.reshape(1, -1)
    return dict(
        g_mix=row2(g_mix[l]), w_in=w_r.astype(MXU_DTYPE),
        rec_cw=rec_conv_w[l], rec_cb=row2(rec_conv_b[l]),
        wa=_block_diag(lru_wa[l]).astype(MXU_DTYPE), ba=row2(lru_ba[l]),
        wx=_block_diag(lru_wx[l]).astype(MXU_DTYPE), bx=row2(lru_bx[l]), lam=row2(lru_lam[l]),
        ln_g=row2(gm_ln_g[l]), ln_b=row2(gm_ln_b[l]), gm_w=gm_ws[l] * tril, gm_b=gm_bs[l],
        w_out=w_out[l].astype(MXU_DTYPE), g_ffn=row2(g_ffn[l]),
        w_up=w_up[l].astype(MXU_DTYPE), ffn_cw=ffn_conv_w[l], ffn_cb=row2(ffn_conv_b[l]),
        w_down=w_down[l].astype(MXU_DTYPE))


def _gmlp_mix_weights(p, c, n_seq):
    w = p['gm_w'][:, :c, :c]
    if n_seq > 1:
        w = jnp.einsum('ab,gts->gatbs', jnp.eye(n_seq, dtype=w.dtype), w).reshape(
            N_GM_HEADS, n_seq * c, n_seq * c)
    bias = jnp.repeat(p['gm_b'][:, :c].T, GM_HEAD_DIM, axis=1)
    return w.astype(MXU_DTYPE), jnp.tile(bias, (n_seq, 1))


def _prompt_layer(x, p, rel_bias, g_final, batch, seq, final):
    tm = 512
    (q, kt, vt, kit, kbf, vtbf, kibf, qi, _, wit, rec, gm) = _in_proj(x, p['g_mix'], p['w_in'], tm, seq)
    att = _prompt_attention(rel_bias, q, qi, wit, kibf, kbf, vtbf, batch, seq)
    zeros8 = jnp.zeros((batch, SUBLANES, D_REC), F32)
    rec_out, hlast, cstate = _rec_branch(rec, zeros8, zeros8, p['rec_cw'], p['rec_cb'], p['wa'], p['ba'],
                                         p['wx'], p['bx'], p['lam'], batch, seq, 256)
    c = min(CHUNK, seq)
    wmix, bias = _gmlp_mix_weights(p, c, 1)
    gm_out, _ = _gmlp(gm, wmix, bias, p['ln_g'], p['ln_b'], tm)
    x1, h2 = _out_proj(x, att, rec_out, gm_out, p['w_out'], p['g_ffn'], tm)
    hist16 = jnp.zeros((batch, HALO, 2 * D_FF), F32)
    x2, upst = _ffn_prompt(h2, x1, hist16, p['w_up'], p['ffn_cw'], p['ffn_cb'], p['w_down'], g_final,
                           batch, seq, tm, final)
    heads_last = lambda a: a.reshape(batch, N_KV_HEADS, HEAD_DIM, seq).transpose(0, 3, 1, 2)
    state = (heads_last(kt), heads_last(vt), kit.transpose(0, 2, 1), hlast[:, SUBLANES - 1, :],
             cstate[:, SUBLANES - (REC_CONV_W - 1):, :], upst[:, HALO - (FFN_CONV_W - 1):, :])
    return x2, state


def _sample_layer(x, p, rel_bias, g_final, l, batch, t_new, final, page_table, cache_k, cache_v, cache_kidx,
                  h0, rec_hist, ffn_hist):
    m = batch * t_new
    (q, k, v, ki, _, _, _, qi, kiwi, _, rec, gm) = _in_proj(x, p['g_mix'], p['w_in'], m)
    att = _sample_attention(page_table, rel_bias, q, qi, kiwi, k, v, cache_k, cache_v, cache_kidx,
                            l, batch, t_new)
    hist8 = _pad_rows_front(rec_hist, SUBLANES)
    h08 = jnp.broadcast_to(h0[:, None, :], (batch, SUBLANES, D_REC))
    rec_out, hlast, cstate = _rec_branch(rec, hist8, h08, p['rec_cw'], p['rec_cb'], p['wa'], p['ba'],
                                         p['wx'], p['bx'], p['lam'], batch, t_new, t_new)
    c = min(CHUNK, t_new)
    wmix, bias = _gmlp_mix_weights(p, c, m // c)
    gm_out, vn = _gmlp(gm, wmix, bias, p['ln_g'], p['ln_b'], m)
    x1, h2 = _out_proj(x, att, rec_out, gm_out, p['w_out'], p['g_ffn'], m)
    zrow = jnp.zeros((batch, t_new - 2, 2 * D_FF), F32)
    p2 = jnp.concatenate([ffn_hist, zrow], axis=1).reshape(m, 2 * D_FF)
    p1 = jnp.concatenate([ffn_hist[:, 1:], zrow, zrow[:, :1]], axis=1).reshape(m, 2 * D_FF)
    x2, up = _ffn_sample(h2, x1, p1, p2, p['w_up'], p['ffn_cw'], p['ffn_cb'], p['w_down'], g_final,
                         t_new, final)
    state = (k.reshape(batch, t_new, N_KV_HEADS, HEAD_DIM), v.reshape(batch, t_new, N_KV_HEADS, HEAD_DIM),
             ki.reshape(batch, t_new, IDX_DIM), hlast[:, SUBLANES - 1, :],
             cstate[:, SUBLANES - (REC_CONV_W - 1):, :],
             up.reshape(batch, t_new, 2 * D_FF)[:, t_new - (FFN_CONV_W - 1):, :],
             vn.reshape(batch, t_new, D_GM))
    return x2, state


def kernel(x_prompt, x_sample, cache_k, cache_v, cache_kidx, page_table, state_lru_h, state_conv_rec,
           state_conv_ffn, rel_bias, g_mix, w_in, rec_conv_w, rec_conv_b, lru_wa, lru_ba, lru_wx, lru_bx,
           lru_lam, gm_ln_g, gm_ln_b, gm_ws, gm_bs, w_out, g_ffn, w_up, ffn_conv_w, ffn_conv_b, w_down,
           g_final):
    batch, seq, _ = x_prompt.shape
    dec_batch, t_new, _ = x_sample.shape
    depth = w_in.shape[0]
    assert seq % 512 == 0 and t_new == SUBLANES and (dec_batch * t_new) % KEY_CHUNK == 0
    xp = x_prompt.reshape(batch * seq, D_MODEL)
    xs = x_sample.reshape(dec_batch * t_new, D_MODEL)
    gf = g_final.reshape(1, D_MODEL)
    p_states, s_states = [], []
    for l in range(depth):
        p = _layer_weights(l, g_mix, w_in, rec_conv_w, rec_conv_b, lru_wa, lru_ba, lru_wx, lru_bx, lru_lam,
                           gm_ln_g, gm_ln_b, gm_ws, gm_bs, w_out, g_ffn, w_up, ffn_conv_w, ffn_conv_b, w_down)
        final = l == depth - 1
        xp, st = _prompt_layer(xp, p, rel_bias, gf, batch, seq, final)
        p_states.append(st)
        xs, st = _sample_layer(xs, p, rel_bias, gf, l, dec_batch, t_new, final, page_table, cache_k, cache_v,
                               cache_kidx, state_lru_h[l], state_conv_rec[l], state_conv_ffn[l])
        s_states.append(st)
    stack = lambda states, i: jnp.stack([s[i] for s in states])
    return ((xp.reshape(batch, seq, D_MODEL), xs.reshape(dec_batch, t_new, D_MODEL))
            + tuple(stack(p_states, i) for i in range(6))
            + tuple(stack(s_states, i) for i in range(7)))
```

```python
import functools
import math

import jax
import jax.numpy as jnp
from jax import lax
from jax.experimental import pallas as pl
from jax.experimental.pallas import tpu as pltpu

F32 = jnp.float32
I32 = jnp.int32
MXU_DTYPE = jnp.bfloat16

D_MODEL = 1024
N_HEADS = 8
HEAD_DIM = 64
D_ATT = N_HEADS * HEAD_DIM
N_KV_HEADS = 2
GROUP = N_HEADS // N_KV_HEADS
D_KV = N_KV_HEADS * HEAD_DIM
N_IDX_HEADS = 4
IDX_DIM = 64
D_QI = N_IDX_HEADS * IDX_DIM
TOPK_MAX = 256
NUM_BUCKETS = 32
MAX_EXACT = NUM_BUCKETS // 2
MAX_DISTANCE = 128
D_REC = 256
N_REC_BLOCKS = 4
REC_CONV_W = 4
LRU_C = 8.0
D_GM = 256
N_GM_HEADS = 4
GM_HEAD_DIM = D_GM // N_GM_HEADS
CHUNK = 128
D_FF = 2816
FFN_CONV_W = 3
EPS = 1e-6
PAGE_SIZE = 128
Q_SCALE = HEAD_DIM ** -0.5

LANES = 128
SUBLANES = 8
VMEM_LIMIT_BYTES = 56 * 1024 * 1024

C_Q = 0
C_KV = C_Q + D_ATT
C_QI = C_KV + 2 * D_KV
C_REC = C_QI + D_QI
C_GM = C_REC + 2 * D_REC
C_KIWI = C_GM + 2 * D_GM
D_IN_PAD = C_KIWI + LANES

KEY_CHUNK = 256
Q_BLOCK = 128
INT_MIN = -2 ** 31
F32_MIN_NORMAL = 2.0 ** -126
PACKED_ROWS = 2 * SUBLANES
NEG_BIG = -2.0 ** 100
COUNT_ACCUMULATORS = 8

BUCKET_LO = tuple(
    b if b <= MAX_EXACT else math.ceil(MAX_EXACT * (MAX_DISTANCE / MAX_EXACT) ** ((b - MAX_EXACT) / (NUM_BUCKETS - MAX_EXACT)))
    for b in range(NUM_BUCKETS))


def _cparams(*sem):
    return pltpu.CompilerParams(dimension_semantics=sem, vmem_limit_bytes=VMEM_LIMIT_BYTES)


def _const_spec(shape):
    nd = len(shape)
    return pl.BlockSpec(shape, lambda *_: (0,) * nd, pipeline_mode=pl.Buffered(1))


def _rmsnorm(x, g):
    return x * lax.rsqrt(jnp.mean(x * x, axis=-1, keepdims=True) + EPS) * g


def _gelu(x):
    return 0.5 * x * (1.0 + jnp.tanh(math.sqrt(2.0 / math.pi) * (x + 0.044715 * (x * x * x))))


def _sigmoid(x):
    return 1.0 / (1.0 + jnp.exp(-x))


def _mm(a, b):
    return jnp.dot(a, b, preferred_element_type=F32)


def _mm_nt(a, b):
    return lax.dot_general(a, b, (((1,), (1,)), ((), ())), preferred_element_type=F32)


def _flush_subnormal(f):
    return jnp.where(jnp.abs(f) < F32_MIN_NORMAL, 0.0, f)


def _pattern_to_f32(c):
    return _flush_subnormal(pltpu.bitcast(jnp.where(c >= 0, c, c ^ 0x7FFFFFFF), F32))


def _tree_sum(xs):
    while len(xs) > 1:
        xs = [a + b for a, b in zip(xs[0::2], xs[1::2])] + ([xs[-1]] if len(xs) % 2 else [])
    return xs[0]


def _pair_loop(n_pairs, chunk_fn, init):
    def body(p, carry):
        return chunk_fn(2 * p + 1, chunk_fn(2 * p, carry))
    return lax.fori_loop(0, n_pairs, body, init)


def _count_above(ref, n_chunks, cand, strict):
    cb = jnp.broadcast_to(cand, (SUBLANES, LANES))
    accs = [jnp.zeros((SUBLANES, LANES), F32)] * COUNT_ACCUMULATORS
    for c in range(n_chunks):
        x = ref[c]
        for j in range(KEY_CHUNK // SUBLANES):
            blk = x[j * SUBLANES:(j + 1) * SUBLANES]
            k = j % COUNT_ACCUMULATORS
            accs[k] = accs[k] + jnp.where((blk > cb) if strict else (blk >= cb), 1.0, 0.0)
    return jnp.sum(_tree_sum(accs), axis=0, keepdims=True)


def _kth_largest(count_ge, need, start):
    def step(i, t):
        cand = t + lax.shift_left(jnp.int32(1), 31 - i)
        return jnp.where(count_ge(_pattern_to_f32(cand)) >= need, cand, t)

    t = lax.fori_loop(0, 32, step, start)
    return jnp.where(t == INT_MIN, -jnp.inf, _pattern_to_f32(t))


def _in_proj_body(x_ref, g_ref, w_ref, q_ref, k_ref, v_ref, ki_ref, kbf_ref, vt_ref, kibf_ref,
                  qi_ref, kiwi_ref, wit_ref, rec_ref, gm_ref, *, transposed_state):
    n_chunks = kbf_ref.shape[0]
    h = _rmsnorm(x_ref[...], g_ref[...]).astype(MXU_DTYPE)

    def proj(lo, hi):
        return _mm(h, w_ref[:, lo:hi])

    q_ref[...] = (proj(C_Q, C_KV) * Q_SCALE).astype(MXU_DTYPE)
    kv = proj(C_KV, C_QI)
    k = kv[:, :D_KV]
    v = kv[:, D_KV:]
    vt = v.T
    kb = k.astype(MXU_DTYPE)
    qi_ref[...] = proj(C_QI, C_REC).astype(MXU_DTYPE)
    rec_ref[...] = proj(C_REC, C_GM)
    gm_ref[...] = proj(C_GM, C_KIWI)
    kiwi = proj(C_KIWI, D_IN_PAD)
    ki = kiwi[:, :IDX_DIM]
    kiwi_ref[...] = kiwi
    kib = ki.astype(MXU_DTYPE)
    kiwi_t = kiwi.T
    wit_ref[...] = kiwi_t[IDX_DIM:IDX_DIM + SUBLANES, :]
    if transposed_state:
        k_ref[0] = k.T
        v_ref[0] = vt
        ki_ref[0] = kiwi_t[:IDX_DIM, :]
    else:
        k_ref[...] = k
        v_ref[...] = v
        ki_ref[...] = ki
    for c in range(n_chunks):
        rows = slice(c * KEY_CHUNK, (c + 1) * KEY_CHUNK)
        kbf_ref[c] = kb[rows]
        kibf_ref[c] = kib[rows]
        vt_ref[c] = vt[:, rows].astype(MXU_DTYPE)


def _in_proj(x, g, w, tm, seq=None):
    m = x.shape[0]
    nc = tm // KEY_CHUNK
    row = lambda n: pl.BlockSpec((tm, n), lambda i: (i, 0))
    chunked = lambda a, b: pl.BlockSpec((nc, a, b), lambda i: (i, 0, 0))
    if seq is None:
        state_shape = lambda n: jax.ShapeDtypeStruct((m, n), F32)
        state_spec = row
    else:
        tps = seq // tm
        state_shape = lambda n: jax.ShapeDtypeStruct((m // seq, n, seq), F32)
        state_spec = lambda n: pl.BlockSpec((1, n, tm), lambda i: (i // tps, 0, i % tps))
    out_shape = (
        jax.ShapeDtypeStruct((m, D_ATT), MXU_DTYPE),
        state_shape(D_KV),
        state_shape(D_KV),
        state_shape(IDX_DIM),
        jax.ShapeDtypeStruct((m // KEY_CHUNK, KEY_CHUNK, D_KV), MXU_DTYPE),
        jax.ShapeDtypeStruct((m // KEY_CHUNK, D_KV, KEY_CHUNK), MXU_DTYPE),
        jax.ShapeDtypeStruct((m // KEY_CHUNK, KEY_CHUNK, IDX_DIM), MXU_DTYPE),
        jax.ShapeDtypeStruct((m, D_QI), MXU_DTYPE),
        jax.ShapeDtypeStruct((m, LANES), F32),
        jax.ShapeDtypeStruct((SUBLANES, m), F32),
        jax.ShapeDtypeStruct((m, 2 * D_REC), F32),
        jax.ShapeDtypeStruct((m, 2 * D_GM), F32),
    )
    out_specs = (
        row(D_ATT), state_spec(D_KV), state_spec(D_KV), state_spec(IDX_DIM),
        chunked(KEY_CHUNK, D_KV), chunked(D_KV, KEY_CHUNK), chunked(KEY_CHUNK, IDX_DIM),
        row(D_QI), row(LANES), pl.BlockSpec((SUBLANES, tm), lambda i: (0, i)),
        row(2 * D_REC), row(2 * D_GM),
    )
    return pl.pallas_call(
        functools.partial(_in_proj_body, transposed_state=seq is not None),
        grid=(m // tm,),
        in_specs=[row(D_MODEL), _const_spec((1, D_MODEL)), _const_spec((D_MODEL, D_IN_PAD))],
        out_specs=out_specs,
        out_shape=out_shape,
        compiler_params=_cparams("arbitrary"),
        name="in_proj",
    )(x, g, w)


def _bias_table(rb_ref, head, delta, shape):
    row = lax.broadcasted_iota(I32, shape, 0)
    lane = lax.broadcasted_iota(I32, shape, 1)
    d = delta + lane - row
    far = rb_ref[NUM_BUCKETS - 1, head]
    val = jnp.full(shape, rb_ref[0, head] - far, F32)
    for b in range(1, NUM_BUCKETS - 1):
        val = jnp.where(d >= BUCKET_LO[b], rb_ref[b, head] - far, val)
    return jnp.where(d >= BUCKET_LO[NUM_BUCKETS - 1], 0.0, val)


def _prompt_attn_body(rb_ref, q_ref, qi_ref, wit_ref, ki_ref, k_ref, vt_ref, att_ref,
                      s_s, mb_s, tab_s, tri_s, thr_s, need_s, m_s, acc_s, qpad_s, *, topk):
    b = pl.program_id(0)
    i = pl.program_id(1)
    n_pairs = i // 4 + 1
    ck = (KEY_CHUNK, Q_BLOCK)

    @pl.when((b == 0) & (i == 0))
    def _init_tables():
        r = lax.broadcasted_iota(I32, (KEY_CHUNK, KEY_CHUNK), 0)
        c = lax.broadcasted_iota(I32, (KEY_CHUNK, KEY_CHUNK), 1)
        tri_s[...] = jnp.where(c <= r, 1.0, 0.0).astype(MXU_DTYPE)

        def per_head(h, carry):
            for ti in range(4):
                tab_s[ti, h] = _bias_table(rb_ref, h, ti * Q_BLOCK, ck)
            return carry

        lax.fori_loop(0, N_HEADS, per_head, 0)

    row = lax.broadcasted_iota(I32, ck, 0)
    lane = lax.broadcasted_iota(I32, ck, 1)
    q_pos = i * Q_BLOCK + lane

    qi = qi_ref[...]
    qis = jnp.concatenate([qi[:, h * IDX_DIM:(h + 1) * IDX_DIM] for h in range(N_IDX_HEADS)], axis=0)
    wit = wit_ref[...]
    w_row = jnp.concatenate([wit[h:h + 1, :] for h in range(N_IDX_HEADS)], axis=1)

    def score_chunk(c, carry):
        s4 = jnp.maximum(_mm_nt(ki_ref[c], qis), 0.0) * w_row
        s = _tree_sum([s4[:, h * Q_BLOCK:(h + 1) * Q_BLOCK] for h in range(N_IDX_HEADS)])
        valid = (c * KEY_CHUNK + row) <= q_pos
        s_s[c] = jnp.where(valid, s, -jnp.inf)
        return carry

    _pair_loop(n_pairs, score_chunk, 0)

    def search(n_chunks):
        t = _kth_largest(lambda cand: _count_above(s_s, n_chunks, cand, False), topk,
                         jnp.full((1, Q_BLOCK), INT_MIN, I32))
        thr_s[0:1, :] = t
        need_s[0:1, :] = topk - _count_above(s_s, n_chunks, t, True)

    n_chunks_causal = (i + 2) // 2
    for n in range(1, s_s.shape[0] + 1):
        pl.when(n_chunks_causal == n)(functools.partial(search, n))
    t = thr_s[0:1, :]
    need = need_s[0:1, :]

    tri = tri_s[...]

    def mask_chunk(c, carry):
        sc = s_s[c]
        tie = (sc == t) & (sc > -jnp.inf)
        tie_f = jnp.where(tie, 1.0, 0.0)
        incl = _mm(tri, tie_f.astype(MXU_DTYPE))
        rank = carry + incl - tie_f
        sel = (sc > t) | (tie & (rank < need))
        mb_s[c] = jnp.where(sel, 0.0, -jnp.inf).astype(mb_s.dtype)
        return carry + incl[KEY_CHUNK - 1:KEY_CHUNK, :]

    _pair_loop(n_pairs, mask_chunk, jnp.zeros((1, Q_BLOCK), F32))

    q = q_ref[...]
    zeros = jnp.zeros((Q_BLOCK, HEAD_DIM), MXU_DTYPE)
    for h in range(N_HEADS):
        qh = q[:, h * HEAD_DIM:(h + 1) * HEAD_DIM]
        parts = [qh, zeros] if h < GROUP else [zeros, qh]
        qpad_s[h * Q_BLOCK:(h + 1) * Q_BLOCK, :] = jnp.concatenate(parts, axis=1)
    m_s[...] = jnp.full(m_s.shape, NEG_BIG, F32)
    acc_s[...] = jnp.zeros(acc_s.shape, F32)
    ones_rows = jnp.ones((PACKED_ROWS, 2 * KEY_CHUNK), MXU_DTYPE)

    def attend_pair(p, near):
        c0, c1 = 2 * p, 2 * p + 1
        keys = jnp.concatenate([k_ref[c0], k_ref[c1]], axis=0)
        logits = _mm_nt(keys, qpad_s[...])
        mb = jnp.concatenate([mb_s[c0], mb_s[c1]], axis=0)
        if near:
            t0 = jnp.clip(i - 2 * c0, 0, 3)
            t1 = jnp.clip(i - 2 * c1, 0, 3)
        for g in range(N_KV_HEADS):
            ps, alphas = [], []
            for hh in range(GROUP):
                h = g * GROUP + hh
                cols = slice(h * Q_BLOCK, (h + 1) * Q_BLOCK)
                lt = logits[:, cols]
                if near:
                    lt = lt + jnp.concatenate([tab_s[t0, h], tab_s[t1, h]], axis=0)
                lt = lt.astype(MXU_DTYPE) + mb
                m_old = m_s[0:1, cols]
                m_new = jnp.maximum(m_old, jnp.max(lt, axis=0, keepdims=True).astype(F32))
                alphas.append(jnp.exp(m_old - m_new))
                ps.append(jnp.exp(lt - m_new.astype(MXU_DTYPE)))
                m_s[0:1, cols] = m_new
            pg = jnp.concatenate(ps, axis=1)
            ag = jnp.concatenate(alphas, axis=1)
            rows = slice(g * HEAD_DIM, (g + 1) * HEAD_DIM)
            vt = jnp.concatenate([jnp.concatenate([vt_ref[c0, rows, :], vt_ref[c1, rows, :]], axis=1), ones_rows],
                                 axis=0)
            acc_s[g] = acc_s[g] * ag + _mm(vt, pg)

    def far_pair(p, carry):
        attend_pair(2 * p, False)
        attend_pair(2 * p + 1, False)
        return carry

    n_far = jnp.maximum(n_pairs - 2, 0)
    lax.fori_loop(0, n_far // 2, far_pair, 0)

    @pl.when(n_far % 2 == 1)
    def _odd_far_pair():
        attend_pair(n_far - 1, False)

    @pl.when(n_pairs >= 2)
    def _near_pairs():
        attend_pair(n_pairs - 2, True)
        attend_pair(n_pairs - 1, True)

    @pl.when(n_pairs == 1)
    def _only_pair():
        attend_pair(0, True)

    for h in range(N_HEADS):
        g, hh = divmod(h, GROUP)
        cols = slice(hh * Q_BLOCK, (hh + 1) * Q_BLOCK)
        o = acc_s[g][:HEAD_DIM, cols] / acc_s[g][HEAD_DIM:HEAD_DIM + 1, cols]
        att_ref[:, h * HEAD_DIM:(h + 1) * HEAD_DIM] = o.T.astype(att_ref.dtype)


def _prompt_attention(rel_bias, q, qi, wit, kibf, kbf, vtbf, batch, seq):
    nq = seq // Q_BLOCK
    nc = seq // KEY_CHUNK
    topk = float(min(TOPK_MAX, seq // 4))
    qrow = lambda n: pl.BlockSpec((Q_BLOCK, n), lambda b, i: (b * nq + i, 0))
    per_batch = lambda a, c: pl.BlockSpec((nc, a, c), lambda b, i: (b, 0, 0))
    return pl.pallas_call(
        functools.partial(_prompt_attn_body, topk=topk),
        grid=(batch, nq),
        in_specs=[
            pl.BlockSpec(memory_space=pltpu.SMEM),
            qrow(D_ATT), qrow(D_QI),
            pl.BlockSpec((SUBLANES, Q_BLOCK), lambda b, i: (0, b * nq + i)),
            per_batch(KEY_CHUNK, IDX_DIM), per_batch(KEY_CHUNK, D_KV), per_batch(D_KV, KEY_CHUNK),
        ],
        out_specs=qrow(D_ATT),
        out_shape=jax.ShapeDtypeStruct((batch * seq, D_ATT), MXU_DTYPE),
        scratch_shapes=[
            pltpu.VMEM((nc, KEY_CHUNK, Q_BLOCK), F32),
            pltpu.VMEM((nc, KEY_CHUNK, Q_BLOCK), MXU_DTYPE),
            pltpu.VMEM((4, N_HEADS, KEY_CHUNK, Q_BLOCK), F32),
            pltpu.VMEM((KEY_CHUNK, KEY_CHUNK), MXU_DTYPE),
            pltpu.VMEM((SUBLANES, Q_BLOCK), F32),
            pltpu.VMEM((SUBLANES, Q_BLOCK), F32),
            pltpu.VMEM((SUBLANES, N_HEADS * Q_BLOCK), F32),
            pltpu.VMEM((N_KV_HEADS, HEAD_DIM + PACKED_ROWS, GROUP * Q_BLOCK), F32),
            pltpu.VMEM((N_HEADS * Q_BLOCK, D_KV), MXU_DTYPE),
        ],
        compiler_params=_cparams("arbitrary", "arbitrary"),
        name="prompt_attn",
    )(rel_bias, q, qi, wit, kibf, kbf, vtbf)


def _sample_attn_body(pt_ref, rb_ref, q_ref, qi_ref, kiwi_ref, knew_ref, vnew_ref,
                      ck_hbm, cv_hbm, cki_hbm, att_ref,
                      ktbuf, vtbuf, kitbuf, sem, *, topk, n_pages, layer, t_new):
    b = pl.program_id(0)
    past = n_pages * PAGE_SIZE
    slot = b % 2

    def page_copies(seq, buf, p):
        phys = pt_ref[seq, p]
        cols = pl.ds(pl.multiple_of(p * PAGE_SIZE, PAGE_SIZE), PAGE_SIZE)
        return (pltpu.make_async_copy(ck_hbm.at[layer, phys], ktbuf.at[buf, :, cols], sem.at[0, buf]),
                pltpu.make_async_copy(cv_hbm.at[layer, phys], vtbuf.at[buf, :, cols], sem.at[1, buf]),
                pltpu.make_async_copy(cki_hbm.at[layer, phys], kitbuf.at[buf, :, cols], sem.at[2, buf]))

    def start_pages(seq, buf):
        def body(p, carry):
            for cp in page_copies(seq, buf, p):
                cp.start()
            return carry
        lax.fori_loop(0, n_pages, body, 0)

    def wait_pages(seq, buf):
        def body(p, carry):
            for cp in page_copies(seq, buf, p):
                cp.wait()
            return carry
        lax.fori_loop(0, n_pages, body, 0)

    @pl.when(b == 0)
    def _first_fetch():
        start_pages(0, 0)

    @pl.when(b + 1 < pl.num_programs(0))
    def _prefetch_next():
        start_pages(b + 1, 1 - slot)

    qi = qi_ref[...]
    qis = jnp.concatenate([qi[:, h * IDX_DIM:(h + 1) * IDX_DIM] for h in range(N_IDX_HEADS)], axis=0)
    kiwi = kiwi_ref[...]
    w_col = jnp.concatenate([kiwi[:, IDX_DIM + h:IDX_DIM + h + 1] for h in range(N_IDX_HEADS)], axis=0)
    q = q_ref[...]
    zeros = jnp.zeros((t_new, HEAD_DIM), MXU_DTYPE)
    qpad = jnp.concatenate(
        [jnp.concatenate([q[:, h * HEAD_DIM:(h + 1) * HEAD_DIM], zeros] if h < GROUP else
                         [zeros, q[:, h * HEAD_DIM:(h + 1) * HEAD_DIM]], axis=1)
         for h in range(N_HEADS)], axis=0)

    wait_pages(b, slot)

    def idx_score(qk):
        s4 = jnp.maximum(qk, 0.0) * w_col
        s = s4[0:t_new]
        for h in range(1, N_IDX_HEADS):
            s = s + s4[h * t_new:(h + 1) * t_new]
        return s

    s_p = idx_score(_mm(qis, kitbuf[slot].astype(MXU_DTYPE)))
    ki_new = kiwi[:, :IDX_DIM].astype(MXU_DTYPE)
    ki_new = jnp.concatenate([ki_new, jnp.zeros((LANES - t_new, IDX_DIM), MXU_DTYPE)], axis=0)
    rown = lax.broadcasted_iota(I32, (t_new, LANES), 0)
    lanen = lax.broadcasted_iota(I32, (t_new, LANES), 1)
    valid_n = lanen <= rown
    s_n = jnp.where(valid_n, idx_score(_mm_nt(qis, ki_new)), -jnp.inf)

    def count(cand, strict):
        cmp = (lambda x: x > cand) if strict else (lambda x: x >= cand)
        return (jnp.sum(jnp.where(cmp(s_p), 1.0, 0.0), axis=1, keepdims=True)
                + jnp.sum(jnp.where(cmp(s_n), 1.0, 0.0), axis=1, keepdims=True))

    t = _kth_largest(lambda cand: count(cand, False), topk, jnp.full((t_new, 1), INT_MIN, I32))
    need = topk - count(t, True)

    r = lax.broadcasted_iota(I32, (KEY_CHUNK, KEY_CHUNK), 0)
    c = lax.broadcasted_iota(I32, (KEY_CHUNK, KEY_CHUNK), 1)
    tri = jnp.where(r <= c, 1.0, 0.0).astype(MXU_DTYPE)
    tie_p = s_p == t
    carry = jnp.zeros((t_new, 1), F32)
    mask_parts = []
    for ch in range(past // KEY_CHUNK):
        cols = slice(ch * KEY_CHUNK, (ch + 1) * KEY_CHUNK)
        tie_c = tie_p[:, cols]
        tie_f = jnp.where(tie_c, 1.0, 0.0)
        incl = _mm(tie_f.astype(MXU_DTYPE), tri)
        sel = (s_p[:, cols] > t) | (tie_c & ((carry + incl - tie_f) < need))
        mask_parts.append(jnp.where(sel, 0.0, -jnp.inf))
        carry = carry + incl[:, KEY_CHUNK - 1:KEY_CHUNK]
    mb_p = jnp.concatenate(mask_parts, axis=1)
    tie_n = (s_n == t) & valid_n
    tie_nf = jnp.where(tie_n, 1.0, 0.0)
    incl_n = _mm(tie_nf.astype(MXU_DTYPE), tri[:LANES, :LANES])
    sel_n = (s_n > t) | (tie_n & ((carry + incl_n - tie_nf) < need))
    mb_n = jnp.where(sel_n & valid_n, 0.0, -jnp.inf)

    lg_p = _mm(qpad, ktbuf[slot].astype(MXU_DTYPE))
    k_new = jnp.concatenate([knew_ref[...].astype(MXU_DTYPE),
                             jnp.zeros((LANES - t_new, D_KV), MXU_DTYPE)], axis=0)
    lg_n = _mm_nt(qpad, k_new)
    near = slice(past - LANES, past)
    lp_rows, ln_rows = [], []
    for h in range(N_HEADS):
        rows = slice(h * t_new, (h + 1) * t_new)
        far = rb_ref[NUM_BUCKETS - 1, h]

        def bias(d):
            val = jnp.full(d.shape, rb_ref[0, h] - far, F32)
            for bk in range(1, NUM_BUCKETS - 1):
                val = jnp.where(d >= BUCKET_LO[bk], rb_ref[bk, h] - far, val)
            return jnp.where(d >= BUCKET_LO[NUM_BUCKETS - 1], 0.0, val)

        lp = lg_p[rows] + mb_p
        lp_near = lp[:, near] + bias(rown + (LANES - lanen))
        lp_rows.append(jnp.concatenate([lp[:, :past - LANES], lp_near], axis=1))
        ln_rows.append(lg_n[rows] + mb_n + bias(rown - lanen))
    lp_all = jnp.concatenate(lp_rows, axis=0)
    ln_all = jnp.concatenate(ln_rows, axis=0)
    m = jnp.maximum(jnp.max(lp_all, axis=1, keepdims=True), jnp.max(ln_all, axis=1, keepdims=True))
    p_p = jnp.exp(lp_all - m)
    p_n = jnp.exp(ln_all - m)
    denom = jnp.sum(p_p, axis=1, keepdims=True) + jnp.sum(p_n, axis=1, keepdims=True)
    v_new = jnp.concatenate([vnew_ref[...].astype(MXU_DTYPE),
                             jnp.zeros((LANES - t_new, D_KV), MXU_DTYPE)], axis=0)
    o = (_mm_nt(p_p.astype(MXU_DTYPE), vtbuf[slot].astype(MXU_DTYPE))
         + _mm(p_n.astype(MXU_DTYPE), v_new)) / denom
    for h in range(N_HEADS):
        g = h // GROUP
        att_ref[:, h * HEAD_DIM:(h + 1) * HEAD_DIM] = (
            o[h * t_new:(h + 1) * t_new, g * HEAD_DIM:(g + 1) * HEAD_DIM].astype(att_ref.dtype))


def _sample_attention(page_table, rel_bias, q, qi, kiwi, k_new, v_new, cache_k, cache_v, cache_kidx,
                      layer, batch, t_new):
    n_pages = page_table.shape[1]
    past = n_pages * PAGE_SIZE
    topk = float(min(TOPK_MAX, (past + t_new) // 4))
    depth, n_pool = cache_k.shape[:2]
    ck = cache_k.transpose(0, 1, 3, 4, 2).reshape(depth, n_pool, D_KV, PAGE_SIZE)
    cv = cache_v.transpose(0, 1, 3, 4, 2).reshape(depth, n_pool, D_KV, PAGE_SIZE)
    cki = cache_kidx.transpose(0, 1, 3, 2)
    qrow = lambda n: pl.BlockSpec((t_new, n), lambda b, pt: (b, 0))
    hbm = pl.BlockSpec(memory_space=pl.ANY)
    grid_spec = pltpu.PrefetchScalarGridSpec(
        num_scalar_prefetch=1,
        grid=(batch,),
        in_specs=[pl.BlockSpec(memory_space=pltpu.SMEM),
                  qrow(D_ATT), qrow(D_QI), qrow(LANES), qrow(D_KV), qrow(D_KV), hbm, hbm, hbm],
        out_specs=qrow(D_ATT),
        scratch_shapes=[
            pltpu.VMEM((2, D_KV, past), F32),
            pltpu.VMEM((2, D_KV, past), F32),
            pltpu.VMEM((2, IDX_DIM, past), F32),
            pltpu.SemaphoreType.DMA((3, 2)),
        ],
    )
    return pl.pallas_call(
        functools.partial(_sample_attn_body, topk=topk, n_pages=n_pages, layer=layer, t_new=t_new),
        grid_spec=grid_spec,
        out_shape=jax.ShapeDtypeStruct((batch * t_new, D_ATT), MXU_DTYPE),
        compiler_params=_cparams("arbitrary"),
        name="sample_attn",
    )(page_table, rel_bias, q, qi, kiwi, k_new, v_new, ck, cv, cki)


def _rec_body(rec_ref, hist_ref, h0_ref, cw_ref, cb_ref, wa_ref, ba_ref, wx_ref, bx_ref, lam_ref,
              out_ref, hlast_ref, cstate_ref, hcar, xprev):
    tt = rec_ref.shape[0]

    @pl.when(pl.program_id(1) == 0)
    def _load_state():
        hcar[...] = h0_ref[0]
        xprev[...] = hist_ref[0]

    rx = rec_ref[:, :D_REC]
    rg = rec_ref[:, D_REC:]
    ext = jnp.concatenate([xprev[...], rx], axis=0)
    xc = cb_ref[...]
    for j in range(REC_CONV_W - 1):
        xc = xc + pltpu.roll(ext, REC_CONV_W - 1 - j, 0)[SUBLANES:] * cw_ref[j:j + 1, :]
    xc = xc + rx * cw_ref[REC_CONV_W - 1:REC_CONV_W, :]
    xcb = xc.astype(MXU_DTYPE)
    r = _sigmoid(_mm(xcb, wa_ref[...]) + ba_ref[...])
    gi = _sigmoid(_mm(xcb, wx_ref[...]) + bx_ref[...])
    nl = -lam_ref[...]
    softplus = jnp.maximum(nl, 0.0) + jnp.log1p(jnp.exp(-jnp.abs(nl)))
    log_a = -LRU_C * r * softplus
    a = jnp.exp(log_a)
    u = jnp.sqrt(-jnp.tanh(log_a) * (a * a + 1.0)) * (gi * xc)
    row = lax.broadcasted_iota(I32, (tt, D_REC), 0)
    s = 1
    while s < tt:
        keep = row >= s
        u = jnp.where(keep, u + a * pltpu.roll(u, s, 0), u)
        a = jnp.where(keep, a * pltpu.roll(a, s, 0), a)
        s *= 2
    hs = u + a * hcar[0:1, :]
    hcar[...] = jnp.broadcast_to(hs[tt - 1:tt, :], hcar.shape)
    xprev[...] = rx[tt - SUBLANES:, :]
    out_ref[...] = (_gelu(rg) * hs).astype(out_ref.dtype)
    hlast_ref[0] = hs[tt - SUBLANES:, :]
    cstate_ref[0] = rx[tt - SUBLANES:, :]


def _rec_branch(rec, hist8, h08, cw, cb, wa, ba, wx, bx, lam, batch, t_len, tt):
    nt = t_len // tt
    state = pl.BlockSpec((1, SUBLANES, D_REC), lambda b, t: (b, 0, 0))
    vec = _const_spec((1, D_REC))
    return pl.pallas_call(
        _rec_body,
        grid=(batch, nt),
        in_specs=[pl.BlockSpec((tt, 2 * D_REC), lambda b, t: (b * nt + t, 0)), state, state,
                  _const_spec((REC_CONV_W, D_REC)), vec, _const_spec((D_REC, D_REC)), vec,
                  _const_spec((D_REC, D_REC)), vec, vec],
        out_specs=(pl.BlockSpec((tt, D_REC), lambda b, t: (b * nt + t, 0)), state, state),
        out_shape=(jax.ShapeDtypeStruct((batch * t_len, D_REC), MXU_DTYPE),
                   jax.ShapeDtypeStruct((batch, SUBLANES, D_REC), F32),
                   jax.ShapeDtypeStruct((batch, SUBLANES, D_REC), F32)),
        scratch_shapes=[pltpu.VMEM((SUBLANES, D_REC), F32), pltpu.VMEM((SUBLANES, D_REC), F32)],
        compiler_params=_cparams("arbitrary", "arbitrary"),
        name="rec_branch",
    )(rec, hist8, h08, cw, cb, wa, ba, wx, bx, lam)


def _gmlp_body(gm_ref, wmix_ref, bias_ref, lng_ref, lnb_ref, out_ref, vn_ref):
    rows = wmix_ref.shape[1]
    n_sub = gm_ref.shape[0] // rows
    head_of_lane = lax.broadcasted_iota(I32, (rows, D_GM), 1) // GM_HEAD_DIM
    for sb in range(n_sub):
        sl = slice(sb * rows, (sb + 1) * rows)
        u = _gelu(gm_ref[sl, :D_GM])
        gv = _gelu(gm_ref[sl, D_GM:])
        xc = gv - jnp.mean(gv, axis=-1, keepdims=True)
        var = jnp.mean(xc * xc, axis=-1, keepdims=True)
        vn = xc * lax.rsqrt(var + EPS) * lng_ref[...] + lnb_ref[...]
        vn_ref[sl, :] = vn
        vnb = vn.astype(MXU_DTYPE)
        mix = jnp.zeros((rows, D_GM), F32)
        for g in range(N_GM_HEADS):
            mix = jnp.where(head_of_lane == g, _mm(wmix_ref[g], vnb), mix)
        out_ref[sl, :] = (u * (mix + bias_ref[...])).astype(out_ref.dtype)


def _gmlp(gm, wmix, bias, lng, lnb, tm):
    m = gm.shape[0]
    rows = wmix.shape[1]
    return pl.pallas_call(
        _gmlp_body,
        grid=(m // tm,),
        in_specs=[pl.BlockSpec((tm, 2 * D_GM), lambda i: (i, 0)),
                  _const_spec((N_GM_HEADS, rows, rows)), _const_spec((rows, D_GM)),
                  _const_spec((1, D_GM)), _const_spec((1, D_GM))],
        out_specs=(pl.BlockSpec((tm, D_GM), lambda i: (i, 0)), pl.BlockSpec((tm, D_GM), lambda i: (i, 0))),
        out_shape=(jax.ShapeDtypeStruct((m, D_GM), MXU_DTYPE), jax.ShapeDtypeStruct((m, D_GM), F32)),
        compiler_params=_cparams("arbitrary"),
        name="gmlp",
    )(gm, wmix, bias, lng, lnb)


def _out_proj_body(x_ref, att_ref, rec_ref, gm_ref, w_ref, g_ref, x1_ref, h2_ref):
    y = (_mm(att_ref[...], w_ref[0:D_ATT, :])
         + _mm(rec_ref[...], w_ref[D_ATT:D_ATT + D_REC, :])
         + _mm(gm_ref[...], w_ref[D_ATT + D_REC:, :]))
    x1 = x_ref[...] + y
    x1_ref[...] = x1
    h2_ref[...] = _rmsnorm(x1, g_ref[...]).astype(h2_ref.dtype)


def _out_proj(x, att, rec, gm, w, g, tm):
    m = x.shape[0]
    row = lambda n: pl.BlockSpec((tm, n), lambda i: (i, 0))
    return pl.pallas_call(
        _out_proj_body,
        grid=(m // tm,),
        in_specs=[row(D_MODEL), row(D_ATT), row(D_REC), row(D_GM),
                  _const_spec((D_MODEL, D_MODEL)), _const_spec((1, D_MODEL))],
        out_specs=(row(D_MODEL), row(D_MODEL)),
        out_shape=(jax.ShapeDtypeStruct((m, D_MODEL), F32), jax.ShapeDtypeStruct((m, D_MODEL), MXU_DTYPE)),
        compiler_params=_cparams("arbitrary"),
        name="out_proj",
    )(x, att, rec, gm, w, g)


FF_TILE = 256
HALO = 16


def _ffn_finish(x1, acc, gf_ref, out_ref, final):
    x2 = x1 + acc
    out_ref[...] = _rmsnorm(x2, gf_ref[...]) if final else x2


def _ffn_prompt_body(h_ref, halo_ref, x1_ref, hist_ref, wup_ref, cw_ref, cb_ref, wdn_ref, gf_ref,
                     out_ref, upst_ref, act_s, *, tiles_per_seq, final):
    tm = h_ref.shape[0]
    first = (pl.program_id(0) % tiles_per_seq) == 0
    h_ext = jnp.concatenate([halo_ref[...], h_ref[...]], axis=0)
    for j in range(D_FF // FF_TILE):
        halves = []
        for base in (0, D_FF):
            cols = slice(base + j * FF_TILE, base + (j + 1) * FF_TILE)
            up = _mm(h_ext, wup_ref[:, cols])
            up_m = up[HALO:]
            upst_ref[0, :, cols] = up_m[tm - HALO:, :]
            ext = jnp.concatenate([jnp.where(first, hist_ref[0, :, cols], up[:HALO]), up_m], axis=0)
            uc = cb_ref[:, cols]
            for jj in range(FFN_CONV_W - 1):
                uc = uc + pltpu.roll(ext, FFN_CONV_W - 1 - jj, 0)[HALO:] * cw_ref[jj:jj + 1, cols]
            halves.append(uc + up_m * cw_ref[FFN_CONV_W - 1:FFN_CONV_W, cols])
        act_s[:, j * FF_TILE:(j + 1) * FF_TILE] = (_gelu(halves[0]) * halves[1]).astype(MXU_DTYPE)
    _ffn_finish(x1_ref[...], _mm(act_s[...], wdn_ref[...]), gf_ref, out_ref, final)


def _ffn_prompt(h2, x1, hist16, wup, cw, cb, wdn, gf, batch, seq, tm, final):
    m = h2.shape[0]
    tps = seq // tm
    row = lambda n: pl.BlockSpec((tm, n), lambda i: (i, 0))
    state = pl.BlockSpec((1, HALO, 2 * D_FF), lambda i: (i // tps, 0, 0))
    return pl.pallas_call(
        functools.partial(_ffn_prompt_body, tiles_per_seq=tps, final=final),
        grid=(m // tm,),
        in_specs=[row(D_MODEL),
                  pl.BlockSpec((HALO, D_MODEL), lambda i: (jnp.maximum(i * (tm // HALO) - 1, 0), 0)),
                  row(D_MODEL), state,
                  _const_spec((D_MODEL, 2 * D_FF)), _const_spec((FFN_CONV_W, 2 * D_FF)),
                  _const_spec((1, 2 * D_FF)), _const_spec((D_FF, D_MODEL)), _const_spec((1, D_MODEL))],
        out_specs=(row(D_MODEL), state),
        out_shape=(jax.ShapeDtypeStruct((m, D_MODEL), F32),
                   jax.ShapeDtypeStruct((batch, HALO, 2 * D_FF), F32)),
        scratch_shapes=[pltpu.VMEM((tm, D_FF), MXU_DTYPE)],
        compiler_params=_cparams("arbitrary"),
        name="ffn_prompt",
    )(h2, h2, x1, hist16, wup, cw, cb, wdn, gf)


def _ffn_sample_body(h_ref, x1_ref, p1_ref, p2_ref, wup_ref, cw_ref, cb_ref, wdn_ref, gf_ref,
                     out_ref, up_ref, *, t_new, final):
    tm = h_ref.shape[0]
    hm = h_ref[...]
    pos = lax.broadcasted_iota(I32, (tm, FF_TILE), 0) % t_new
    acc = jnp.zeros((tm, D_MODEL), F32)
    for j in range(D_FF // FF_TILE):
        halves = []
        for base in (0, D_FF):
            cols = slice(base + j * FF_TILE, base + (j + 1) * FF_TILE)
            up = _mm(hm, wup_ref[:, cols])
            up_ref[:, cols] = up
            tap2 = jnp.where(pos < 2, p2_ref[:, cols], pltpu.roll(up, 2, 0))
            tap1 = jnp.where(pos < 1, p1_ref[:, cols], pltpu.roll(up, 1, 0))
            halves.append(cb_ref[:, cols] + tap2 * cw_ref[0:1, cols] + tap1 * cw_ref[1:2, cols]
                          + up * cw_ref[2:3, cols])
        act = (_gelu(halves[0]) * halves[1]).astype(MXU_DTYPE)
        acc = acc + _mm(act, wdn_ref[j * FF_TILE:(j + 1) * FF_TILE, :])
    _ffn_finish(x1_ref[...], acc, gf_ref, out_ref, final)


def _ffn_sample(h2, x1, p1, p2, wup, cw, cb, wdn, gf, t_new, final):
    m = h2.shape[0]
    full = lambda n: _const_spec((m, n))
    return pl.pallas_call(
        functools.partial(_ffn_sample_body, t_new=t_new, final=final),
        grid=(1,),
        in_specs=[full(D_MODEL), full(D_MODEL), full(2 * D_FF), full(2 * D_FF),
                  _const_spec((D_MODEL, 2 * D_FF)), _const_spec((FFN_CONV_W, 2 * D_FF)),
                  _const_spec((1, 2 * D_FF)), _const_spec((D_FF, D_MODEL)), _const_spec((1, D_MODEL))],
        out_specs=(full(D_MODEL), full(2 * D_FF)),
        out_shape=(jax.ShapeDtypeStruct((m, D_MODEL), F32), jax.ShapeDtypeStruct((m, 2 * D_FF), F32)),
        compiler_params=_cparams("arbitrary"),
        name="ffn_sample",
    )(h2, x1, p1, p2, wup, cw, cb, wdn, gf)


def _block_diag(w):
    n, blk, _ = w.shape
    out = jnp.zeros((n * blk, n * blk), w.dtype)
    for i in range(n):
        out = out.at[i * blk:(i + 1) * blk, i * blk:(i + 1) * blk].set(w[i])
    return out


def _pad_rows_front(a, rows):
    pad = rows - a.shape[1]
    return jnp.pad(a, ((0, 0), (pad, 0), (0, 0)))


def _layer_weights(l, g_mix, w_in, rec_conv_w, rec_conv_b, lru_wa, lru_ba, lru_wx, lru_bx, lru_lam,
                   gm_ln_g, gm_ln_b, gm_ws, gm_bs, w_out, g_ffn, w_up, ffn_conv_w, ffn_conv_b, w_down):
    w = w_in[l]
    c_ki = D_ATT + 2 * D_KV + D_QI
    c_rx = c_ki + IDX_DIM + N_IDX_HEADS
    w_r = jnp.concatenate([w[:, :c_ki], w[:, c_rx:], w[:, c_ki:c_rx],
                           jnp.zeros((D_MODEL, LANES - IDX_DIM - N_IDX_HEADS), w.dtype)], axis=1)
    tril = jnp.tril(jnp.ones((CHUNK, CHUNK), gm_ws.dtype))
    row2 = lambda a: a.reshape(1, -1)
    return dict(
        g_mix=row2(g_mix[l]), w_in=w_r.astype(MXU_DTYPE),
        rec_cw=rec_conv_w[l], rec_cb=row2(rec_conv_b[l]),
        wa=_block_diag(lru_wa[l]).astype(MXU_DTYPE), ba=row2(lru_ba[l]),
        wx=_block_diag(lru_wx[l]).astype(MXU_DTYPE), bx=row2(lru_bx[l]), lam=row2(lru_lam[l]),
        ln_g=row2(gm_ln_g[l]), ln_b=row2(gm_ln_b[l]), gm_w=gm_ws[l] * tril, gm_b=gm_bs[l],
        w_out=w_out[l].astype(MXU_DTYPE), g_ffn=row2(g_ffn[l]),
        w_up=w_up[l].astype(MXU_DTYPE), ffn_cw=ffn_conv_w[l], ffn_cb=row2(ffn_conv_b[l]),
        w_down=w_down[l].astype(MXU_DTYPE))


def _gmlp_mix_weights(p, c, n_seq):
    w = p['gm_w'][:, :c, :c]
    if n_seq > 1:
        w = jnp.einsum('ab,gts->gatbs', jnp.eye(n_seq, dtype=w.dtype), w).reshape(
            N_GM_HEADS, n_seq * c, n_seq * c)
    bias = jnp.repeat(p['gm_b'][:, :c].T, GM_HEAD_DIM, axis=1)
    return w.astype(MXU_DTYPE), jnp.tile(bias, (n_seq, 1))


def _prompt_layer(x, p, rel_bias, g_final, batch, seq, final):
    tm = 512
    (q, kt, vt, kit, kbf, vtbf, kibf, qi, _, wit, rec, gm) = _in_proj(x, p['g_mix'], p['w_in'], tm, seq)
    att = _prompt_attention(rel_bias, q, qi, wit, kibf, kbf, vtbf, batch, seq)
    zeros8 = jnp.zeros((batch, SUBLANES, D_REC), F32)
    rec_out, hlast, cstate = _rec_branch(rec, zeros8, zeros8, p['rec_cw'], p['rec_cb'], p['wa'], p['ba'],
                                         p['wx'], p['bx'], p['lam'], batch, seq, 256)
    c = min(CHUNK, seq)
    wmix, bias = _gmlp_mix_weights(p, c, 1)
    gm_out, _ = _gmlp(gm, wmix, bias, p['ln_g'], p['ln_b'], tm)
    x1, h2 = _out_proj(x, att, rec_out, gm_out, p['w_out'], p['g_ffn'], tm)
    hist16 = jnp.zeros((batch, HALO, 2 * D_FF), F32)
    x2, upst = _ffn_prompt(h2, x1, hist16, p['w_up'], p['ffn_cw'], p['ffn_cb'], p['w_down'], g_final,
                           batch, seq, tm, final)
    heads_last = lambda a: a.reshape(batch, N_KV_HEADS, HEAD_DIM, seq).transpose(0, 3, 1, 2)
    state = (heads_last(kt), heads_last(vt), kit.transpose(0, 2, 1), hlast[:, SUBLANES - 1, :],
             cstate[:, SUBLANES - (REC_CONV_W - 1):, :], upst[:, HALO - (FFN_CONV_W - 1):, :])
    return x2, state


def _sample_layer(x, p, rel_bias, g_final, l, batch, t_new, final, page_table, cache_k, cache_v, cache_kidx,
                  h0, rec_hist, ffn_hist):
    m = batch * t_new
    (q, k, v, ki, _, _, _, qi, kiwi, _, rec, gm) = _in_proj(x, p['g_mix'], p['w_in'], m)
    att = _sample_attention(page_table, rel_bias, q, qi, kiwi, k, v, cache_k, cache_v, cache_kidx,
                            l, batch, t_new)
    hist8 = _pad_rows_front(rec_hist, SUBLANES)
    h08 = jnp.broadcast_to(h0[:, None, :], (batch, SUBLANES, D_REC))
    rec_out, hlast, cstate = _rec_branch(rec, hist8, h08, p['rec_cw'], p['rec_cb'], p['wa'], p['ba'],
                                         p['wx'], p['bx'], p['lam'], batch, t_new, t_new)
    c = min(CHUNK, t_new)
    wmix, bias = _gmlp_mix_weights(p, c, m // c)
    gm_out, vn = _gmlp(gm, wmix, bias, p['ln_g'], p['ln_b'], m)
    x1, h2 = _out_proj(x, att, rec_out, gm_out, p['w_out'], p['g_ffn'], m)
    zrow = jnp.zeros((batch, t_new - 2, 2 * D_FF), F32)
    p2 = jnp.concatenate([ffn_hist, zrow], axis=1).reshape(m, 2 * D_FF)
    p1 = jnp.concatenate([ffn_hist[:, 1:], zrow, zrow[:, :1]], axis=1).reshape(m, 2 * D_FF)
    x2, up = _ffn_sample(h2, x1, p1, p2, p['w_up'], p['ffn_cw'], p['ffn_cb'], p['w_down'], g_final,
                         t_new, final)
    state = (k.reshape(batch, t_new, N_KV_HEADS, HEAD_DIM), v.reshape(batch, t_new, N_KV_HEADS, HEAD_DIM),
             ki.reshape(batch, t_new, IDX_DIM), hlast[:, SUBLANES - 1, :],
             cstate[:, SUBLANES - (REC_CONV_W - 1):, :],
             up.reshape(batch, t_new, 2 * D_FF)[:, t_new - (FFN_CONV_W - 1):, :],
             vn.reshape(batch, t_new, D_GM))
    return x2, state


def kernel(x_prompt, x_sample, cache_k, cache_v, cache_kidx, page_table, state_lru_h, state_conv_rec,
           state_conv_ffn, rel_bias, g_mix, w_in, rec_conv_w, rec_conv_b, lru_wa, lru_ba, lru_wx, lru_bx,
           lru_lam, gm_ln_g, gm_ln_b, gm_ws, gm_bs, w_out, g_ffn, w_up, ffn_conv_w, ffn_conv_b, w_down,
           g_final):
    batch, seq, _ = x_prompt.shape
    dec_batch, t_new, _ = x_sample.shape
    depth = w_in.shape[0]
    assert seq % 512 == 0 and t_new == SUBLANES and (dec_batch * t_new) % KEY_CHUNK == 0
    xp = x_prompt.reshape(batch * seq, D_MODEL)
    xs = x_sample.reshape(dec_batch * t_new, D_MODEL)
    gf = g_final.reshape(1, D_MODEL)
    p_states, s_states = [], []
    for l in range(depth):
        p = _layer_weights(l, g_mix, w_in, rec_conv_w, rec_conv_b, lru_wa, lru_ba, lru_wx, lru_bx, lru_lam,
                           gm_ln_g, gm_ln_b, gm_ws, gm_bs, w_out, g_ffn, w_up, ffn_conv_w, ffn_conv_b, w_down)
        final = l == depth - 1
        xp, st = _prompt_layer(xp, p, rel_bias, gf, batch, seq, final)
        p_states.append(st)
        xs, st = _sample_layer(xs, p, rel_bias, gf, l, dec_batch, t_new, final, page_table, cache_k, cache_v,
                               cache_kidx, state_lru_h[l], state_conv_rec[l], state_conv_ffn[l])
        s_states.append(st)
    stack = lambda states, i: jnp.stack([s[i] for s in states])
    return ((xp.reshape(batch, seq, D_MODEL), xs.reshape(dec_batch, t_new, D_MODEL))
            + tuple(stack(p_states, i) for i in range(6))
            + tuple(stack(s_states, i) for i in range(7)))
```

```python
import functools
import math

import jax
import jax.numpy as jnp
from jax import lax
from jax.experimental import pallas as pl
from jax.experimental.pallas import tpu as pltpu

F32 = jnp.float32
I32 = jnp.int32
MXU_DTYPE = jnp.bfloat16

D_MODEL = 1024
N_HEADS = 8
HEAD_DIM = 64
D_ATT = N_HEADS * HEAD_DIM
N_KV_HEADS = 2
GROUP = N_HEADS // N_KV_HEADS
D_KV = N_KV_HEADS * HEAD_DIM
N_IDX_HEADS = 4
IDX_DIM = 64
D_QI = N_IDX_HEADS * IDX_DIM
TOPK_MAX = 256
NUM_BUCKETS = 32
MAX_EXACT = NUM_BUCKETS // 2
MAX_DISTANCE = 128
D_REC = 256
N_REC_BLOCKS = 4
REC_CONV_W = 4
LRU_C = 8.0
D_GM = 256
N_GM_HEADS = 4
GM_HEAD_DIM = D_GM // N_GM_HEADS
CHUNK = 128
D_FF = 2816
FFN_CONV_W = 3
EPS = 1e-6
PAGE_SIZE = 128
Q_SCALE = HEAD_DIM ** -0.5

LANES = 128
SUBLANES = 8
VMEM_LIMIT_BYTES = 56 * 1024 * 1024

C_Q = 0
C_KV = C_Q + D_ATT
C_QI = C_KV + 2 * D_KV
C_REC = C_QI + D_QI
C_GM = C_REC + 2 * D_REC
C_KIWI = C_GM + 2 * D_GM
D_IN_PAD = C_KIWI + LANES

KEY_CHUNK = 256
Q_BLOCK = 128
INT_MIN = -2 ** 31
F32_MIN_NORMAL = 2.0 ** -126
PACKED_ROWS = 2 * SUBLANES
NEG_BIG = -2.0 ** 100
COUNT_ACCUMULATORS = 8

BUCKET_LO = tuple(
    b if b <= MAX_EXACT else math.ceil(MAX_EXACT * (MAX_DISTANCE / MAX_EXACT) ** ((b - MAX_EXACT) / (NUM_BUCKETS - MAX_EXACT)))
    for b in range(NUM_BUCKETS))


def _cparams(*sem):
    return pltpu.CompilerParams(dimension_semantics=sem, vmem_limit_bytes=VMEM_LIMIT_BYTES)


def _const_spec(shape):
    nd = len(shape)
    return pl.BlockSpec(shape, lambda *_: (0,) * nd, pipeline_mode=pl.Buffered(1))


def _rmsnorm(x, g):
    return x * lax.rsqrt(jnp.mean(x * x, axis=-1, keepdims=True) + EPS) * g


def _gelu(x):
    return 0.5 * x * (1.0 + jnp.tanh(math.sqrt(2.0 / math.pi) * (x + 0.044715 * (x * x * x))))


def _sigmoid(x):
    return 1.0 / (1.0 + jnp.exp(-x))


def _mm(a, b):
    return jnp.dot(a, b, preferred_element_type=F32)


def _mm_nt(a, b):
    return lax.dot_general(a, b, (((1,), (1,)), ((), ())), preferred_element_type=F32)


def _flush_subnormal(f):
    return jnp.where(jnp.abs(f) < F32_MIN_NORMAL, 0.0, f)


def _pattern_to_f32(c):
    return _flush_subnormal(pltpu.bitcast(jnp.where(c >= 0, c, c ^ 0x7FFFFFFF), F32))


def _tree_sum(xs):
    while len(xs) > 1:
        xs = [a + b for a, b in zip(xs[0::2], xs[1::2])] + ([xs[-1]] if len(xs) % 2 else [])
    return xs[0]


def _pair_loop(n_pairs, chunk_fn, init):
    def body(p, carry):
        return chunk_fn(2 * p + 1, chunk_fn(2 * p, carry))
    return lax.fori_loop(0, n_pairs, body, init)


def _count_above(ref, n_chunks, cand, strict):
    cb = jnp.broadcast_to(cand, (SUBLANES, LANES))

    def chunk(c, accs):
        accs = list(accs)
        x = ref[c]
        for j in range(KEY_CHUNK // SUBLANES):
            blk = x[j * SUBLANES:(j + 1) * SUBLANES]
            k = j % COUNT_ACCUMULATORS
            accs[k] = accs[k] + jnp.where((blk > cb) if strict else (blk >= cb), 1.0, 0.0)
        return tuple(accs)

    zeros = (jnp.zeros((SUBLANES, LANES), F32),) * COUNT_ACCUMULATORS
    accs = lax.fori_loop(0, n_chunks, chunk, zeros, unroll=True)
    return jnp.sum(_tree_sum(list(accs)), axis=0, keepdims=True)


def _kth_largest(count_ge, need, start):
    def step(i, t):
        cand = t + lax.shift_left(jnp.int32(1), 31 - i)
        return jnp.where(count_ge(_pattern_to_f32(cand)) >= need, cand, t)

    t = lax.fori_loop(0, 32, step, start)
    return jnp.where(t == INT_MIN, -jnp.inf, _pattern_to_f32(t))


def _in_proj_body(x_ref, g_ref, w_ref, q_ref, k_ref, v_ref, ki_ref, kbf_ref, vt_ref, kibf_ref,
                  qi_ref, kiwi_ref, wit_ref, rec_ref, gm_ref, *, transposed_state):
    n_chunks = kbf_ref.shape[0]
    h = _rmsnorm(x_ref[...], g_ref[...]).astype(MXU_DTYPE)

    def proj(lo, hi):
        return _mm(h, w_ref[:, lo:hi])

    q_ref[...] = (proj(C_Q, C_KV) * Q_SCALE).astype(MXU_DTYPE)
    kv = proj(C_KV, C_QI)
    k = kv[:, :D_KV]
    v = kv[:, D_KV:]
    vt = v.T
    kb = k.astype(MXU_DTYPE)
    qi_ref[...] = proj(C_QI, C_REC).astype(MXU_DTYPE)
    rec_ref[...] = proj(C_REC, C_GM)
    gm_ref[...] = proj(C_GM, C_KIWI)
    kiwi = proj(C_KIWI, D_IN_PAD)
    ki = kiwi[:, :IDX_DIM]
    kiwi_ref[...] = kiwi
    kib = ki.astype(MXU_DTYPE)
    kiwi_t = kiwi.T
    wit_ref[...] = kiwi_t[IDX_DIM:IDX_DIM + SUBLANES, :]
    if transposed_state:
        k_ref[0] = k.T
        v_ref[0] = vt
        ki_ref[0] = kiwi_t[:IDX_DIM, :]
    else:
        k_ref[...] = k
        v_ref[...] = v
        ki_ref[...] = ki
    for c in range(n_chunks):
        rows = slice(c * KEY_CHUNK, (c + 1) * KEY_CHUNK)
        kbf_ref[c] = kb[rows]
        kibf_ref[c] = kib[rows]
        vt_ref[c] = vt[:, rows].astype(MXU_DTYPE)


def _in_proj(x, g, w, tm, seq=None):
    m = x.shape[0]
    nc = tm // KEY_CHUNK
    row = lambda n: pl.BlockSpec((tm, n), lambda i: (i, 0))
    chunked = lambda a, b: pl.BlockSpec((nc, a, b), lambda i: (i, 0, 0))
    if seq is None:
        state_shape = lambda n: jax.ShapeDtypeStruct((m, n), F32)
        state_spec = row
    else:
        tps = seq // tm
        state_shape = lambda n: jax.ShapeDtypeStruct((m // seq, n, seq), F32)
        state_spec = lambda n: pl.BlockSpec((1, n, tm), lambda i: (i // tps, 0, i % tps))
    out_shape = (
        jax.ShapeDtypeStruct((m, D_ATT), MXU_DTYPE),
        state_shape(D_KV),
        state_shape(D_KV),
        state_shape(IDX_DIM),
        jax.ShapeDtypeStruct((m // KEY_CHUNK, KEY_CHUNK, D_KV), MXU_DTYPE),
        jax.ShapeDtypeStruct((m // KEY_CHUNK, D_KV, KEY_CHUNK), MXU_DTYPE),
        jax.ShapeDtypeStruct((m // KEY_CHUNK, KEY_CHUNK, IDX_DIM), MXU_DTYPE),
        jax.ShapeDtypeStruct((m, D_QI), MXU_DTYPE),
        jax.ShapeDtypeStruct((m, LANES), F32),
        jax.ShapeDtypeStruct((SUBLANES, m), F32),
        jax.ShapeDtypeStruct((m, 2 * D_REC), F32),
        jax.ShapeDtypeStruct((m, 2 * D_GM), F32),
    )
    out_specs = (
        row(D_ATT), state_spec(D_KV), state_spec(D_KV), state_spec(IDX_DIM),
        chunked(KEY_CHUNK, D_KV), chunked(D_KV, KEY_CHUNK), chunked(KEY_CHUNK, IDX_DIM),
        row(D_QI), row(LANES), pl.BlockSpec((SUBLANES, tm), lambda i: (0, i)),
        row(2 * D_REC), row(2 * D_GM),
    )
    return pl.pallas_call(
        functools.partial(_in_proj_body, transposed_state=seq is not None),
        grid=(m // tm,),
        in_specs=[row(D_MODEL), _const_spec((1, D_MODEL)), _const_spec((D_MODEL, D_IN_PAD))],
        out_specs=out_specs,
        out_shape=out_shape,
        compiler_params=_cparams("arbitrary"),
        name="in_proj",
    )(x, g, w)


def _bias_table(rb_ref, head, delta, shape):
    row = lax.broadcasted_iota(I32, shape, 0)
    lane = lax.broadcasted_iota(I32, shape, 1)
    d = delta + lane - row
    far = rb_ref[NUM_BUCKETS - 1, head]
    val = jnp.full(shape, rb_ref[0, head] - far, F32)
    for b in range(1, NUM_BUCKETS - 1):
        val = jnp.where(d >= BUCKET_LO[b], rb_ref[b, head] - far, val)
    return jnp.where(d >= BUCKET_LO[NUM_BUCKETS - 1], 0.0, val)


def _prompt_attn_body(rb_ref, q_ref, qi_ref, wit_ref, ki_ref, k_ref, vt_ref, att_ref,
                      s_s, mb_s, tab_s, tri_s, thr_s, need_s, surplus_s, m_s, acc_s, qpad_s, *, topk):
    b = pl.program_id(0)
    i = pl.program_id(1)
    n_pairs = i // 4 + 1
    ck = (KEY_CHUNK, Q_BLOCK)

    @pl.when((b == 0) & (i == 0))
    def _init_tables():
        r = lax.broadcasted_iota(I32, (KEY_CHUNK, KEY_CHUNK), 0)
        c = lax.broadcasted_iota(I32, (KEY_CHUNK, KEY_CHUNK), 1)
        tri_s[...] = jnp.where(c <= r, 1.0, 0.0).astype(MXU_DTYPE)

        def per_head(h, carry):
            for ti in range(4):
                tab_s[ti, h] = _bias_table(rb_ref, h, ti * Q_BLOCK, ck)
            return carry

        lax.fori_loop(0, N_HEADS, per_head, 0)

    row = lax.broadcasted_iota(I32, ck, 0)
    lane = lax.broadcasted_iota(I32, ck, 1)
    q_pos = i * Q_BLOCK + lane

    qi = qi_ref[...]
    qis = jnp.concatenate([qi[:, h * IDX_DIM:(h + 1) * IDX_DIM] for h in range(N_IDX_HEADS)], axis=0)
    wit = wit_ref[...]
    w_row = jnp.concatenate([wit[h:h + 1, :] for h in range(N_IDX_HEADS)], axis=1)

    def score_chunk(c, carry):
        s4 = jnp.maximum(_mm_nt(ki_ref[c], qis), 0.0) * w_row
        s = _tree_sum([s4[:, h * Q_BLOCK:(h + 1) * Q_BLOCK] for h in range(N_IDX_HEADS)])
        valid = (c * KEY_CHUNK + row) <= q_pos
        s_s[c] = jnp.where(valid, s, -jnp.inf)
        return carry

    _pair_loop(n_pairs, score_chunk, 0)

    def search(n_chunks):
        t = _kth_largest(lambda cand: _count_above(s_s, n_chunks, cand, False), topk,
                         jnp.full((1, Q_BLOCK), INT_MIN, I32))
        thr_s[0:1, :] = t
        need_s[0:1, :] = topk - _count_above(s_s, n_chunks, t, True)
        surplus_s[0] = jnp.max(_count_above(s_s, n_chunks, t, False) - topk).astype(I32)

    n_chunks_causal = (i + 2) // 2
    for n in range(1, s_s.shape[0] + 1):
        pl.when(n_chunks_causal == n)(functools.partial(search, n))
    t = thr_s[0:1, :]
    need = need_s[0:1, :]

    tri = tri_s[...]

    def mask_chunk(c, carry):
        sc = s_s[c]
        tie = (sc == t) & (sc > -jnp.inf)
        tie_f = jnp.where(tie, 1.0, 0.0)
        incl = _mm(tri, tie_f.astype(MXU_DTYPE))
        rank = carry + incl - tie_f
        sel = (sc > t) | (tie & (rank < need))
        mb_s[c] = jnp.where(sel, 0.0, -jnp.inf).astype(mb_s.dtype)
        return carry + incl[KEY_CHUNK - 1:KEY_CHUNK, :]

    def mask_chunk_all_ties(c, carry):
        sc = s_s[c]
        mb_s[c] = jnp.where((sc >= t) & (sc > -jnp.inf), 0.0, -jnp.inf).astype(mb_s.dtype)
        return carry

    ties_fit = surplus_s[0] <= 0

    @pl.when(ties_fit)
    def _mask_without_ranking():
        _pair_loop(n_pairs, mask_chunk_all_ties, 0)

    @pl.when(jnp.logical_not(ties_fit))
    def _mask_with_ranking():
        _pair_loop(n_pairs, mask_chunk, jnp.zeros((1, Q_BLOCK), F32))

    q = q_ref[...]
    zeros = jnp.zeros((Q_BLOCK, HEAD_DIM), MXU_DTYPE)
    for h in range(N_HEADS):
        qh = q[:, h * HEAD_DIM:(h + 1) * HEAD_DIM]
        parts = [qh, zeros] if h < GROUP else [zeros, qh]
        qpad_s[h * Q_BLOCK:(h + 1) * Q_BLOCK, :] = jnp.concatenate(parts, axis=1)
    m_s[...] = jnp.full(m_s.shape, NEG_BIG, F32)
    acc_s[...] = jnp.zeros(acc_s.shape, F32)
    ones_rows = jnp.ones((PACKED_ROWS, 2 * KEY_CHUNK), MXU_DTYPE)

    def attend_pair(p, near):
        c0, c1 = 2 * p, 2 * p + 1
        keys = jnp.concatenate([k_ref[c0], k_ref[c1]], axis=0)
        logits = _mm_nt(keys, qpad_s[...])
        mb = jnp.concatenate([mb_s[c0], mb_s[c1]], axis=0)
        if near:
            t0 = jnp.clip(i - 2 * c0, 0, 3)
            t1 = jnp.clip(i - 2 * c1, 0, 3)
        for g in range(N_KV_HEADS):
            ps, alphas = [], []
            for hh in range(GROUP):
                h = g * GROUP + hh
                cols = slice(h * Q_BLOCK, (h + 1) * Q_BLOCK)
                lt = logits[:, cols]
                if near:
                    lt = lt + jnp.concatenate([tab_s[t0, h], tab_s[t1, h]], axis=0)
                lt = lt.astype(MXU_DTYPE) + mb
                m_old = m_s[0:1, cols]
                m_new = jnp.maximum(m_old, jnp.max(lt, axis=0, keepdims=True).astype(F32))
                alphas.append(jnp.exp(m_old - m_new))
                ps.append(jnp.exp(lt - m_new.astype(MXU_DTYPE)))
                m_s[0:1, cols] = m_new
            pg = jnp.concatenate(ps, axis=1)
            ag = jnp.concatenate(alphas, axis=1)
            rows = slice(g * HEAD_DIM, (g + 1) * HEAD_DIM)
            vt = jnp.concatenate([jnp.concatenate([vt_ref[c0, rows, :], vt_ref[c1, rows, :]], axis=1), ones_rows],
                                 axis=0)
            acc_s[g] = acc_s[g] * ag + _mm(vt, pg)

    def far_pair(p, carry):
        attend_pair(2 * p, False)
        attend_pair(2 * p + 1, False)
        return carry

    n_far = jnp.maximum(n_pairs - 2, 0)
    lax.fori_loop(0, n_far // 2, far_pair, 0)

    @pl.when(n_far % 2 == 1)
    def _odd_far_pair():
        attend_pair(n_far - 1, False)

    @pl.when(n_pairs >= 2)
    def _near_pairs():
        attend_pair(n_pairs - 2, True)
        attend_pair(n_pairs - 1, True)

    @pl.when(n_pairs == 1)
    def _only_pair():
        attend_pair(0, True)

    for h in range(N_HEADS):
        g, hh = divmod(h, GROUP)
        cols = slice(hh * Q_BLOCK, (hh + 1) * Q_BLOCK)
        o = acc_s[g][:HEAD_DIM, cols] / acc_s[g][HEAD_DIM:HEAD_DIM + 1, cols]
        att_ref[:, h * HEAD_DIM:(h + 1) * HEAD_DIM] = o.T.astype(att_ref.dtype)


def _prompt_attention(rel_bias, q, qi, wit, kibf, kbf, vtbf, batch, seq):
    nq = seq // Q_BLOCK
    nc = seq // KEY_CHUNK
    topk = float(min(TOPK_MAX, seq // 4))
    qrow = lambda n: pl.BlockSpec((Q_BLOCK, n), lambda b, i: (b * nq + i, 0))
    per_batch = lambda a, c: pl.BlockSpec((nc, a, c), lambda b, i: (b, 0, 0))
    return pl.pallas_call(
        functools.partial(_prompt_attn_body, topk=topk),
        grid=(batch, nq),
        in_specs=[
            pl.BlockSpec(memory_space=pltpu.SMEM),
            qrow(D_ATT), qrow(D_QI),
            pl.BlockSpec((SUBLANES, Q_BLOCK), lambda b, i: (0, b * nq + i)),
            per_batch(KEY_CHUNK, IDX_DIM), per_batch(KEY_CHUNK, D_KV), per_batch(D_KV, KEY_CHUNK),
        ],
        out_specs=qrow(D_ATT),
        out_shape=jax.ShapeDtypeStruct((batch * seq, D_ATT), MXU_DTYPE),
        scratch_shapes=[
            pltpu.VMEM((nc, KEY_CHUNK, Q_BLOCK), F32),
            pltpu.VMEM((nc, KEY_CHUNK, Q_BLOCK), MXU_DTYPE),
            pltpu.VMEM((4, N_HEADS, KEY_CHUNK, Q_BLOCK), F32),
            pltpu.VMEM((KEY_CHUNK, KEY_CHUNK), MXU_DTYPE),
            pltpu.VMEM((SUBLANES, Q_BLOCK), F32),
            pltpu.VMEM((SUBLANES, Q_BLOCK), F32),
            pltpu.SMEM((1,), I32),
            pltpu.VMEM((SUBLANES, N_HEADS * Q_BLOCK), F32),
            pltpu.VMEM((N_KV_HEADS, HEAD_DIM + PACKED_ROWS, GROUP * Q_BLOCK), F32),
            pltpu.VMEM((N_HEADS * Q_BLOCK, D_KV), MXU_DTYPE),
        ],
        compiler_params=_cparams("arbitrary", "arbitrary"),
        name="prompt_attn",
    )(rel_bias, q, qi, wit, kibf, kbf, vtbf)


def _sample_attn_body(pt_ref, rb_ref, q_ref, qi_ref, kiwi_ref, knew_ref, vnew_ref,
                      ck_hbm, cv_hbm, cki_hbm, att_ref,
                      ktbuf, vtbuf, kitbuf, sem, *, topk, n_pages, layer, t_new):
    b = pl.program_id(0)
    past = n_pages * PAGE_SIZE
    slot = b % 2

    def page_copies(seq, buf, p):
        phys = pt_ref[seq, p]
        cols = pl.ds(pl.multiple_of(p * PAGE_SIZE, PAGE_SIZE), PAGE_SIZE)
        return (pltpu.make_async_copy(ck_hbm.at[layer, phys], ktbuf.at[buf, :, cols], sem.at[0, buf]),
                pltpu.make_async_copy(cv_hbm.at[layer, phys], vtbuf.at[buf, :, cols], sem.at[1, buf]),
                pltpu.make_async_copy(cki_hbm.at[layer, phys], kitbuf.at[buf, :, cols], sem.at[2, buf]))

    def start_pages(seq, buf):
        def body(p, carry):
            for cp in page_copies(seq, buf, p):
                cp.start()
            return carry
        lax.fori_loop(0, n_pages, body, 0)

    def wait_pages(seq, buf):
        def body(p, carry):
            for cp in page_copies(seq, buf, p):
                cp.wait()
            return carry
        lax.fori_loop(0, n_pages, body, 0)

    @pl.when(b == 0)
    def _first_fetch():
        start_pages(0, 0)

    @pl.when(b + 1 < pl.num_programs(0))
    def _prefetch_next():
        start_pages(b + 1, 1 - slot)

    qi = qi_ref[...]
    qis = jnp.concatenate([qi[:, h * IDX_DIM:(h + 1) * IDX_DIM] for h in range(N_IDX_HEADS)], axis=0)
    kiwi = kiwi_ref[...]
    w_col = jnp.concatenate([kiwi[:, IDX_DIM + h:IDX_DIM + h + 1] for h in range(N_IDX_HEADS)], axis=0)
    q = q_ref[...]
    zeros = jnp.zeros((t_new, HEAD_DIM), MXU_DTYPE)
    qpad = jnp.concatenate(
        [jnp.concatenate([q[:, h * HEAD_DIM:(h + 1) * HEAD_DIM], zeros] if h < GROUP else
                         [zeros, q[:, h * HEAD_DIM:(h + 1) * HEAD_DIM]], axis=1)
         for h in range(N_HEADS)], axis=0)

    wait_pages(b, slot)

    def idx_score(qk):
        s4 = jnp.maximum(qk, 0.0) * w_col
        s = s4[0:t_new]
        for h in range(1, N_IDX_HEADS):
            s = s + s4[h * t_new:(h + 1) * t_new]
        return s

    s_p = idx_score(_mm(qis, kitbuf[slot].astype(MXU_DTYPE)))
    ki_new = kiwi[:, :IDX_DIM].astype(MXU_DTYPE)
    ki_new = jnp.concatenate([ki_new, jnp.zeros((LANES - t_new, IDX_DIM), MXU_DTYPE)], axis=0)
    rown = lax.broadcasted_iota(I32, (t_new, LANES), 0)
    lanen = lax.broadcasted_iota(I32, (t_new, LANES), 1)
    valid_n = lanen <= rown
    s_n = jnp.where(valid_n, idx_score(_mm_nt(qis, ki_new)), -jnp.inf)

    def count(cand, strict):
        cmp = (lambda x: x > cand) if strict else (lambda x: x >= cand)
        return (jnp.sum(jnp.where(cmp(s_p), 1.0, 0.0), axis=1, keepdims=True)
                + jnp.sum(jnp.where(cmp(s_n), 1.0, 0.0), axis=1, keepdims=True))

    t = _kth_largest(lambda cand: count(cand, False), topk, jnp.full((t_new, 1), INT_MIN, I32))
    need = topk - count(t, True)

    r = lax.broadcasted_iota(I32, (KEY_CHUNK, KEY_CHUNK), 0)
    c = lax.broadcasted_iota(I32, (KEY_CHUNK, KEY_CHUNK), 1)
    tri = jnp.where(r <= c, 1.0, 0.0).astype(MXU_DTYPE)
    tie_p = s_p == t
    carry = jnp.zeros((t_new, 1), F32)
    mask_parts = []
    for ch in range(past // KEY_CHUNK):
        cols = slice(ch * KEY_CHUNK, (ch + 1) * KEY_CHUNK)
        tie_c = tie_p[:, cols]
        tie_f = jnp.where(tie_c, 1.0, 0.0)
        incl = _mm(tie_f.astype(MXU_DTYPE), tri)
        sel = (s_p[:, cols] > t) | (tie_c & ((carry + incl - tie_f) < need))
        mask_parts.append(jnp.where(sel, 0.0, -jnp.inf))
        carry = carry + incl[:, KEY_CHUNK - 1:KEY_CHUNK]
    mb_p = jnp.concatenate(mask_parts, axis=1)
    tie_n = (s_n == t) & valid_n
    tie_nf = jnp.where(tie_n, 1.0, 0.0)
    incl_n = _mm(tie_nf.astype(MXU_DTYPE), tri[:LANES, :LANES])
    sel_n = (s_n > t) | (tie_n & ((carry + incl_n - tie_nf) < need))
    mb_n = jnp.where(sel_n & valid_n, 0.0, -jnp.inf)

    lg_p = _mm(qpad, ktbuf[slot].astype(MXU_DTYPE))
    k_new = jnp.concatenate([knew_ref[...].astype(MXU_DTYPE),
                             jnp.zeros((LANES - t_new, D_KV), MXU_DTYPE)], axis=0)
    lg_n = _mm_nt(qpad, k_new)
    near = slice(past - LANES, past)
    lp_rows, ln_rows = [], []
    for h in range(N_HEADS):
        rows = slice(h * t_new, (h + 1) * t_new)
        far = rb_ref[NUM_BUCKETS - 1, h]

        def bias(d):
            val = jnp.full(d.shape, rb_ref[0, h] - far, F32)
            for bk in range(1, NUM_BUCKETS - 1):
                val = jnp.where(d >= BUCKET_LO[bk], rb_ref[bk, h] - far, val)
            return jnp.where(d >= BUCKET_LO[NUM_BUCKETS - 1], 0.0, val)

        lp = lg_p[rows] + mb_p
        lp_near = lp[:, near] + bias(rown + (LANES - lanen))
        lp_rows.append(jnp.concatenate([lp[:, :past - LANES], lp_near], axis=1))
        ln_rows.append(lg_n[rows] + mb_n + bias(rown - lanen))
    lp_all = jnp.concatenate(lp_rows, axis=0)
    ln_all = jnp.concatenate(ln_rows, axis=0)
    m = jnp.maximum(jnp.max(lp_all, axis=1, keepdims=True), jnp.max(ln_all, axis=1, keepdims=True))
    p_p = jnp.exp(lp_all - m)
    p_n = jnp.exp(ln_all - m)
    denom = jnp.sum(p_p, axis=1, keepdims=True) + jnp.sum(p_n, axis=1, keepdims=True)
    v_new = jnp.concatenate([vnew_ref[...].astype(MXU_DTYPE),
                             jnp.zeros((LANES - t_new, D_KV), MXU_DTYPE)], axis=0)
    o = (_mm_nt(p_p.astype(MXU_DTYPE), vtbuf[slot].astype(MXU_DTYPE))
         + _mm(p_n.astype(MXU_DTYPE), v_new)) / denom
    for h in range(N_HEADS):
        g = h // GROUP
        att_ref[:, h * HEAD_DIM:(h + 1) * HEAD_DIM] = (
            o[h * t_new:(h + 1) * t_new, g * HEAD_DIM:(g + 1) * HEAD_DIM].astype(att_ref.dtype))


def _sample_attention(page_table, rel_bias, q, qi, kiwi, k_new, v_new, cache_k, cache_v, cache_kidx,
                      layer, batch, t_new):
    n_pages = page_table.shape[1]
    past = n_pages * PAGE_SIZE
    topk = float(min(TOPK_MAX, (past + t_new) // 4))
    depth, n_pool = cache_k.shape[:2]
    ck = cache_k.transpose(0, 1, 3, 4, 2).reshape(depth, n_pool, D_KV, PAGE_SIZE)
    cv = cache_v.transpose(0, 1, 3, 4, 2).reshape(depth, n_pool, D_KV, PAGE_SIZE)
    cki = cache_kidx.transpose(0, 1, 3, 2)
    qrow = lambda n: pl.BlockSpec((t_new, n), lambda b, pt: (b, 0))
    hbm = pl.BlockSpec(memory_space=pl.ANY)
    grid_spec = pltpu.PrefetchScalarGridSpec(
        num_scalar_prefetch=1,
        grid=(batch,),
        in_specs=[pl.BlockSpec(memory_space=pltpu.SMEM),
                  qrow(D_ATT), qrow(D_QI), qrow(LANES), qrow(D_KV), qrow(D_KV), hbm, hbm, hbm],
        out_specs=qrow(D_ATT),
        scratch_shapes=[
            pltpu.VMEM((2, D_KV, past), F32),
            pltpu.VMEM((2, D_KV, past), F32),
            pltpu.VMEM((2, IDX_DIM, past), F32),
            pltpu.SemaphoreType.DMA((3, 2)),
        ],
    )
    return pl.pallas_call(
        functools.partial(_sample_attn_body, topk=topk, n_pages=n_pages, layer=layer, t_new=t_new),
        grid_spec=grid_spec,
        out_shape=jax.ShapeDtypeStruct((batch * t_new, D_ATT), MXU_DTYPE),
        compiler_params=_cparams("arbitrary"),
        name="sample_attn",
    )(page_table, rel_bias, q, qi, kiwi, k_new, v_new, ck, cv, cki)


def _rec_body(rec_ref, hist_ref, h0_ref, cw_ref, cb_ref, wa_ref, ba_ref, wx_ref, bx_ref, lam_ref,
              out_ref, hlast_ref, cstate_ref, hcar, xprev):
    tt = rec_ref.shape[0]

    @pl.when(pl.program_id(1) == 0)
    def _load_state():
        hcar[...] = h0_ref[0]
        xprev[...] = hist_ref[0]

    rx = rec_ref[:, :D_REC]
    rg = rec_ref[:, D_REC:]
    ext = jnp.concatenate([xprev[...], rx], axis=0)
    xc = cb_ref[...]
    for j in range(REC_CONV_W - 1):
        xc = xc + pltpu.roll(ext, REC_CONV_W - 1 - j, 0)[SUBLANES:] * cw_ref[j:j + 1, :]
    xc = xc + rx * cw_ref[REC_CONV_W - 1:REC_CONV_W, :]
    xcb = xc.astype(MXU_DTYPE)
    r = _sigmoid(_mm(xcb, wa_ref[...]) + ba_ref[...])
    gi = _sigmoid(_mm(xcb, wx_ref[...]) + bx_ref[...])
    nl = -lam_ref[...]
    softplus = jnp.maximum(nl, 0.0) + jnp.log1p(jnp.exp(-jnp.abs(nl)))
    log_a = -LRU_C * r * softplus
    a = jnp.exp(log_a)
    u = jnp.sqrt(-jnp.tanh(log_a) * (a * a + 1.0)) * (gi * xc)
    row = lax.broadcasted_iota(I32, (tt, D_REC), 0)
    s = 1
    while s < tt:
        keep = row >= s
        u = jnp.where(keep, u + a * pltpu.roll(u, s, 0), u)
        a = jnp.where(keep, a * pltpu.roll(a, s, 0), a)
        s *= 2
    hs = u + a * hcar[0:1, :]
    hcar[...] = jnp.broadcast_to(hs[tt - 1:tt, :], hcar.shape)
    xprev[...] = rx[tt - SUBLANES:, :]
    out_ref[...] = (_gelu(rg) * hs).astype(out_ref.dtype)
    hlast_ref[0] = hs[tt - SUBLANES:, :]
    cstate_ref[0] = rx[tt - SUBLANES:, :]


def _rec_branch(rec, hist8, h08, cw, cb, wa, ba, wx, bx, lam, batch, t_len, tt):
    nt = t_len // tt
    state = pl.BlockSpec((1, SUBLANES, D_REC), lambda b, t: (b, 0, 0))
    vec = _const_spec((1, D_REC))
    return pl.pallas_call(
        _rec_body,
        grid=(batch, nt),
        in_specs=[pl.BlockSpec((tt, 2 * D_REC), lambda b, t: (b * nt + t, 0)), state, state,
                  _const_spec((REC_CONV_W, D_REC)), vec, _const_spec((D_REC, D_REC)), vec,
                  _const_spec((D_REC, D_REC)), vec, vec],
        out_specs=(pl.BlockSpec((tt, D_REC), lambda b, t: (b * nt + t, 0)), state, state),
        out_shape=(jax.ShapeDtypeStruct((batch * t_len, D_REC), MXU_DTYPE),
                   jax.ShapeDtypeStruct((batch, SUBLANES, D_REC), F32),
                   jax.ShapeDtypeStruct((batch, SUBLANES, D_REC), F32)),
        scratch_shapes=[pltpu.VMEM((SUBLANES, D_REC), F32), pltpu.VMEM((SUBLANES, D_REC), F32)],
        compiler_params=_cparams("arbitrary", "arbitrary"),
        name="rec_branch",
    )(rec, hist8, h08, cw, cb, wa, ba, wx, bx, lam)


def _gmlp_body(gm_ref, wmix_ref, bias_ref, lng_ref, lnb_ref, out_ref, vn_ref):
    rows = wmix_ref.shape[1]
    n_sub = gm_ref.shape[0] // rows
    head_of_lane = lax.broadcasted_iota(I32, (rows, D_GM), 1) // GM_HEAD_DIM
    for sb in range(n_sub):
        sl = slice(sb * rows, (sb + 1) * rows)
        u = _gelu(gm_ref[sl, :D_GM])
        gv = _gelu(gm_ref[sl, D_GM:])
        xc = gv - jnp.mean(gv, axis=-1, keepdims=True)
        var = jnp.mean(xc * xc, axis=-1, keepdims=True)
        vn = xc * lax.rsqrt(var + EPS) * lng_ref[...] + lnb_ref[...]
        vn_ref[sl, :] = vn
        vnb = vn.astype(MXU_DTYPE)
        mix = jnp.zeros((rows, D_GM), F32)
        for g in range(N_GM_HEADS):
            mix = jnp.where(head_of_lane == g, _mm(wmix_ref[g], vnb), mix)
        out_ref[sl, :] = (u * (mix + bias_ref[...])).astype(out_ref.dtype)


def _gmlp(gm, wmix, bias, lng, lnb, tm):
    m = gm.shape[0]
    rows = wmix.shape[1]
    return pl.pallas_call(
        _gmlp_body,
        grid=(m // tm,),
        in_specs=[pl.BlockSpec((tm, 2 * D_GM), lambda i: (i, 0)),
                  _const_spec((N_GM_HEADS, rows, rows)), _const_spec((rows, D_GM)),
                  _const_spec((1, D_GM)), _const_spec((1, D_GM))],
        out_specs=(pl.BlockSpec((tm, D_GM), lambda i: (i, 0)), pl.BlockSpec((tm, D_GM), lambda i: (i, 0))),
        out_shape=(jax.ShapeDtypeStruct((m, D_GM), MXU_DTYPE), jax.ShapeDtypeStruct((m, D_GM), F32)),
        compiler_params=_cparams("arbitrary"),
        name="gmlp",
    )(gm, wmix, bias, lng, lnb)


def _out_proj_body(x_ref, att_ref, rec_ref, gm_ref, w_ref, g_ref, x1_ref, h2_ref):
    y = (_mm(att_ref[...], w_ref[0:D_ATT, :])
         + _mm(rec_ref[...], w_ref[D_ATT:D_ATT + D_REC, :])
         + _mm(gm_ref[...], w_ref[D_ATT + D_REC:, :]))
    x1 = x_ref[...] + y
    x1_ref[...] = x1
    h2_ref[...] = _rmsnorm(x1, g_ref[...]).astype(h2_ref.dtype)


def _out_proj(x, att, rec, gm, w, g, tm):
    m = x.shape[0]
    row = lambda n: pl.BlockSpec((tm, n), lambda i: (i, 0))
    return pl.pallas_call(
        _out_proj_body,
        grid=(m // tm,),
        in_specs=[row(D_MODEL), row(D_ATT), row(D_REC), row(D_GM),
                  _const_spec((D_MODEL, D_MODEL)), _const_spec((1, D_MODEL))],
        out_specs=(row(D_MODEL), row(D_MODEL)),
        out_shape=(jax.ShapeDtypeStruct((m, D_MODEL), F32), jax.ShapeDtypeStruct((m, D_MODEL), MXU_DTYPE)),
        compiler_params=_cparams("arbitrary"),
        name="out_proj",
    )(x, att, rec, gm, w, g)


FF_TILE = 256
HALO = 16


def _ffn_finish(x1, acc, gf_ref, out_ref, final):
    x2 = x1 + acc
    out_ref[...] = _rmsnorm(x2, gf_ref[...]) if final else x2


def _ffn_prompt_body(h_ref, halo_ref, x1_ref, hist_ref, wup_ref, cw_ref, cb_ref, wdn_ref, gf_ref,
                     out_ref, upst_ref, act_s, *, tiles_per_seq, final):
    tm = h_ref.shape[0]
    first = (pl.program_id(0) % tiles_per_seq) == 0
    h_ext = jnp.concatenate([halo_ref[...], h_ref[...]], axis=0)
    for j in range(D_FF // FF_TILE):
        halves = []
        for base in (0, D_FF):
            cols = slice(base + j * FF_TILE, base + (j + 1) * FF_TILE)
            up = _mm(h_ext, wup_ref[:, cols])
            up_m = up[HALO:]
            upst_ref[0, :, cols] = up_m[tm - HALO:, :]
            ext = jnp.concatenate([jnp.where(first, hist_ref[0, :, cols], up[:HALO]), up_m], axis=0)
            uc = cb_ref[:, cols]
            for jj in range(FFN_CONV_W - 1):
                uc = uc + pltpu.roll(ext, FFN_CONV_W - 1 - jj, 0)[HALO:] * cw_ref[jj:jj + 1, cols]
            halves.append(uc + up_m * cw_ref[FFN_CONV_W - 1:FFN_CONV_W, cols])
        act_s[:, j * FF_TILE:(j + 1) * FF_TILE] = (_gelu(halves[0]) * halves[1]).astype(MXU_DTYPE)
    _ffn_finish(x1_ref[...], _mm(act_s[...], wdn_ref[...]), gf_ref, out_ref, final)


def _ffn_prompt(h2, x1, hist16, wup, cw, cb, wdn, gf, batch, seq, tm, final):
    m = h2.shape[0]
    tps = seq // tm
    row = lambda n: pl.BlockSpec((tm, n), lambda i: (i, 0))
    state = pl.BlockSpec((1, HALO, 2 * D_FF), lambda i: (i // tps, 0, 0))
    return pl.pallas_call(
        functools.partial(_ffn_prompt_body, tiles_per_seq=tps, final=final),
        grid=(m // tm,),
        in_specs=[row(D_MODEL),
                  pl.BlockSpec((HALO, D_MODEL), lambda i: (jnp.maximum(i * (tm // HALO) - 1, 0), 0)),
                  row(D_MODEL), state,
                  _const_spec((D_MODEL, 2 * D_FF)), _const_spec((FFN_CONV_W, 2 * D_FF)),
                  _const_spec((1, 2 * D_FF)), _const_spec((D_FF, D_MODEL)), _const_spec((1, D_MODEL))],
        out_specs=(row(D_MODEL), state),
        out_shape=(jax.ShapeDtypeStruct((m, D_MODEL), F32),
                   jax.ShapeDtypeStruct((batch, HALO, 2 * D_FF), F32)),
        scratch_shapes=[pltpu.VMEM((tm, D_FF), MXU_DTYPE)],
        compiler_params=_cparams("arbitrary"),
        name="ffn_prompt",
    )(h2, h2, x1, hist16, wup, cw, cb, wdn, gf)


def _ffn_sample_body(h_ref, x1_ref, p1_ref, p2_ref, wup_ref, cw_ref, cb_ref, wdn_ref, gf_ref,
                     out_ref, up_ref, *, t_new, final):
    tm = h_ref.shape[0]
    hm = h_ref[...]
    pos = lax.broadcasted_iota(I32, (tm, FF_TILE), 0) % t_new
    acc = jnp.zeros((tm, D_MODEL), F32)
    for j in range(D_FF // FF_TILE):
        halves = []
        for base in (0, D_FF):
            cols = slice(base + j * FF_TILE, base + (j + 1) * FF_TILE)
            up = _mm(hm, wup_ref[:, cols])
            up_ref[:, cols] = up
            tap2 = jnp.where(pos < 2, p2_ref[:, cols], pltpu.roll(up, 2, 0))
            tap1 = jnp.where(pos < 1, p1_ref[:, cols], pltpu.roll(up, 1, 0))
            halves.append(cb_ref[:, cols] + tap2 * cw_ref[0:1, cols] + tap1 * cw_ref[1:2, cols]
                          + up * cw_ref[2:3, cols])
        act = (_gelu(halves[0]) * halves[1]).astype(MXU_DTYPE)
        acc = acc + _mm(act, wdn_ref[j * FF_TILE:(j + 1) * FF_TILE, :])
    _ffn_finish(x1_ref[...], acc, gf_ref, out_ref, final)


def _ffn_sample(h2, x1, p1, p2, wup, cw, cb, wdn, gf, t_new, final):
    m = h2.shape[0]
    full = lambda n: _const_spec((m, n))
    return pl.pallas_call(
        functools.partial(_ffn_sample_body, t_new=t_new, final=final),
        grid=(1,),
        in_specs=[full(D_MODEL), full(D_MODEL), full(2 * D_FF), full(2 * D_FF),
                  _const_spec((D_MODEL, 2 * D_FF)), _const_spec((FFN_CONV_W, 2 * D_FF)),
                  _const_spec((1, 2 * D_FF)), _const_spec((D_FF, D_MODEL)), _const_spec((1, D_MODEL))],
        out_specs=(full(D_MODEL), full(2 * D_FF)),
        out_shape=(jax.ShapeDtypeStruct((m, D_MODEL), F32), jax.ShapeDtypeStruct((m, 2 * D_FF), F32)),
        compiler_params=_cparams("arbitrary"),
        name="ffn_sample",
    )(h2, x1, p1, p2, wup, cw, cb, wdn, gf)


def _block_diag(w):
    n, blk, _ = w.shape
    out = jnp.zeros((n * blk, n * blk), w.dtype)
    for i in range(n):
        out = out.at[i * blk:(i + 1) * blk, i * blk:(i + 1) * blk].set(w[i])
    return out


def _pad_rows_front(a, rows):
    pad = rows - a.shape[1]
    return jnp.pad(a, ((0, 0), (pad, 0), (0, 0)))


def _layer_weights(l, g_mix, w_in, rec_conv_w, rec_conv_b, lru_wa, lru_ba, lru_wx, lru_bx, lru_lam,
                   gm_ln_g, gm_ln_b, gm_ws, gm_bs, w_out, g_ffn, w_up, ffn_conv_w, ffn_conv_b, w_down):
    w = w_in[l]
    c_ki = D_ATT + 2 * D_KV + D_QI
    c_rx = c_ki + IDX_DIM + N_IDX_HEADS
    w_r = jnp.concatenate([w[:, :c_ki], w[:, c_rx:], w[:, c_ki:c_rx],
                           jnp.zeros((D_MODEL, LANES - IDX_DIM - N_IDX_HEADS), w.dtype)], axis=1)
    tril = jnp.tril(jnp.ones((CHUNK, CHUNK), gm_ws.dtype))
    row2 = lambda a: a.reshape(1, -1)
    return dict(
        g_mix=row2(g_mix[l]), w_in=w_r.astype(MXU_DTYPE),
        rec_cw=rec_conv_w[l], rec_cb=row2(rec_conv_b[l]),
        wa=_block_diag(lru_wa[l]).astype(MXU_DTYPE), ba=row2(lru_ba[l]),
        wx=_block_diag(lru_wx[l]).astype(MXU_DTYPE), bx=row2(lru_bx[l]), lam=row2(lru_lam[l]),
        ln_g=row2(gm_ln_g[l]), ln_b=row2(gm_ln_b[l]), gm_w=gm_ws[l] * tril, gm_b=gm_bs[l],
        w_out=w_out[l].astype(MXU_DTYPE), g_ffn=row2(g_ffn[l]),
        w_up=w_up[l].astype(MXU_DTYPE), ffn_cw=ffn_conv_w[l], ffn_cb=row2(ffn_conv_b[l]),
        w_down=w_down[l].astype(MXU_DTYPE))


def _gmlp_mix_weights(p, c, n_seq):
    w = p['gm_w'][:, :c, :c]
    if n_seq > 1:
        w = jnp.einsum('ab,gts->gatbs', jnp.eye(n_seq, dtype=w.dtype), w).reshape(
            N_GM_HEADS, n_seq * c, n_seq * c)
    bias = jnp.repeat(p['gm_b'][:, :c].T, GM_HEAD_DIM, axis=1)
    return w.astype(MXU_DTYPE), jnp.tile(bias, (n_seq, 1))


def _prompt_layer(x, p, rel_bias, g_final, batch, seq, final):
    tm = 512
    (q, kt, vt, kit, kbf, vtbf, kibf, qi, _, wit, rec, gm) = _in_proj(x, p['g_mix'], p['w_in'], tm, seq)
    att = _prompt_attention(rel_bias, q, qi, wit, kibf, kbf, vtbf, batch, seq)
    zeros8 = jnp.zeros((batch, SUBLANES, D_REC), F32)
    rec_out, hlast, cstate = _rec_branch(rec, zeros8, zeros8, p['rec_cw'], p['rec_cb'], p['wa'], p['ba'],
                                         p['wx'], p['bx'], p['lam'], batch, seq, 256)
    c = min(CHUNK, seq)
    wmix, bias = _gmlp_mix_weights(p, c, 1)
    gm_out, _ = _gmlp(gm, wmix, bias, p['ln_g'], p['ln_b'], tm)
    x1, h2 = _out_proj(x, att, rec_out, gm_out, p['w_out'], p['g_ffn'], tm)
    hist16 = jnp.zeros((batch, HALO, 2 * D_FF), F32)
    x2, upst = _ffn_prompt(h2, x1, hist16, p['w_up'], p['ffn_cw'], p['ffn_cb'], p['w_down'], g_final,
                           batch, seq, tm, final)
    heads_last = lambda a: a.reshape(batch, N_KV_HEADS, HEAD_DIM, seq).transpose(0, 3, 1, 2)
    state = (heads_last(kt), heads_last(vt), kit.transpose(0, 2, 1), hlast[:, SUBLANES - 1, :],
             cstate[:, SUBLANES - (REC_CONV_W - 1):, :], upst[:, HALO - (FFN_CONV_W - 1):, :])
    return x2, state


def _sample_layer(x, p, rel_bias, g_final, l, batch, t_new, final, page_table, cache_k, cache_v, cache_kidx,
                  h0, rec_hist, ffn_hist):
    m = batch * t_new
    (q, k, v, ki, _, _, _, qi, kiwi, _, rec, gm) = _in_proj(x, p['g_mix'], p['w_in'], m)
    att = _sample_attention(page_table, rel_bias, q, qi, kiwi, k, v, cache_k, cache_v, cache_kidx,
                            l, batch, t_new)
    hist8 = _pad_rows_front(rec_hist, SUBLANES)
    h08 = jnp.broadcast_to(h0[:, None, :], (batch, SUBLANES, D_REC))
    rec_out, hlast, cstate = _rec_branch(rec, hist8, h08, p['rec_cw'], p['rec_cb'], p['wa'], p['ba'],
                                         p['wx'], p['bx'], p['lam'], batch, t_new, t_new)
    c = min(CHUNK, t_new)
    wmix, bias = _gmlp_mix_weights(p, c, m // c)
    gm_out, vn = _gmlp(gm, wmix, bias, p['ln_g'], p['ln_b'], m)
    x1, h2 = _out_proj(x, att, rec_out, gm_out, p['w_out'], p['g_ffn'], m)
    zrow = jnp.zeros((batch, t_new - 2, 2 * D_FF), F32)
    p2 = jnp.concatenate([ffn_hist, zrow], axis=1).reshape(m, 2 * D_FF)
    p1 = jnp.concatenate([ffn_hist[:, 1:], zrow, zrow[:, :1]], axis=1).reshape(m, 2 * D_FF)
    x2, up = _ffn_sample(h2, x1, p1, p2, p['w_up'], p['ffn_cw'], p['ffn_cb'], p['w_down'], g_final,
                         t_new, final)
    state = (k.reshape(batch, t_new, N_KV_HEADS, HEAD_DIM), v.reshape(batch, t_new, N_KV_HEADS, HEAD_DIM),
             ki.reshape(batch, t_new, IDX_DIM), hlast[:, SUBLANES - 1, :],
             cstate[:, SUBLANES - (REC_CONV_W - 1):, :],
             up.reshape(batch, t_new, 2 * D_FF)[:, t_new - (FFN_CONV_W - 1):, :],
             vn.reshape(batch, t_new, D_GM))
    return x2, state


def kernel(x_prompt, x_sample, cache_k, cache_v, cache_kidx, page_table, state_lru_h, state_conv_rec,
           state_conv_ffn, rel_bias, g_mix, w_in, rec_conv_w, rec_conv_b, lru_wa, lru_ba, lru_wx, lru_bx,
           lru_lam, gm_ln_g, gm_ln_b, gm_ws, gm_bs, w_out, g_ffn, w_up, ffn_conv_w, ffn_conv_b, w_down,
           g_final):
    batch, seq, _ = x_prompt.shape
    dec_batch, t_new, _ = x_sample.shape
    depth = w_in.shape[0]
    assert seq % 512 == 0 and t_new == SUBLANES and (dec_batch * t_new) % KEY_CHUNK == 0
    xp = x_prompt.reshape(batch * seq, D_MODEL)
    xs = x_sample.reshape(dec_batch * t_new, D_MODEL)
    gf = g_final.reshape(1, D_MODEL)
    p_states, s_states = [], []
    for l in range(depth):
        p = _layer_weights(l, g_mix, w_in, rec_conv_w, rec_conv_b, lru_wa, lru_ba, lru_wx, lru_bx, lru_lam,
                           gm_ln_g, gm_ln_b, gm_ws, gm_bs, w_out, g_ffn, w_up, ffn_conv_w, ffn_conv_b, w_down)
        final = l == depth - 1
        xp, st = _prompt_layer(xp, p, rel_bias, gf, batch, seq, final)
        p_states.append(st)
        xs, st = _sample_layer(xs, p, rel_bias, gf, l, dec_batch, t_new, final, page_table, cache_k, cache_v,
                               cache_kidx, state_lru_h[l], state_conv_rec[l], state_conv_ffn[l])
        s_states.append(st)
    stack = lambda states, i: jnp.stack([s[i] for s in states])
    return ((xp.reshape(batch, seq, D_MODEL), xs.reshape(dec_batch, t_new, D_MODEL))
            + tuple(stack(p_states, i) for i in range(6))
            + tuple(stack(s_states, i) for i in range(7)))
```

```python
import functools
import math

import jax
import jax.numpy as jnp
from jax import lax
from jax.experimental import pallas as pl
from jax.experimental.pallas import tpu as pltpu

F32 = jnp.float32
I32 = jnp.int32
MXU_DTYPE = jnp.bfloat16

D_MODEL = 1024
N_HEADS = 8
HEAD_DIM = 64
D_ATT = N_HEADS * HEAD_DIM
N_KV_HEADS = 2
GROUP = N_HEADS // N_KV_HEADS
D_KV = N_KV_HEADS * HEAD_DIM
N_IDX_HEADS = 4
IDX_DIM = 64
D_QI = N_IDX_HEADS * IDX_DIM
TOPK_MAX = 256
NUM_BUCKETS = 32
MAX_EXACT = NUM_BUCKETS // 2
MAX_DISTANCE = 128
D_REC = 256
N_REC_BLOCKS = 4
REC_CONV_W = 4
LRU_C = 8.0
D_GM = 256
N_GM_HEADS = 4
GM_HEAD_DIM = D_GM // N_GM_HEADS
CHUNK = 128
D_FF = 2816
FFN_CONV_W = 3
EPS = 1e-6
PAGE_SIZE = 128
Q_SCALE = HEAD_DIM ** -0.5

LANES = 128
SUBLANES = 8
VMEM_LIMIT_BYTES = 56 * 1024 * 1024

C_Q = 0
C_KV = C_Q + D_ATT
C_QI = C_KV + 2 * D_KV
C_REC = C_QI + D_QI
C_GM = C_REC + 2 * D_REC
C_KIWI = C_GM + 2 * D_GM
D_IN_PAD = C_KIWI + LANES

KEY_CHUNK = 256
Q_BLOCK = 128
INT_MIN = -2 ** 31
F32_MIN_NORMAL = 2.0 ** -126
PACKED_ROWS = 2 * SUBLANES
NEG_BIG = -2.0 ** 100
COUNT_ACCUMULATORS = 8

BUCKET_LO = tuple(
    b if b <= MAX_EXACT else math.ceil(MAX_EXACT * (MAX_DISTANCE / MAX_EXACT) ** ((b - MAX_EXACT) / (NUM_BUCKETS - MAX_EXACT)))
    for b in range(NUM_BUCKETS))


def _cparams(*sem):
    return pltpu.CompilerParams(dimension_semantics=sem, vmem_limit_bytes=VMEM_LIMIT_BYTES)


def _const_spec(shape):
    nd = len(shape)
    return pl.BlockSpec(shape, lambda *_: (0,) * nd, pipeline_mode=pl.Buffered(1))


def _rmsnorm(x, g):
    return x * lax.rsqrt(jnp.mean(x * x, axis=-1, keepdims=True) + EPS) * g


def _gelu(x):
    return 0.5 * x * (1.0 + jnp.tanh(math.sqrt(2.0 / math.pi) * (x + 0.044715 * (x * x * x))))


def _sigmoid(x):
    return 1.0 / (1.0 + jnp.exp(-x))


def _mm(a, b):
    return jnp.dot(a, b, preferred_element_type=F32)


def _mm_nt(a, b):
    return lax.dot_general(a, b, (((1,), (1,)), ((), ())), preferred_element_type=F32)


def _flush_subnormal(f):
    return jnp.where(jnp.abs(f) < F32_MIN_NORMAL, 0.0, f)


def _pattern_to_f32(c):
    return _flush_subnormal(pltpu.bitcast(jnp.where(c >= 0, c, c ^ 0x7FFFFFFF), F32))


def _tree_sum(xs):
    while len(xs) > 1:
        xs = [a + b for a, b in zip(xs[0::2], xs[1::2])] + ([xs[-1]] if len(xs) % 2 else [])
    return xs[0]


def _pair_loop(n_pairs, chunk_fn, init):
    def body(p, carry):
        return chunk_fn(2 * p + 1, chunk_fn(2 * p, carry))
    return lax.fori_loop(0, n_pairs, body, init)


def _count_above(ref, n_chunks, cand, strict):
    cb = jnp.broadcast_to(cand, (SUBLANES, LANES))
    accs = [jnp.zeros((SUBLANES, LANES), F32)] * COUNT_ACCUMULATORS
    for c in range(n_chunks):
        x = ref[c]
        for j in range(KEY_CHUNK // SUBLANES):
            blk = x[j * SUBLANES:(j + 1) * SUBLANES]
            k = j % COUNT_ACCUMULATORS
            accs[k] = accs[k] + jnp.where((blk > cb) if strict else (blk >= cb), 1.0, 0.0)
    return jnp.sum(_tree_sum(accs), axis=0, keepdims=True)


def _kth_largest(count_ge, need, start):
    def step(i, t):
        cand = t + lax.shift_left(jnp.int32(1), 31 - i)
        return jnp.where(count_ge(_pattern_to_f32(cand)) >= need, cand, t)

    t = lax.fori_loop(0, 32, step, start)
    return jnp.where(t == INT_MIN, -jnp.inf, _pattern_to_f32(t))


def _in_proj_body(x_ref, g_ref, w_ref, q_ref, k_ref, v_ref, ki_ref, kbf_ref, vt_ref, kibf_ref,
                  qi_ref, kiwi_ref, wit_ref, rec_ref, gm_ref, *, transposed_state):
    n_chunks = kbf_ref.shape[0]
    h = _rmsnorm(x_ref[...], g_ref[...]).astype(MXU_DTYPE)

    def proj(lo, hi):
        return _mm(h, w_ref[:, lo:hi])

    q_ref[...] = (proj(C_Q, C_KV) * Q_SCALE).astype(MXU_DTYPE)
    kv = proj(C_KV, C_QI)
    k = kv[:, :D_KV]
    v = kv[:, D_KV:]
    vt = v.T
    kb = k.astype(MXU_DTYPE)
    qi_ref[...] = proj(C_QI, C_REC).astype(MXU_DTYPE)
    rec_ref[...] = proj(C_REC, C_GM)
    gm_ref[...] = proj(C_GM, C_KIWI)
    kiwi = proj(C_KIWI, D_IN_PAD)
    ki = kiwi[:, :IDX_DIM]
    kiwi_ref[...] = kiwi
    kib = ki.astype(MXU_DTYPE)
    kiwi_t = kiwi.T
    wit_ref[...] = kiwi_t[IDX_DIM:IDX_DIM + SUBLANES, :]
    if transposed_state:
        k_ref[0] = k.T
        v_ref[0] = vt
        ki_ref[0] = kiwi_t[:IDX_DIM, :]
    else:
        k_ref[...] = k
        v_ref[...] = v
        ki_ref[...] = ki
    for c in range(n_chunks):
        rows = slice(c * KEY_CHUNK, (c + 1) * KEY_CHUNK)
        kbf_ref[c] = kb[rows]
        kibf_ref[c] = kib[rows]
        vt_ref[c] = vt[:, rows].astype(MXU_DTYPE)


def _in_proj(x, g, w, tm, seq=None):
    m = x.shape[0]
    nc = tm // KEY_CHUNK
    row = lambda n: pl.BlockSpec((tm, n), lambda i: (i, 0))
    chunked = lambda a, b: pl.BlockSpec((nc, a, b), lambda i: (i, 0, 0))
    if seq is None:
        state_shape = lambda n: jax.ShapeDtypeStruct((m, n), F32)
        state_spec = row
    else:
        tps = seq // tm
        state_shape = lambda n: jax.ShapeDtypeStruct((m // seq, n, seq), F32)
        state_spec = lambda n: pl.BlockSpec((1, n, tm), lambda i: (i // tps, 0, i % tps))
    out_shape = (
        jax.ShapeDtypeStruct((m, D_ATT), MXU_DTYPE),
        state_shape(D_KV),
        state_shape(D_KV),
        state_shape(IDX_DIM),
        jax.ShapeDtypeStruct((m // KEY_CHUNK, KEY_CHUNK, D_KV), MXU_DTYPE),
        jax.ShapeDtypeStruct((m // KEY_CHUNK, D_KV, KEY_CHUNK), MXU_DTYPE),
        jax.ShapeDtypeStruct((m // KEY_CHUNK, KEY_CHUNK, IDX_DIM), MXU_DTYPE),
        jax.ShapeDtypeStruct((m, D_QI), MXU_DTYPE),
        jax.ShapeDtypeStruct((m, LANES), F32),
        jax.ShapeDtypeStruct((SUBLANES, m), F32),
        jax.ShapeDtypeStruct((m, 2 * D_REC), F32),
        jax.ShapeDtypeStruct((m, 2 * D_GM), F32),
    )
    out_specs = (
        row(D_ATT), state_spec(D_KV), state_spec(D_KV), state_spec(IDX_DIM),
        chunked(KEY_CHUNK, D_KV), chunked(D_KV, KEY_CHUNK), chunked(KEY_CHUNK, IDX_DIM),
        row(D_QI), row(LANES), pl.BlockSpec((SUBLANES, tm), lambda i: (0, i)),
        row(2 * D_REC), row(2 * D_GM),
    )
    return pl.pallas_call(
        functools.partial(_in_proj_body, transposed_state=seq is not None),
        grid=(m // tm,),
        in_specs=[row(D_MODEL), _const_spec((1, D_MODEL)), _const_spec((D_MODEL, D_IN_PAD))],
        out_specs=out_specs,
        out_shape=out_shape,
        compiler_params=_cparams("arbitrary"),
        name="in_proj",
    )(x, g, w)


def _bias_table(rb_ref, head, delta, shape):
    row = lax.broadcasted_iota(I32, shape, 0)
    lane = lax.broadcasted_iota(I32, shape, 1)
    d = delta + lane - row
    far = rb_ref[NUM_BUCKETS - 1, head]
    val = jnp.full(shape, rb_ref[0, head] - far, F32)
    for b in range(1, NUM_BUCKETS - 1):
        val = jnp.where(d >= BUCKET_LO[b], rb_ref[b, head] - far, val)
    return jnp.where(d >= BUCKET_LO[NUM_BUCKETS - 1], 0.0, val)


def _prompt_attn_body(rb_ref, q_ref, qi_ref, wit_ref, ki_ref, k_ref, vt_ref, att_ref,
                      s_s, mb_s, tab_s, tri_s, thr_s, need_s, m_s, acc_s, qpad_s, *, topk):
    b = pl.program_id(0)
    i = pl.program_id(1)
    n_pairs = i // 4 + 1
    ck = (KEY_CHUNK, Q_BLOCK)

    @pl.when((b == 0) & (i == 0))
    def _init_tables():
        r = lax.broadcasted_iota(I32, (KEY_CHUNK, KEY_CHUNK), 0)
        c = lax.broadcasted_iota(I32, (KEY_CHUNK, KEY_CHUNK), 1)
        tri_s[...] = jnp.where(c <= r, 1.0, 0.0).astype(MXU_DTYPE)

        def per_head(h, carry):
            for ti in range(4):
                tab_s[ti, h] = _bias_table(rb_ref, h, ti * Q_BLOCK, ck)
            return carry

        lax.fori_loop(0, N_HEADS, per_head, 0)

    row = lax.broadcasted_iota(I32, ck, 0)
    lane = lax.broadcasted_iota(I32, ck, 1)
    q_pos = i * Q_BLOCK + lane

    qi = qi_ref[...]
    qis = jnp.concatenate([qi[:, h * IDX_DIM:(h + 1) * IDX_DIM] for h in range(N_IDX_HEADS)], axis=0)
    wit = wit_ref[...]
    w_row = jnp.concatenate([wit[h:h + 1, :] for h in range(N_IDX_HEADS)], axis=1)

    def score_chunk(c, carry):
        s4 = jnp.maximum(_mm_nt(ki_ref[c], qis), 0.0) * w_row
        s = _tree_sum([s4[:, h * Q_BLOCK:(h + 1) * Q_BLOCK] for h in range(N_IDX_HEADS)])
        valid = (c * KEY_CHUNK + row) <= q_pos
        s_s[c] = jnp.where(valid, s, -jnp.inf)
        return carry

    _pair_loop(n_pairs, score_chunk, 0)

    def search(n_chunks):
        t = _kth_largest(lambda cand: _count_above(s_s, n_chunks, cand, False), topk,
                         jnp.full((1, Q_BLOCK), INT_MIN, I32))
        thr_s[0:1, :] = t
        need_s[0:1, :] = topk - _count_above(s_s, n_chunks, t, True)

    n_chunks_causal = (i + 2) // 2
    for n in range(1, s_s.shape[0] + 1):
        pl.when(n_chunks_causal == n)(functools.partial(search, n))
    t = thr_s[0:1, :]
    need = need_s[0:1, :]

    tri = tri_s[...]

    def mask_chunk(c, carry):
        sc = s_s[c]
        tie = (sc == t) & (sc > -jnp.inf)
        tie_f = jnp.where(tie, 1.0, 0.0)
        incl = _mm(tri, tie_f.astype(MXU_DTYPE))
        rank = carry + incl - tie_f
        sel = (sc > t) | (tie & (rank < need))
        mb_s[c] = jnp.where(sel, 0.0, -jnp.inf).astype(mb_s.dtype)
        return carry + incl[KEY_CHUNK - 1:KEY_CHUNK, :]

    _pair_loop(n_pairs, mask_chunk, jnp.zeros((1, Q_BLOCK), F32))

    q = q_ref[...]
    zeros = jnp.zeros((Q_BLOCK, HEAD_DIM), MXU_DTYPE)
    for h in range(N_HEADS):
        qh = q[:, h * HEAD_DIM:(h + 1) * HEAD_DIM]
        parts = [qh, zeros] if h < GROUP else [zeros, qh]
        qpad_s[h * Q_BLOCK:(h + 1) * Q_BLOCK, :] = jnp.concatenate(parts, axis=1)
    m_s[...] = jnp.full(m_s.shape, NEG_BIG, F32)
    acc_s[...] = jnp.zeros(acc_s.shape, F32)
    ones_rows = jnp.ones((PACKED_ROWS, 2 * KEY_CHUNK), MXU_DTYPE)

    def attend_pair(p, near):
        c0, c1 = 2 * p, 2 * p + 1
        keys = jnp.concatenate([k_ref[c0], k_ref[c1]], axis=0)
        logits = _mm_nt(keys, qpad_s[...])
        mb = jnp.concatenate([mb_s[c0], mb_s[c1]], axis=0)
        if near:
            t0 = jnp.clip(i - 2 * c0, 0, 3)
            t1 = jnp.clip(i - 2 * c1, 0, 3)
        for g in range(N_KV_HEADS):
            ps, alphas = [], []
            for hh in range(GROUP):
                h = g * GROUP + hh
                cols = slice(h * Q_BLOCK, (h + 1) * Q_BLOCK)
                lt = logits[:, cols]
                if near:
                    lt = lt + jnp.concatenate([tab_s[t0, h], tab_s[t1, h]], axis=0)
                lt = lt.astype(MXU_DTYPE) + mb
                m_old = m_s[0:1, cols]
                m_new = jnp.maximum(m_old, jnp.max(lt, axis=0, keepdims=True).astype(F32))
                alphas.append(jnp.exp(m_old - m_new))
                ps.append(jnp.exp(lt - m_new.astype(MXU_DTYPE)))
                m_s[0:1, cols] = m_new
            pg = jnp.concatenate(ps, axis=1)
            ag = jnp.concatenate(alphas, axis=1)
            rows = slice(g * HEAD_DIM, (g + 1) * HEAD_DIM)
            vt = jnp.concatenate([jnp.concatenate([vt_ref[c0, rows, :], vt_ref[c1, rows, :]], axis=1), ones_rows],
                                 axis=0)
            acc_s[g] = acc_s[g] * ag + _mm(vt, pg)

    def far_pair(p, carry):
        attend_pair(2 * p, False)
        attend_pair(2 * p + 1, False)
        return carry

    n_far = jnp.maximum(n_pairs - 2, 0)
    lax.fori_loop(0, n_far // 2, far_pair, 0)

    @pl.when(n_far % 2 == 1)
    def _odd_far_pair():
        attend_pair(n_far - 1, False)

    @pl.when(n_pairs >= 2)
    def _near_pairs():
        attend_pair(n_pairs - 2, True)
        attend_pair(n_pairs - 1, True)

    @pl.when(n_pairs == 1)
    def _only_pair():
        attend_pair(0, True)

    for h in range(N_HEADS):
        g, hh = divmod(h, GROUP)
        cols = slice(hh * Q_BLOCK, (hh + 1) * Q_BLOCK)
        o = acc_s[g][:HEAD_DIM, cols] / acc_s[g][HEAD_DIM:HEAD_DIM + 1, cols]
        att_ref[:, h * HEAD_DIM:(h + 1) * HEAD_DIM] = o.T.astype(att_ref.dtype)


def _prompt_attention(rel_bias, q, qi, wit, kibf, kbf, vtbf, batch, seq):
    nq = seq // Q_BLOCK
    nc = seq // KEY_CHUNK
    topk = float(min(TOPK_MAX, seq // 4))
    qrow = lambda n: pl.BlockSpec((Q_BLOCK, n), lambda b, i: (b * nq + i, 0))
    per_batch = lambda a, c: pl.BlockSpec((nc, a, c), lambda b, i: (b, 0, 0))
    return pl.pallas_call(
        functools.partial(_prompt_attn_body, topk=topk),
        grid=(batch, nq),
        in_specs=[
            pl.BlockSpec(memory_space=pltpu.SMEM),
            qrow(D_ATT), qrow(D_QI),
            pl.BlockSpec((SUBLANES, Q_BLOCK), lambda b, i: (0, b * nq + i)),
            per_batch(KEY_CHUNK, IDX_DIM), per_batch(KEY_CHUNK, D_KV), per_batch(D_KV, KEY_CHUNK),
        ],
        out_specs=qrow(D_ATT),
        out_shape=jax.ShapeDtypeStruct((batch * seq, D_ATT), MXU_DTYPE),
        scratch_shapes=[
            pltpu.VMEM((nc, KEY_CHUNK, Q_BLOCK), F32),
            pltpu.VMEM((nc, KEY_CHUNK, Q_BLOCK), MXU_DTYPE),
            pltpu.VMEM((4, N_HEADS, KEY_CHUNK, Q_BLOCK), F32),
            pltpu.VMEM((KEY_CHUNK, KEY_CHUNK), MXU_DTYPE),
            pltpu.VMEM((SUBLANES, Q_BLOCK), F32),
            pltpu.VMEM((SUBLANES, Q_BLOCK), F32),
            pltpu.VMEM((SUBLANES, N_HEADS * Q_BLOCK), F32),
            pltpu.VMEM((N_KV_HEADS, HEAD_DIM + PACKED_ROWS, GROUP * Q_BLOCK), F32),
            pltpu.VMEM((N_HEADS * Q_BLOCK, D_KV), MXU_DTYPE),
        ],
        compiler_params=_cparams("arbitrary", "arbitrary"),
        name="prompt_attn",
    )(rel_bias, q, qi, wit, kibf, kbf, vtbf)


def _sample_attn_body(pt_ref, rb_ref, q_ref, qi_ref, kiwi_ref, knew_ref, vnew_ref,
                      ck_hbm, cv_hbm, cki_hbm, att_ref,
                      ktbuf, vtbuf, kitbuf, sem, *, topk, n_pages, layer, t_new):
    b = pl.program_id(0)
    past = n_pages * PAGE_SIZE
    slot = b % 2

    def page_copies(seq, buf, p):
        phys = pt_ref[seq, p]
        cols = pl.ds(pl.multiple_of(p * PAGE_SIZE, PAGE_SIZE), PAGE_SIZE)
        return (pltpu.make_async_copy(ck_hbm.at[layer, phys], ktbuf.at[buf, :, cols], sem.at[0, buf]),
                pltpu.make_async_copy(cv_hbm.at[layer, phys], vtbuf.at[buf, :, cols], sem.at[1, buf]),
                pltpu.make_async_copy(cki_hbm.at[layer, phys], kitbuf.at[buf, :, cols], sem.at[2, buf]))

    def start_pages(seq, buf):
        def body(p, carry):
            for cp in page_copies(seq, buf, p):
                cp.start()
            return carry
        lax.fori_loop(0, n_pages, body, 0)

    def wait_pages(seq, buf):
        def body(p, carry):
            for cp in page_copies(seq, buf, p):
                cp.wait()
            return carry
        lax.fori_loop(0, n_pages, body, 0)

    @pl.when(b == 0)
    def _first_fetch():
        start_pages(0, 0)

    @pl.when(b + 1 < pl.num_programs(0))
    def _prefetch_next():
        start_pages(b + 1, 1 - slot)

    qi = qi_ref[...]
    qis = jnp.concatenate([qi[:, h * IDX_DIM:(h + 1) * IDX_DIM] for h in range(N_IDX_HEADS)], axis=0)
    kiwi = kiwi_ref[...]
    w_col = jnp.concatenate([kiwi[:, IDX_DIM + h:IDX_DIM + h + 1] for h in range(N_IDX_HEADS)], axis=0)
    q = q_ref[...]
    zeros = jnp.zeros((t_new, HEAD_DIM), MXU_DTYPE)
    qpad = jnp.concatenate(
        [jnp.concatenate([q[:, h * HEAD_DIM:(h + 1) * HEAD_DIM], zeros] if h < GROUP else
                         [zeros, q[:, h * HEAD_DIM:(h + 1) * HEAD_DIM]], axis=1)
         for h in range(N_HEADS)], axis=0)

    wait_pages(b, slot)

    def idx_score(qk):
        s4 = jnp.maximum(qk, 0.0) * w_col
        s = s4[0:t_new]
        for h in range(1, N_IDX_HEADS):
            s = s + s4[h * t_new:(h + 1) * t_new]
        return s

    s_p = idx_score(_mm(qis, kitbuf[slot].astype(MXU_DTYPE)))
    ki_new = kiwi[:, :IDX_DIM].astype(MXU_DTYPE)
    ki_new = jnp.concatenate([ki_new, jnp.zeros((LANES - t_new, IDX_DIM), MXU_DTYPE)], axis=0)
    rown = lax.broadcasted_iota(I32, (t_new, LANES), 0)
    lanen = lax.broadcasted_iota(I32, (t_new, LANES), 1)
    valid_n = lanen <= rown
    s_n = jnp.where(valid_n, idx_score(_mm_nt(qis, ki_new)), -jnp.inf)

    def count(cand, strict):
        cmp = (lambda x: x > cand) if strict else (lambda x: x >= cand)
        return (jnp.sum(jnp.where(cmp(s_p), 1.0, 0.0), axis=1, keepdims=True)
                + jnp.sum(jnp.where(cmp(s_n), 1.0, 0.0), axis=1, keepdims=True))

    t = _kth_largest(lambda cand: count(cand, False), topk, jnp.full((t_new, 1), INT_MIN, I32))
    need = topk - count(t, True)

    r = lax.broadcasted_iota(I32, (KEY_CHUNK, KEY_CHUNK), 0)
    c = lax.broadcasted_iota(I32, (KEY_CHUNK, KEY_CHUNK), 1)
    tri = jnp.where(r <= c, 1.0, 0.0).astype(MXU_DTYPE)
    tie_p = s_p == t
    carry = jnp.zeros((t_new, 1), F32)
    mask_parts = []
    for ch in range(past // KEY_CHUNK):
        cols = slice(ch * KEY_CHUNK, (ch + 1) * KEY_CHUNK)
        tie_c = tie_p[:, cols]
        tie_f = jnp.where(tie_c, 1.0, 0.0)
        incl = _mm(tie_f.astype(MXU_DTYPE), tri)
        sel = (s_p[:, cols] > t) | (tie_c & ((carry + incl - tie_f) < need))
        mask_parts.append(jnp.where(sel, 0.0, -jnp.inf))
        carry = carry + incl[:, KEY_CHUNK - 1:KEY_CHUNK]
    mb_p = jnp.concatenate(mask_parts, axis=1)
    tie_n = (s_n == t) & valid_n
    tie_nf = jnp.where(tie_n, 1.0, 0.0)
    incl_n = _mm(tie_nf.astype(MXU_DTYPE), tri[:LANES, :LANES])
    sel_n = (s_n > t) | (tie_n & ((carry + incl_n - tie_nf) < need))
    mb_n = jnp.where(sel_n & valid_n, 0.0, -jnp.inf)

    lg_p = _mm(qpad, ktbuf[slot].astype(MXU_DTYPE))
    k_new = jnp.concatenate([knew_ref[...].astype(MXU_DTYPE),
                             jnp.zeros((LANES - t_new, D_KV), MXU_DTYPE)], axis=0)
    lg_n = _mm_nt(qpad, k_new)
    near = slice(past - LANES, past)
    lp_rows, ln_rows = [], []
    for h in range(N_HEADS):
        rows = slice(h * t_new, (h + 1) * t_new)
        far = rb_ref[NUM_BUCKETS - 1, h]

        def bias(d):
            val = jnp.full(d.shape, rb_ref[0, h] - far, F32)
            for bk in range(1, NUM_BUCKETS - 1):
                val = jnp.where(d >= BUCKET_LO[bk], rb_ref[bk, h] - far, val)
            return jnp.where(d >= BUCKET_LO[NUM_BUCKETS - 1], 0.0, val)

        lp = lg_p[rows] + mb_p
        lp_near = lp[:, near] + bias(rown + (LANES - lanen))
        lp_rows.append(jnp.concatenate([lp[:, :past - LANES], lp_near], axis=1))
        ln_rows.append(lg_n[rows] + mb_n + bias(rown - lanen))
    lp_all = jnp.concatenate(lp_rows, axis=0)
    ln_all = jnp.concatenate(ln_rows, axis=0)
    m = jnp.maximum(jnp.max(lp_all, axis=1, keepdims=True), jnp.max(ln_all, axis=1, keepdims=True))
    p_p = jnp.exp(lp_all - m)
    p_n = jnp.exp(ln_all - m)
    denom = jnp.sum(p_p, axis=1, keepdims=True) + jnp.sum(p_n, axis=1, keepdims=True)
    v_new = jnp.concatenate([vnew_ref[...].astype(MXU_DTYPE),
                             jnp.zeros((LANES - t_new, D_KV), MXU_DTYPE)], axis=0)
    o = (_mm_nt(p_p.astype(MXU_DTYPE), vtbuf[slot].astype(MXU_DTYPE))
         + _mm(p_n.astype(MXU_DTYPE), v_new)) / denom
    for h in range(N_HEADS):
        g = h // GROUP
        att_ref[:, h * HEAD_DIM:(h + 1) * HEAD_DIM] = (
            o[h * t_new:(h + 1) * t_new, g * HEAD_DIM:(g + 1) * HEAD_DIM].astype(att_ref.dtype))


def _sample_attention(page_table, rel_bias, q, qi, kiwi, k_new, v_new, cache_k, cache_v, cache_kidx,
                      layer, batch, t_new):
    n_pages = page_table.shape[1]
    past = n_pages * PAGE_SIZE
    topk = float(min(TOPK_MAX, (past + t_new) // 4))
    depth, n_pool = cache_k.shape[:2]
    ck = cache_k.transpose(0, 1, 3, 4, 2).reshape(depth, n_pool, D_KV, PAGE_SIZE)
    cv = cache_v.transpose(0, 1, 3, 4, 2).reshape(depth, n_pool, D_KV, PAGE_SIZE)
    cki = cache_kidx.transpose(0, 1, 3, 2)
    qrow = lambda n: pl.BlockSpec((t_new, n), lambda b, pt: (b, 0))
    hbm = pl.BlockSpec(memory_space=pl.ANY)
    grid_spec = pltpu.PrefetchScalarGridSpec(
        num_scalar_prefetch=1,
        grid=(batch,),
        in_specs=[pl.BlockSpec(memory_space=pltpu.SMEM),
                  qrow(D_ATT), qrow(D_QI), qrow(LANES), qrow(D_KV), qrow(D_KV), hbm, hbm, hbm],
        out_specs=qrow(D_ATT),
        scratch_shapes=[
            pltpu.VMEM((2, D_KV, past), F32),
            pltpu.VMEM((2, D_KV, past), F32),
            pltpu.VMEM((2, IDX_DIM, past), F32),
            pltpu.SemaphoreType.DMA((3, 2)),
        ],
    )
    return pl.pallas_call(
        functools.partial(_sample_attn_body, topk=topk, n_pages=n_pages, layer=layer, t_new=t_new),
        grid_spec=grid_spec,
        out_shape=jax.ShapeDtypeStruct((batch * t_new, D_ATT), MXU_DTYPE),
        compiler_params=_cparams("arbitrary"),
        name="sample_attn",
    )(page_table, rel_bias, q, qi, kiwi, k_new, v_new, ck, cv, cki)


def _rec_body(rec_ref, hist_ref, h0_ref, cw_ref, cb_ref, wa_ref, ba_ref, wx_ref, bx_ref, lam_ref,
              out_ref, hlast_ref, cstate_ref, hcar, xprev):
    tt = rec_ref.shape[0]

    @pl.when(pl.program_id(1) == 0)
    def _load_state():
        hcar[...] = h0_ref[0]
        xprev[...] = hist_ref[0]

    rx = rec_ref[:, :D_REC]
    rg = rec_ref[:, D_REC:]
    ext = jnp.concatenate([xprev[...], rx], axis=0)
    xc = cb_ref[...]
    for j in range(REC_CONV_W - 1):
        xc = xc + pltpu.roll(ext, REC_CONV_W - 1 - j, 0)[SUBLANES:] * cw_ref[j:j + 1, :]
    xc = xc + rx * cw_ref[REC_CONV_W - 1:REC_CONV_W, :]
    xcb = xc.astype(MXU_DTYPE)
    r = _sigmoid(_mm(xcb, wa_ref[...]) + ba_ref[...])
    gi = _sigmoid(_mm(xcb, wx_ref[...]) + bx_ref[...])
    nl = -lam_ref[...]
    softplus = jnp.maximum(nl, 0.0) + jnp.log1p(jnp.exp(-jnp.abs(nl)))
    log_a = -LRU_C * r * softplus
    a = jnp.exp(log_a)
    u = jnp.sqrt(-jnp.tanh(log_a) * (a * a + 1.0)) * (gi * xc)
    row = lax.broadcasted_iota(I32, (tt, D_REC), 0)
    s = 1
    while s < tt:
        keep = row >= s
        u = jnp.where(keep, u + a * pltpu.roll(u, s, 0), u)
        a = jnp.where(keep, a * pltpu.roll(a, s, 0), a)
        s *= 2
    hs = u + a * hcar[0:1, :]
    hcar[...] = jnp.broadcast_to(hs[tt - 1:tt, :], hcar.shape)
    xprev[...] = rx[tt - SUBLANES:, :]
    out_ref[...] = (_gelu(rg) * hs).astype(out_ref.dtype)
    hlast_ref[0] = hs[tt - SUBLANES:, :]
    cstate_ref[0] = rx[tt - SUBLANES:, :]


def _rec_branch(rec, hist8, h08, cw, cb, wa, ba, wx, bx, lam, batch, t_len, tt):
    nt = t_len // tt
    state = pl.BlockSpec((1, SUBLANES, D_REC), lambda b, t: (b, 0, 0))
    vec = _const_spec((1, D_REC))
    return pl.pallas_call(
        _rec_body,
        grid=(batch, nt),
        in_specs=[pl.BlockSpec((tt, 2 * D_REC), lambda b, t: (b * nt + t, 0)), state, state,
                  _const_spec((REC_CONV_W, D_REC)), vec, _const_spec((D_REC, D_REC)), vec,
                  _const_spec((D_REC, D_REC)), vec, vec],
        out_specs=(pl.BlockSpec((tt, D_REC), lambda b, t: (b * nt + t, 0)), state, state),
        out_shape=(jax.ShapeDtypeStruct((batch * t_len, D_REC), MXU_DTYPE),
                   jax.ShapeDtypeStruct((batch, SUBLANES, D_REC), F32),
                   jax.ShapeDtypeStruct((batch, SUBLANES, D_REC), F32)),
        scratch_shapes=[pltpu.VMEM((SUBLANES, D_REC), F32), pltpu.VMEM((SUBLANES, D_REC), F32)],
        compiler_params=_cparams("arbitrary", "arbitrary"),
        name="rec_branch",
    )(rec, hist8, h08, cw, cb, wa, ba, wx, bx, lam)


def _gmlp_body(gm_ref, wmix_ref, bias_ref, lng_ref, lnb_ref, out_ref, vn_ref):
    rows = wmix_ref.shape[1]
    n_sub = gm_ref.shape[0] // rows
    head_of_lane = lax.broadcasted_iota(I32, (rows, D_GM), 1) // GM_HEAD_DIM
    for sb in range(n_sub):
        sl = slice(sb * rows, (sb + 1) * rows)
        u = _gelu(gm_ref[sl, :D_GM])
        gv = _gelu(gm_ref[sl, D_GM:])
        xc = gv - jnp.mean(gv, axis=-1, keepdims=True)
        var = jnp.mean(xc * xc, axis=-1, keepdims=True)
        vn = xc * lax.rsqrt(var + EPS) * lng_ref[...] + lnb_ref[...]
        vn_ref[sl, :] = vn
        vnb = vn.astype(MXU_DTYPE)
        mix = jnp.zeros((rows, D_GM), F32)
        for g in range(N_GM_HEADS):
            mix = jnp.where(head_of_lane == g, _mm(wmix_ref[g], vnb), mix)
        out_ref[sl, :] = (u * (mix + bias_ref[...])).astype(out_ref.dtype)


def _gmlp(gm, wmix, bias, lng, lnb, tm):
    m = gm.shape[0]
    rows = wmix.shape[1]
    return pl.pallas_call(
        _gmlp_body,
        grid=(m // tm,),
        in_specs=[pl.BlockSpec((tm, 2 * D_GM), lambda i: (i, 0)),
                  _const_spec((N_GM_HEADS, rows, rows)), _const_spec((rows, D_GM)),
                  _const_spec((1, D_GM)), _const_spec((1, D_GM))],
        out_specs=(pl.BlockSpec((tm, D_GM), lambda i: (i, 0)), pl.BlockSpec((tm, D_GM), lambda i: (i, 0))),
        out_shape=(jax.ShapeDtypeStruct((m, D_GM), MXU_DTYPE), jax.ShapeDtypeStruct((m, D_GM), F32)),
        compiler_params=_cparams("arbitrary"),
        name="gmlp",
    )(gm, wmix, bias, lng, lnb)


def _out_proj_body(x_ref, att_ref, rec_ref, gm_ref, w_ref, g_ref, x1_ref, h2_ref):
    y = (_mm(att_ref[...], w_ref[0:D_ATT, :])
         + _mm(rec_ref[...], w_ref[D_ATT:D_ATT + D_REC, :])
         + _mm(gm_ref[...], w_ref[D_ATT + D_REC:, :]))
    x1 = x_ref[...] + y
    x1_ref[...] = x1
    h2_ref[...] = _rmsnorm(x1, g_ref[...]).astype(h2_ref.dtype)


def _out_proj(x, att, rec, gm, w, g, tm):
    m = x.shape[0]
    row = lambda n: pl.BlockSpec((tm, n), lambda i: (i, 0))
    return pl.pallas_call(
        _out_proj_body,
        grid=(m // tm,),
        in_specs=[row(D_MODEL), row(D_ATT), row(D_REC), row(D_GM),
                  _const_spec((D_MODEL, D_MODEL)), _const_spec((1, D_MODEL))],
        out_specs=(row(D_MODEL), row(D_MODEL)),
        out_shape=(jax.ShapeDtypeStruct((m, D_MODEL), F32), jax.ShapeDtypeStruct((m, D_MODEL), MXU_DTYPE)),
        compiler_params=_cparams("arbitrary"),
        name="out_proj",
    )(x, att, rec, gm, w, g)


FF_TILE = 256
HALO = 16


def _ffn_finish(x1, acc, gf_ref, out_ref, final):
    x2 = x1 + acc
    out_ref[...] = _rmsnorm(x2, gf_ref[...]) if final else x2


def _ffn_prompt_body(x_ref, xh_ref, att_ref, atth_ref, rec_ref, rech_ref, gm_ref, gmh_ref, wo_ref, g2_ref,
                     hist_ref, wup_ref, cw_ref, cb_ref, wdn_ref, gf_ref,
                     out_ref, upst_ref, act_s, *, tiles_per_seq, final):
    tm = x_ref.shape[0]
    first = (pl.program_id(0) % tiles_per_seq) == 0
    ext = lambda halo, main: jnp.concatenate([halo[...], main[...]], axis=0)
    x1_ext = ext(xh_ref, x_ref) + (_mm(ext(atth_ref, att_ref), wo_ref[0:D_ATT, :])
                                   + _mm(ext(rech_ref, rec_ref), wo_ref[D_ATT:D_ATT + D_REC, :])
                                   + _mm(ext(gmh_ref, gm_ref), wo_ref[D_ATT + D_REC:, :]))
    h_ext = _rmsnorm(x1_ext, g2_ref[...]).astype(MXU_DTYPE)
    for j in range(D_FF // FF_TILE):
        halves = []
        for base in (0, D_FF):
            cols = slice(base + j * FF_TILE, base + (j + 1) * FF_TILE)
            up = _mm(h_ext, wup_ref[:, cols])
            up_m = up[HALO:]
            upst_ref[0, :, cols] = up_m[tm - HALO:, :]
            ext = jnp.concatenate([jnp.where(first, hist_ref[0, :, cols], up[:HALO]), up_m], axis=0)
            uc = cb_ref[:, cols]
            for jj in range(FFN_CONV_W - 1):
                uc = uc + pltpu.roll(ext, FFN_CONV_W - 1 - jj, 0)[HALO:] * cw_ref[jj:jj + 1, cols]
            halves.append(uc + up_m * cw_ref[FFN_CONV_W - 1:FFN_CONV_W, cols])
        act_s[:, j * FF_TILE:(j + 1) * FF_TILE] = (_gelu(halves[0]) * halves[1]).astype(MXU_DTYPE)
    _ffn_finish(x1_ext[HALO:], _mm(act_s[...], wdn_ref[...]), gf_ref, out_ref, final)


def _ffn_prompt(x, att, rec, gm, wo, g2, hist16, wup, cw, cb, wdn, gf, batch, seq, tm, final):
    m = x.shape[0]
    tps = seq // tm
    row = lambda n: pl.BlockSpec((tm, n), lambda i: (i, 0))
    halo = lambda n: pl.BlockSpec((HALO, n), lambda i: (jnp.maximum(i * (tm // HALO) - 1, 0), 0))
    state = pl.BlockSpec((1, HALO, 2 * D_FF), lambda i: (i // tps, 0, 0))
    return pl.pallas_call(
        functools.partial(_ffn_prompt_body, tiles_per_seq=tps, final=final),
        grid=(m // tm,),
        in_specs=[row(D_MODEL), halo(D_MODEL), row(D_ATT), halo(D_ATT), row(D_REC), halo(D_REC),
                  row(D_GM), halo(D_GM), _const_spec((D_MODEL, D_MODEL)), _const_spec((1, D_MODEL)), state,
                  _const_spec((D_MODEL, 2 * D_FF)), _const_spec((FFN_CONV_W, 2 * D_FF)),
                  _const_spec((1, 2 * D_FF)), _const_spec((D_FF, D_MODEL)), _const_spec((1, D_MODEL))],
        out_specs=(row(D_MODEL), state),
        out_shape=(jax.ShapeDtypeStruct((m, D_MODEL), F32),
                   jax.ShapeDtypeStruct((batch, HALO, 2 * D_FF), F32)),
        scratch_shapes=[pltpu.VMEM((tm, D_FF), MXU_DTYPE)],
        compiler_params=_cparams("arbitrary"),
        name="ffn_prompt",
    )(x, x, att, att, rec, rec, gm, gm, wo, g2, hist16, wup, cw, cb, wdn, gf)


def _ffn_sample_body(h_ref, x1_ref, p1_ref, p2_ref, wup_ref, cw_ref, cb_ref, wdn_ref, gf_ref,
                     out_ref, up_ref, *, t_new, final):
    tm = h_ref.shape[0]
    hm = h_ref[...]
    pos = lax.broadcasted_iota(I32, (tm, FF_TILE), 0) % t_new
    acc = jnp.zeros((tm, D_MODEL), F32)
    for j in range(D_FF // FF_TILE):
        halves = []
        for base in (0, D_FF):
            cols = slice(base + j * FF_TILE, base + (j + 1) * FF_TILE)
            up = _mm(hm, wup_ref[:, cols])
            up_ref[:, cols] = up
            tap2 = jnp.where(pos < 2, p2_ref[:, cols], pltpu.roll(up, 2, 0))
            tap1 = jnp.where(pos < 1, p1_ref[:, cols], pltpu.roll(up, 1, 0))
            halves.append(cb_ref[:, cols] + tap2 * cw_ref[0:1, cols] + tap1 * cw_ref[1:2, cols]
                          + up * cw_ref[2:3, cols])
        act = (_gelu(halves[0]) * halves[1]).astype(MXU_DTYPE)
        acc = acc + _mm(act, wdn_ref[j * FF_TILE:(j + 1) * FF_TILE, :])
    _ffn_finish(x1_ref[...], acc, gf_ref, out_ref, final)


def _ffn_sample(h2, x1, p1, p2, wup, cw, cb, wdn, gf, t_new, final):
    m = h2.shape[0]
    full = lambda n: _const_spec((m, n))
    return pl.pallas_call(
        functools.partial(_ffn_sample_body, t_new=t_new, final=final),
        grid=(1,),
        in_specs=[full(D_MODEL), full(D_MODEL), full(2 * D_FF), full(2 * D_FF),
                  _const_spec((D_MODEL, 2 * D_FF)), _const_spec((FFN_CONV_W, 2 * D_FF)),
                  _const_spec((1, 2 * D_FF)), _const_spec((D_FF, D_MODEL)), _const_spec((1, D_MODEL))],
        out_specs=(full(D_MODEL), full(2 * D_FF)),
        out_shape=(jax.ShapeDtypeStruct((m, D_MODEL), F32), jax.ShapeDtypeStruct((m, 2 * D_FF), F32)),
        compiler_params=_cparams("arbitrary"),
        name="ffn_sample",
    )(h2, x1, p1, p2, wup, cw, cb, wdn, gf)


def _block_diag(w):
    n, blk, _ = w.shape
    out = jnp.zeros((n * blk, n * blk), w.dtype)
    for i in range(n):
        out = out.at[i * blk:(i + 1) * blk, i * blk:(i + 1) * blk].set(w[i])
    return out


def _pad_rows_front(a, rows):
    pad = rows - a.shape[1]
    return jnp.pad(a, ((0, 0), (pad, 0), (0, 0)))


def _layer_weights(l, g_mix, w_in, rec_conv_w, rec_conv_b, lru_wa, lru_ba, lru_wx, lru_bx, lru_lam,
                   gm_ln_g, gm_ln_b, gm_ws, gm_bs, w_out, g_ffn, w_up, ffn_conv_w, ffn_conv_b, w_down):
    w = w_in[l]
    c_ki = D_ATT + 2 * D_KV + D_QI
    c_rx = c_ki + IDX_DIM + N_IDX_HEADS
    w_r = jnp.concatenate([w[:, :c_ki], w[:, c_rx:], w[:, c_ki:c_rx],
                           jnp.zeros((D_MODEL, LANES - IDX_DIM - N_IDX_HEADS), w.dtype)], axis=1)
    tril = jnp.tril(jnp.ones((CHUNK, CHUNK), gm_ws.dtype))
    row2 = lambda a: a.reshape(1, -1)
    return dict(
        g_mix=row2(g_mix[l]), w_in=w_r.astype(MXU_DTYPE),
        rec_cw=rec_conv_w[l], rec_cb=row2(rec_conv_b[l]),
        wa=_block_diag(lru_wa[l]).astype(MXU_DTYPE), ba=row2(lru_ba[l]),
        wx=_block_diag(lru_wx[l]).astype(MXU_DTYPE), bx=row2(lru_bx[l]), lam=row2(lru_lam[l]),
        ln_g=row2(gm_ln_g[l]), ln_b=row2(gm_ln_b[l]), gm_w=gm_ws[l] * tril, gm_b=gm_bs[l],
        w_out=w_out[l].astype(MXU_DTYPE), g_ffn=row2(g_ffn[l]),
        w_up=w_up[l].astype(MXU_DTYPE), ffn_cw=ffn_conv_w[l], ffn_cb=row2(ffn_conv_b[l]),
        w_down=w_down[l].astype(MXU_DTYPE))


def _gmlp_mix_weights(p, c, n_seq):
    w = p['gm_w'][:, :c, :c]
    if n_seq > 1:
        w = jnp.einsum('ab,gts->gatbs', jnp.eye(n_seq, dtype=w.dtype), w).reshape(
            N_GM_HEADS, n_seq * c, n_seq * c)
    bias = jnp.repeat(p['gm_b'][:, :c].T, GM_HEAD_DIM, axis=1)
    return w.astype(MXU_DTYPE), jnp.tile(bias, (n_seq, 1))


def _prompt_layer(x, p, rel_bias, g_final, batch, seq, final):
    tm = 512
    (q, kt, vt, kit, kbf, vtbf, kibf, qi, _, wit, rec, gm) = _in_proj(x, p['g_mix'], p['w_in'], tm, seq)
    att = _prompt_attention(rel_bias, q, qi, wit, kibf, kbf, vtbf, batch, seq)
    zeros8 = jnp.zeros((batch, SUBLANES, D_REC), F32)
    rec_out, hlast, cstate = _rec_branch(rec, zeros8, zeros8, p['rec_cw'], p['rec_cb'], p['wa'], p['ba'],
                                         p['wx'], p['bx'], p['lam'], batch, seq, 256)
    c = min(CHUNK, seq)
    wmix, bias = _gmlp_mix_weights(p, c, 1)
    gm_out, _ = _gmlp(gm, wmix, bias, p['ln_g'], p['ln_b'], tm)
    hist16 = jnp.zeros((batch, HALO, 2 * D_FF), F32)
    x2, upst = _ffn_prompt(x, att, rec_out, gm_out, p['w_out'], p['g_ffn'], hist16, p['w_up'], p['ffn_cw'],
                           p['ffn_cb'], p['w_down'], g_final, batch, seq, tm, final)
    heads_last = lambda a: a.reshape(batch, N_KV_HEADS, HEAD_DIM, seq).transpose(0, 3, 1, 2)
    state = (heads_last(kt), heads_last(vt), kit.transpose(0, 2, 1), hlast[:, SUBLANES - 1, :],
             cstate[:, SUBLANES - (REC_CONV_W - 1):, :], upst[:, HALO - (FFN_CONV_W - 1):, :])
    return x2, state


def _sample_layer(x, p, rel_bias, g_final, l, batch, t_new, final, page_table, cache_k, cache_v, cache_kidx,
                  h0, rec_hist, ffn_hist):
    m = batch * t_new
    (q, k, v, ki, _, _, _, qi, kiwi, _, rec, gm) = _in_proj(x, p['g_mix'], p['w_in'], m)
    att = _sample_attention(page_table, rel_bias, q, qi, kiwi, k, v, cache_k, cache_v, cache_kidx,
                            l, batch, t_new)
    hist8 = _pad_rows_front(rec_hist, SUBLANES)
    h08 = jnp.broadcast_to(h0[:, None, :], (batch, SUBLANES, D_REC))
    rec_out, hlast, cstate = _rec_branch(rec, hist8, h08, p['rec_cw'], p['rec_cb'], p['wa'], p['ba'],
                                         p['wx'], p['bx'], p['lam'], batch, t_new, t_new)
    c = min(CHUNK, t_new)
    wmix, bias = _gmlp_mix_weights(p, c, m // c)
    gm_out, vn = _gmlp(gm, wmix, bias, p['ln_g'], p['ln_b'], m)
    x1, h2 = _out_proj(x, att, rec_out, gm_out, p['w_out'], p['g_ffn'], m)
    zrow = jnp.zeros((batch, t_new - 2, 2 * D_FF), F32)
    p2 = jnp.concatenate([ffn_hist, zrow], axis=1).reshape(m, 2 * D_FF)
    p1 = jnp.concatenate([ffn_hist[:, 1:], zrow, zrow[:, :1]], axis=1).reshape(m, 2 * D_FF)
    x2, up = _ffn_sample(h2, x1, p1, p2, p['w_up'], p['ffn_cw'], p['ffn_cb'], p['w_down'], g_final,
                         t_new, final)
    state = (k.reshape(batch, t_new, N_KV_HEADS, HEAD_DIM), v.reshape(batch, t_new, N_KV_HEADS, HEAD_DIM),
             ki.reshape(batch, t_new, IDX_DIM), hlast[:, SUBLANES - 1, :],
             cstate[:, SUBLANES - (REC_CONV_W - 1):, :],
             up.reshape(batch, t_new, 2 * D_FF)[:, t_new - (FFN_CONV_W - 1):, :],
             vn.reshape(batch, t_new, D_GM))
    return x2, state


def kernel(x_prompt, x_sample, cache_k, cache_v, cache_kidx, page_table, state_lru_h, state_conv_rec,
           state_conv_ffn, rel_bias, g_mix, w_in, rec_conv_w, rec_conv_b, lru_wa, lru_ba, lru_wx, lru_bx,
           lru_lam, gm_ln_g, gm_ln_b, gm_ws, gm_bs, w_out, g_ffn, w_up, ffn_conv_w, ffn_conv_b, w_down,
           g_final):
    batch, seq, _ = x_prompt.shape
    dec_batch, t_new, _ = x_sample.shape
    depth = w_in.shape[0]
    assert seq % 512 == 0 and t_new == SUBLANES and (dec_batch * t_new) % KEY_CHUNK == 0
    xp = x_prompt.reshape(batch * seq, D_MODEL)
    xs = x_sample.reshape(dec_batch * t_new, D_MODEL)
    gf = g_final.reshape(1, D_MODEL)
    p_states, s_states = [], []
    for l in range(depth):
        p = _layer_weights(l, g_mix, w_in, rec_conv_w, rec_conv_b, lru_wa, lru_ba, lru_wx, lru_bx, lru_lam,
                           gm_ln_g, gm_ln_b, gm_ws, gm_bs, w_out, g_ffn, w_up, ffn_conv_w, ffn_conv_b, w_down)
        final = l == depth - 1
        xp, st = _prompt_layer(xp, p, rel_bias, gf, batch, seq, final)
        p_states.append(st)
        xs, st = _sample_layer(xs, p, rel_bias, gf, l, dec_batch, t_new, final, page_table, cache_k, cache_v,
                               cache_kidx, state_lru_h[l], state_conv_rec[l], state_conv_ffn[l])
        s_states.append(st)
    stack = lambda states, i: jnp.stack([s[i] for s in states])
    return ((xp.reshape(batch, seq, D_MODEL), xs.reshape(dec_batch, t_new, D_MODEL))
            + tuple(stack(p_states, i) for i in range(6))
            + tuple(stack(s_states, i) for i in range(7)))
```

```python
import functools
import math

import jax
import jax.numpy as jnp
from jax import lax
from jax.experimental import pallas as pl
from jax.experimental.pallas import tpu as pltpu

F32 = jnp.float32
I32 = jnp.int32
MXU_DTYPE = jnp.bfloat16

D_MODEL = 1024
N_HEADS = 8
HEAD_DIM = 64
D_ATT = N_HEADS * HEAD_DIM
N_KV_HEADS = 2
GROUP = N_HEADS // N_KV_HEADS
D_KV = N_KV_HEADS * HEAD_DIM
N_IDX_HEADS = 4
IDX_DIM = 64
D_QI = N_IDX_HEADS * IDX_DIM
TOPK_MAX = 256
NUM_BUCKETS = 32
MAX_EXACT = NUM_BUCKETS // 2
MAX_DISTANCE = 128
D_REC = 256
N_REC_BLOCKS = 4
REC_CONV_W = 4
LRU_C = 8.0
D_GM = 256
N_GM_HEADS = 4
GM_HEAD_DIM = D_GM // N_GM_HEADS
CHUNK = 128
D_FF = 2816
FFN_CONV_W = 3
EPS = 1e-6
PAGE_SIZE = 128
Q_SCALE = HEAD_DIM ** -0.5

LANES = 128
SUBLANES = 8
VMEM_LIMIT_BYTES = 56 * 1024 * 1024

C_Q = 0
C_KV = C_Q + D_ATT
C_QI = C_KV + 2 * D_KV
C_REC = C_QI + D_QI
C_GM = C_REC + 2 * D_REC
C_KIWI = C_GM + 2 * D_GM
D_IN_PAD = C_KIWI + LANES

KEY_CHUNK = 256
Q_BLOCK = 128
INT_MIN = -2 ** 31
F32_MIN_NORMAL = 2.0 ** -126
PACKED_ROWS = 2 * SUBLANES
NEG_BIG = -2.0 ** 100
COUNT_ACCUMULATORS = 8

BUCKET_LO = tuple(
    b if b <= MAX_EXACT else math.ceil(MAX_EXACT * (MAX_DISTANCE / MAX_EXACT) ** ((b - MAX_EXACT) / (NUM_BUCKETS - MAX_EXACT)))
    for b in range(NUM_BUCKETS))


def _cparams(*sem):
    return pltpu.CompilerParams(dimension_semantics=sem, vmem_limit_bytes=VMEM_LIMIT_BYTES)


def _const_spec(shape):
    nd = len(shape)
    return pl.BlockSpec(shape, lambda *_: (0,) * nd, pipeline_mode=pl.Buffered(1))


def _rmsnorm(x, g):
    return x * lax.rsqrt(jnp.mean(x * x, axis=-1, keepdims=True) + EPS) * g


def _gelu(x):
    return 0.5 * x * (1.0 + jnp.tanh(math.sqrt(2.0 / math.pi) * (x + 0.044715 * (x * x * x))))


def _sigmoid(x):
    return 1.0 / (1.0 + jnp.exp(-x))


def _mm(a, b):
    return jnp.dot(a, b, preferred_element_type=F32)


def _mm_nt(a, b):
    return lax.dot_general(a, b, (((1,), (1,)), ((), ())), preferred_element_type=F32)


def _flush_subnormal(f):
    return jnp.where(jnp.abs(f) < F32_MIN_NORMAL, 0.0, f)


def _pattern_to_f32(c):
    return _flush_subnormal(pltpu.bitcast(jnp.where(c >= 0, c, c ^ 0x7FFFFFFF), F32))


def _tree_sum(xs):
    while len(xs) > 1:
        xs = [a + b for a, b in zip(xs[0::2], xs[1::2])] + ([xs[-1]] if len(xs) % 2 else [])
    return xs[0]


def _pair_loop(n_pairs, chunk_fn, init):
    def body(p, carry):
        return chunk_fn(2 * p + 1, chunk_fn(2 * p, carry))
    return lax.fori_loop(0, n_pairs, body, init)


def _count_above(ref, n_chunks, cand, strict):
    cb = jnp.broadcast_to(cand, (SUBLANES, LANES))
    accs = [jnp.zeros((SUBLANES, LANES), F32)] * COUNT_ACCUMULATORS
    for c in range(n_chunks):
        x = ref[c]
        for j in range(KEY_CHUNK // SUBLANES):
            blk = x[j * SUBLANES:(j + 1) * SUBLANES]
            k = j % COUNT_ACCUMULATORS
            accs[k] = accs[k] + jnp.where((blk > cb) if strict else (blk >= cb), 1.0, 0.0)
    return jnp.sum(_tree_sum(accs), axis=0, keepdims=True)


def _kth_largest(count_ge, need, start, two_bits_per_step=False):
    def step(i, t):
        cand = t + lax.shift_left(jnp.int32(1), 31 - i)
        return jnp.where(count_ge(_pattern_to_f32(cand)) >= need, cand, t)

    def step2(i, t):
        d = lax.shift_left(jnp.int32(1), 30 - 2 * i)
        cands = [t + k * d for k in (1, 2, 3)]
        fits = [count_ge(_pattern_to_f32(c)) >= need for c in cands]
        return jnp.where(fits[2], cands[2], jnp.where(fits[1], cands[1], jnp.where(fits[0], cands[0], t)))

    t = lax.fori_loop(0, 16, step2, start) if two_bits_per_step else lax.fori_loop(0, 32, step, start)
    return jnp.where(t == INT_MIN, -jnp.inf, _pattern_to_f32(t))


def _in_proj_body(x_ref, g_ref, w_ref, q_ref, k_ref, v_ref, ki_ref, kbf_ref, vt_ref, kibf_ref,
                  qi_ref, kiwi_ref, wit_ref, rec_ref, gm_ref, *, transposed_state):
    n_chunks = kbf_ref.shape[0]
    h = _rmsnorm(x_ref[...], g_ref[...]).astype(MXU_DTYPE)

    def proj(lo, hi):
        return _mm(h, w_ref[:, lo:hi])

    q_ref[...] = (proj(C_Q, C_KV) * Q_SCALE).astype(MXU_DTYPE)
    kv = proj(C_KV, C_QI)
    k = kv[:, :D_KV]
    v = kv[:, D_KV:]
    vt = v.T
    kb = k.astype(MXU_DTYPE)
    qi_ref[...] = proj(C_QI, C_REC).astype(MXU_DTYPE)
    rec_ref[...] = proj(C_REC, C_GM)
    gm_ref[...] = proj(C_GM, C_KIWI)
    kiwi = proj(C_KIWI, D_IN_PAD)
    ki = kiwi[:, :IDX_DIM]
    kiwi_ref[...] = kiwi
    kib = ki.astype(MXU_DTYPE)
    kiwi_t = kiwi.T
    wit_ref[...] = kiwi_t[IDX_DIM:IDX_DIM + SUBLANES, :]
    if transposed_state:
        k_ref[0] = k.T
        v_ref[0] = vt
        ki_ref[0] = kiwi_t[:IDX_DIM, :]
    else:
        k_ref[...] = k
        v_ref[...] = v
        ki_ref[...] = ki
    for c in range(n_chunks):
        rows = slice(c * KEY_CHUNK, (c + 1) * KEY_CHUNK)
        kbf_ref[c] = kb[rows]
        kibf_ref[c] = kib[rows]
        vt_ref[c] = vt[:, rows].astype(MXU_DTYPE)


def _in_proj(x, g, w, tm, seq=None):
    m = x.shape[0]
    nc = tm // KEY_CHUNK
    row = lambda n: pl.BlockSpec((tm, n), lambda i: (i, 0))
    chunked = lambda a, b: pl.BlockSpec((nc, a, b), lambda i: (i, 0, 0))
    if seq is None:
        state_shape = lambda n: jax.ShapeDtypeStruct((m, n), F32)
        state_spec = row
    else:
        tps = seq // tm
        state_shape = lambda n: jax.ShapeDtypeStruct((m // seq, n, seq), F32)
        state_spec = lambda n: pl.BlockSpec((1, n, tm), lambda i: (i // tps, 0, i % tps))
    out_shape = (
        jax.ShapeDtypeStruct((m, D_ATT), MXU_DTYPE),
        state_shape(D_KV),
        state_shape(D_KV),
        state_shape(IDX_DIM),
        jax.ShapeDtypeStruct((m // KEY_CHUNK, KEY_CHUNK, D_KV), MXU_DTYPE),
        jax.ShapeDtypeStruct((m // KEY_CHUNK, D_KV, KEY_CHUNK), MXU_DTYPE),
        jax.ShapeDtypeStruct((m // KEY_CHUNK, KEY_CHUNK, IDX_DIM), MXU_DTYPE),
        jax.ShapeDtypeStruct((m, D_QI), MXU_DTYPE),
        jax.ShapeDtypeStruct((m, LANES), F32),
        jax.ShapeDtypeStruct((SUBLANES, m), F32),
        jax.ShapeDtypeStruct((m, 2 * D_REC), F32),
        jax.ShapeDtypeStruct((m, 2 * D_GM), F32),
    )
    out_specs = (
        row(D_ATT), state_spec(D_KV), state_spec(D_KV), state_spec(IDX_DIM),
        chunked(KEY_CHUNK, D_KV), chunked(D_KV, KEY_CHUNK), chunked(KEY_CHUNK, IDX_DIM),
        row(D_QI), row(LANES), pl.BlockSpec((SUBLANES, tm), lambda i: (0, i)),
        row(2 * D_REC), row(2 * D_GM),
    )
    return pl.pallas_call(
        functools.partial(_in_proj_body, transposed_state=seq is not None),
        grid=(m // tm,),
        in_specs=[row(D_MODEL), _const_spec((1, D_MODEL)), _const_spec((D_MODEL, D_IN_PAD))],
        out_specs=out_specs,
        out_shape=out_shape,
        compiler_params=_cparams("arbitrary"),
        name="in_proj",
    )(x, g, w)


def _bias_table(rb_ref, head, delta, shape):
    row = lax.broadcasted_iota(I32, shape, 0)
    lane = lax.broadcasted_iota(I32, shape, 1)
    d = delta + lane - row
    far = rb_ref[NUM_BUCKETS - 1, head]
    val = jnp.full(shape, rb_ref[0, head] - far, F32)
    for b in range(1, NUM_BUCKETS - 1):
        val = jnp.where(d >= BUCKET_LO[b], rb_ref[b, head] - far, val)
    return jnp.where(d >= BUCKET_LO[NUM_BUCKETS - 1], 0.0, val)


def _prompt_attn_body(rb_ref, q_ref, qi_ref, wit_ref, ki_ref, k_ref, vt_ref, att_ref,
                      s_s, mb_s, tab_s, tri_s, thr_s, need_s, m_s, acc_s, qpad_s, *, topk):
    b = pl.program_id(0)
    i = pl.program_id(1)
    n_pairs = i // 4 + 1
    ck = (KEY_CHUNK, Q_BLOCK)

    @pl.when((b == 0) & (i == 0))
    def _init_tables():
        r = lax.broadcasted_iota(I32, (KEY_CHUNK, KEY_CHUNK), 0)
        c = lax.broadcasted_iota(I32, (KEY_CHUNK, KEY_CHUNK), 1)
        tri_s[...] = jnp.where(c <= r, 1.0, 0.0).astype(MXU_DTYPE)

        def per_head(h, carry):
            for ti in range(4):
                tab_s[ti, h] = _bias_table(rb_ref, h, ti * Q_BLOCK, ck)
            return carry

        lax.fori_loop(0, N_HEADS, per_head, 0)

    row = lax.broadcasted_iota(I32, ck, 0)
    lane = lax.broadcasted_iota(I32, ck, 1)
    q_pos = i * Q_BLOCK + lane

    qi = qi_ref[...]
    qis = jnp.concatenate([qi[:, h * IDX_DIM:(h + 1) * IDX_DIM] for h in range(N_IDX_HEADS)], axis=0)
    wit = wit_ref[...]
    w_row = jnp.concatenate([wit[h:h + 1, :] for h in range(N_IDX_HEADS)], axis=1)

    def score_chunk(c, carry):
        s4 = jnp.maximum(_mm_nt(ki_ref[c], qis), 0.0) * w_row
        s = _tree_sum([s4[:, h * Q_BLOCK:(h + 1) * Q_BLOCK] for h in range(N_IDX_HEADS)])
        valid = (c * KEY_CHUNK + row) <= q_pos
        s_s[c] = jnp.where(valid, s, -jnp.inf)
        return carry

    _pair_loop(n_pairs, score_chunk, 0)

    def search(n_chunks):
        t = _kth_largest(lambda cand: _count_above(s_s, n_chunks, cand, False), topk,
                         jnp.full((1, Q_BLOCK), INT_MIN, I32))
        thr_s[0:1, :] = t
        need_s[0:1, :] = topk - _count_above(s_s, n_chunks, t, True)

    n_chunks_causal = (i + 2) // 2
    for n in range(1, s_s.shape[0] + 1):
        pl.when(n_chunks_causal == n)(functools.partial(search, n))
    t = thr_s[0:1, :]
    need = need_s[0:1, :]

    tri = tri_s[...]

    def mask_chunk(c, carry):
        sc = s_s[c]
        tie = (sc == t) & (sc > -jnp.inf)
        tie_f = jnp.where(tie, 1.0, 0.0)
        incl = _mm(tri, tie_f.astype(MXU_DTYPE))
        rank = carry + incl - tie_f
        sel = (sc > t) | (tie & (rank < need))
        mb_s[c] = jnp.where(sel, 0.0, -jnp.inf).astype(mb_s.dtype)
        return carry + incl[KEY_CHUNK - 1:KEY_CHUNK, :]

    _pair_loop(n_pairs, mask_chunk, jnp.zeros((1, Q_BLOCK), F32))

    q = q_ref[...]
    zeros = jnp.zeros((Q_BLOCK, HEAD_DIM), MXU_DTYPE)
    for h in range(N_HEADS):
        qh = q[:, h * HEAD_DIM:(h + 1) * HEAD_DIM]
        parts = [qh, zeros] if h < GROUP else [zeros, qh]
        qpad_s[h * Q_BLOCK:(h + 1) * Q_BLOCK, :] = jnp.concatenate(parts, axis=1)
    m_s[...] = jnp.full(m_s.shape, NEG_BIG, F32)
    acc_s[...] = jnp.zeros(acc_s.shape, F32)
    ones_rows = jnp.ones((PACKED_ROWS, 2 * KEY_CHUNK), MXU_DTYPE)

    def attend_pair(p, near):
        c0, c1 = 2 * p, 2 * p + 1
        keys = jnp.concatenate([k_ref[c0], k_ref[c1]], axis=0)
        logits = _mm_nt(keys, qpad_s[...])
        mb = jnp.concatenate([mb_s[c0], mb_s[c1]], axis=0)
        if near:
            t0 = jnp.clip(i - 2 * c0, 0, 3)
            t1 = jnp.clip(i - 2 * c1, 0, 3)
        for g in range(N_KV_HEADS):
            ps, alphas = [], []
            for hh in range(GROUP):
                h = g * GROUP + hh
                cols = slice(h * Q_BLOCK, (h + 1) * Q_BLOCK)
                lt = logits[:, cols]
                if near:
                    lt = lt + jnp.concatenate([tab_s[t0, h], tab_s[t1, h]], axis=0)
                lt = lt.astype(MXU_DTYPE) + mb
                m_old = m_s[0:1, cols]
                m_new = jnp.maximum(m_old, jnp.max(lt, axis=0, keepdims=True).astype(F32))
                alphas.append(jnp.exp(m_old - m_new))
                ps.append(jnp.exp(lt - m_new.astype(MXU_DTYPE)))
                m_s[0:1, cols] = m_new
            pg = jnp.concatenate(ps, axis=1)
            ag = jnp.concatenate(alphas, axis=1)
            rows = slice(g * HEAD_DIM, (g + 1) * HEAD_DIM)
            vt = jnp.concatenate([jnp.concatenate([vt_ref[c0, rows, :], vt_ref[c1, rows, :]], axis=1), ones_rows],
                                 axis=0)
            acc_s[g] = acc_s[g] * ag + _mm(vt, pg)

    def far_pair(p, carry):
        attend_pair(2 * p, False)
        attend_pair(2 * p + 1, False)
        return carry

    n_far = jnp.maximum(n_pairs - 2, 0)
    lax.fori_loop(0, n_far // 2, far_pair, 0)

    @pl.when(n_far % 2 == 1)
    def _odd_far_pair():
        attend_pair(n_far - 1, False)

    @pl.when(n_pairs >= 2)
    def _near_pairs():
        attend_pair(n_pairs - 2, True)
        attend_pair(n_pairs - 1, True)

    @pl.when(n_pairs == 1)
    def _only_pair():
        attend_pair(0, True)

    for h in range(N_HEADS):
        g, hh = divmod(h, GROUP)
        cols = slice(hh * Q_BLOCK, (hh + 1) * Q_BLOCK)
        o = acc_s[g][:HEAD_DIM, cols] / acc_s[g][HEAD_DIM:HEAD_DIM + 1, cols]
        att_ref[:, h * HEAD_DIM:(h + 1) * HEAD_DIM] = o.T.astype(att_ref.dtype)


def _prompt_attention(rel_bias, q, qi, wit, kibf, kbf, vtbf, batch, seq):
    nq = seq // Q_BLOCK
    nc = seq // KEY_CHUNK
    topk = float(min(TOPK_MAX, seq // 4))
    qrow = lambda n: pl.BlockSpec((Q_BLOCK, n), lambda b, i: (b * nq + i, 0))
    per_batch = lambda a, c: pl.BlockSpec((nc, a, c), lambda b, i: (b, 0, 0))
    return pl.pallas_call(
        functools.partial(_prompt_attn_body, topk=topk),
        grid=(batch, nq),
        in_specs=[
            pl.BlockSpec(memory_space=pltpu.SMEM),
            qrow(D_ATT), qrow(D_QI),
            pl.BlockSpec((SUBLANES, Q_BLOCK), lambda b, i: (0, b * nq + i)),
            per_batch(KEY_CHUNK, IDX_DIM), per_batch(KEY_CHUNK, D_KV), per_batch(D_KV, KEY_CHUNK),
        ],
        out_specs=qrow(D_ATT),
        out_shape=jax.ShapeDtypeStruct((batch * seq, D_ATT), MXU_DTYPE),
        scratch_shapes=[
            pltpu.VMEM((nc, KEY_CHUNK, Q_BLOCK), F32),
            pltpu.VMEM((nc, KEY_CHUNK, Q_BLOCK), MXU_DTYPE),
            pltpu.VMEM((4, N_HEADS, KEY_CHUNK, Q_BLOCK), F32),
            pltpu.VMEM((KEY_CHUNK, KEY_CHUNK), MXU_DTYPE),
            pltpu.VMEM((SUBLANES, Q_BLOCK), F32),
            pltpu.VMEM((SUBLANES, Q_BLOCK), F32),
            pltpu.VMEM((SUBLANES, N_HEADS * Q_BLOCK), F32),
            pltpu.VMEM((N_KV_HEADS, HEAD_DIM + PACKED_ROWS, GROUP * Q_BLOCK), F32),
            pltpu.VMEM((N_HEADS * Q_BLOCK, D_KV), MXU_DTYPE),
        ],
        compiler_params=_cparams("arbitrary", "arbitrary"),
        name="prompt_attn",
    )(rel_bias, q, qi, wit, kibf, kbf, vtbf)


def _sample_attn_body(pt_ref, rb_ref, q_ref, qi_ref, kiwi_ref, knew_ref, vnew_ref,
                      ck_hbm, cv_hbm, cki_hbm, att_ref,
                      ktbuf, vtbuf, kitbuf, sem, *, topk, n_pages, layer, t_new):
    b = pl.program_id(0)
    past = n_pages * PAGE_SIZE
    slot = b % 2

    def page_copies(seq, buf, p):
        phys = pt_ref[seq, p]
        cols = pl.ds(pl.multiple_of(p * PAGE_SIZE, PAGE_SIZE), PAGE_SIZE)
        return (pltpu.make_async_copy(ck_hbm.at[layer, phys], ktbuf.at[buf, :, cols], sem.at[0, buf]),
                pltpu.make_async_copy(cv_hbm.at[layer, phys], vtbuf.at[buf, :, cols], sem.at[1, buf]),
                pltpu.make_async_copy(cki_hbm.at[layer, phys], kitbuf.at[buf, :, cols], sem.at[2, buf]))

    def start_pages(seq, buf):
        def body(p, carry):
            for cp in page_copies(seq, buf, p):
                cp.start()
            return carry
        lax.fori_loop(0, n_pages, body, 0)

    def wait_pages(seq, buf):
        def body(p, carry):
            for cp in page_copies(seq, buf, p):
                cp.wait()
            return carry
        lax.fori_loop(0, n_pages, body, 0)

    @pl.when(b == 0)
    def _first_fetch():
        start_pages(0, 0)

    @pl.when(b + 1 < pl.num_programs(0))
    def _prefetch_next():
        start_pages(b + 1, 1 - slot)

    qi = qi_ref[...]
    qis = jnp.concatenate([qi[:, h * IDX_DIM:(h + 1) * IDX_DIM] for h in range(N_IDX_HEADS)], axis=0)
    kiwi = kiwi_ref[...]
    w_col = jnp.concatenate([kiwi[:, IDX_DIM + h:IDX_DIM + h + 1] for h in range(N_IDX_HEADS)], axis=0)
    q = q_ref[...]
    zeros = jnp.zeros((t_new, HEAD_DIM), MXU_DTYPE)
    qpad = jnp.concatenate(
        [jnp.concatenate([q[:, h * HEAD_DIM:(h + 1) * HEAD_DIM], zeros] if h < GROUP else
                         [zeros, q[:, h * HEAD_DIM:(h + 1) * HEAD_DIM]], axis=1)
         for h in range(N_HEADS)], axis=0)

    wait_pages(b, slot)

    def idx_score(qk):
        s4 = jnp.maximum(qk, 0.0) * w_col
        s = s4[0:t_new]
        for h in range(1, N_IDX_HEADS):
            s = s + s4[h * t_new:(h + 1) * t_new]
        return s

    s_p = idx_score(_mm(qis, kitbuf[slot].astype(MXU_DTYPE)))
    ki_new = kiwi[:, :IDX_DIM].astype(MXU_DTYPE)
    ki_new = jnp.concatenate([ki_new, jnp.zeros((LANES - t_new, IDX_DIM), MXU_DTYPE)], axis=0)
    rown = lax.broadcasted_iota(I32, (t_new, LANES), 0)
    lanen = lax.broadcasted_iota(I32, (t_new, LANES), 1)
    valid_n = lanen <= rown
    s_n = jnp.where(valid_n, idx_score(_mm_nt(qis, ki_new)), -jnp.inf)

    def count(cand, strict):
        cmp = (lambda x: x > cand) if strict else (lambda x: x >= cand)
        return (jnp.sum(jnp.where(cmp(s_p), 1.0, 0.0), axis=1, keepdims=True)
                + jnp.sum(jnp.where(cmp(s_n), 1.0, 0.0), axis=1, keepdims=True))

    t = _kth_largest(lambda cand: count(cand, False), topk, jnp.full((t_new, 1), INT_MIN, I32),
                     two_bits_per_step=True)
    need = topk - count(t, True)

    r = lax.broadcasted_iota(I32, (KEY_CHUNK, KEY_CHUNK), 0)
    c = lax.broadcasted_iota(I32, (KEY_CHUNK, KEY_CHUNK), 1)
    tri = jnp.where(r <= c, 1.0, 0.0).astype(MXU_DTYPE)
    tie_p = s_p == t
    carry = jnp.zeros((t_new, 1), F32)
    mask_parts = []
    for ch in range(past // KEY_CHUNK):
        cols = slice(ch * KEY_CHUNK, (ch + 1) * KEY_CHUNK)
        tie_c = tie_p[:, cols]
        tie_f = jnp.where(tie_c, 1.0, 0.0)
        incl = _mm(tie_f.astype(MXU_DTYPE), tri)
        sel = (s_p[:, cols] > t) | (tie_c & ((carry + incl - tie_f) < need))
        mask_parts.append(jnp.where(sel, 0.0, -jnp.inf))
        carry = carry + incl[:, KEY_CHUNK - 1:KEY_CHUNK]
    mb_p = jnp.concatenate(mask_parts, axis=1)
    tie_n = (s_n == t) & valid_n
    tie_nf = jnp.where(tie_n, 1.0, 0.0)
    incl_n = _mm(tie_nf.astype(MXU_DTYPE), tri[:LANES, :LANES])
    sel_n = (s_n > t) | (tie_n & ((carry + incl_n - tie_nf) < need))
    mb_n = jnp.where(sel_n & valid_n, 0.0, -jnp.inf)

    lg_p = _mm(qpad, ktbuf[slot].astype(MXU_DTYPE))
    k_new = jnp.concatenate([knew_ref[...].astype(MXU_DTYPE),
                             jnp.zeros((LANES - t_new, D_KV), MXU_DTYPE)], axis=0)
    lg_n = _mm_nt(qpad, k_new)
    near = slice(past - LANES, past)
    lp_rows, ln_rows = [], []
    for h in range(N_HEADS):
        rows = slice(h * t_new, (h + 1) * t_new)
        far = rb_ref[NUM_BUCKETS - 1, h]

        def bias(d):
            val = jnp.full(d.shape, rb_ref[0, h] - far, F32)
            for bk in range(1, NUM_BUCKETS - 1):
                val = jnp.where(d >= BUCKET_LO[bk], rb_ref[bk, h] - far, val)
            return jnp.where(d >= BUCKET_LO[NUM_BUCKETS - 1], 0.0, val)

        lp = lg_p[rows] + mb_p
        lp_near = lp[:, near] + bias(rown + (LANES - lanen))
        lp_rows.append(jnp.concatenate([lp[:, :past - LANES], lp_near], axis=1))
        ln_rows.append(lg_n[rows] + mb_n + bias(rown - lanen))
    lp_all = jnp.concatenate(lp_rows, axis=0)
    ln_all = jnp.concatenate(ln_rows, axis=0)
    m = jnp.maximum(jnp.max(lp_all, axis=1, keepdims=True), jnp.max(ln_all, axis=1, keepdims=True))
    p_p = jnp.exp(lp_all - m)
    p_n = jnp.exp(ln_all - m)
    denom = jnp.sum(p_p, axis=1, keepdims=True) + jnp.sum(p_n, axis=1, keepdims=True)
    v_new = jnp.concatenate([vnew_ref[...].astype(MXU_DTYPE),
                             jnp.zeros((LANES - t_new, D_KV), MXU_DTYPE)], axis=0)
    o = (_mm_nt(p_p.astype(MXU_DTYPE), vtbuf[slot].astype(MXU_DTYPE))
         + _mm(p_n.astype(MXU_DTYPE), v_new)) / denom
    for h in range(N_HEADS):
        g = h // GROUP
        att_ref[:, h * HEAD_DIM:(h + 1) * HEAD_DIM] = (
            o[h * t_new:(h + 1) * t_new, g * HEAD_DIM:(g + 1) * HEAD_DIM].astype(att_ref.dtype))


def _sample_attention(page_table, rel_bias, q, qi, kiwi, k_new, v_new, cache_k, cache_v, cache_kidx,
                      layer, batch, t_new):
    n_pages = page_table.shape[1]
    past = n_pages * PAGE_SIZE
    topk = float(min(TOPK_MAX, (past + t_new) // 4))
    depth, n_pool = cache_k.shape[:2]
    ck = cache_k.transpose(0, 1, 3, 4, 2).reshape(depth, n_pool, D_KV, PAGE_SIZE)
    cv = cache_v.transpose(0, 1, 3, 4, 2).reshape(depth, n_pool, D_KV, PAGE_SIZE)
    cki = cache_kidx.transpose(0, 1, 3, 2)
    qrow = lambda n: pl.BlockSpec((t_new, n), lambda b, pt: (b, 0))
    hbm = pl.BlockSpec(memory_space=pl.ANY)
    grid_spec = pltpu.PrefetchScalarGridSpec(
        num_scalar_prefetch=1,
        grid=(batch,),
        in_specs=[pl.BlockSpec(memory_space=pltpu.SMEM),
                  qrow(D_ATT), qrow(D_QI), qrow(LANES), qrow(D_KV), qrow(D_KV), hbm, hbm, hbm],
        out_specs=qrow(D_ATT),
        scratch_shapes=[
            pltpu.VMEM((2, D_KV, past), F32),
            pltpu.VMEM((2, D_KV, past), F32),
            pltpu.VMEM((2, IDX_DIM, past), F32),
            pltpu.SemaphoreType.DMA((3, 2)),
        ],
    )
    return pl.pallas_call(
        functools.partial(_sample_attn_body, topk=topk, n_pages=n_pages, layer=layer, t_new=t_new),
        grid_spec=grid_spec,
        out_shape=jax.ShapeDtypeStruct((batch * t_new, D_ATT), MXU_DTYPE),
        compiler_params=_cparams("arbitrary"),
        name="sample_attn",
    )(page_table, rel_bias, q, qi, kiwi, k_new, v_new, ck, cv, cki)


def _rec_body(rec_ref, hist_ref, h0_ref, cw_ref, cb_ref, wa_ref, ba_ref, wx_ref, bx_ref, lam_ref,
              out_ref, hlast_ref, cstate_ref, hcar, xprev):
    tt = rec_ref.shape[0]

    @pl.when(pl.program_id(1) == 0)
    def _load_state():
        hcar[...] = h0_ref[0]
        xprev[...] = hist_ref[0]

    rx = rec_ref[:, :D_REC]
    rg = rec_ref[:, D_REC:]
    ext = jnp.concatenate([xprev[...], rx], axis=0)
    xc = cb_ref[...]
    for j in range(REC_CONV_W - 1):
        xc = xc + pltpu.roll(ext, REC_CONV_W - 1 - j, 0)[SUBLANES:] * cw_ref[j:j + 1, :]
    xc = xc + rx * cw_ref[REC_CONV_W - 1:REC_CONV_W, :]
    xcb = xc.astype(MXU_DTYPE)
    r = _sigmoid(_mm(xcb, wa_ref[...]) + ba_ref[...])
    gi = _sigmoid(_mm(xcb, wx_ref[...]) + bx_ref[...])
    nl = -lam_ref[...]
    softplus = jnp.maximum(nl, 0.0) + jnp.log1p(jnp.exp(-jnp.abs(nl)))
    log_a = -LRU_C * r * softplus
    a = jnp.exp(log_a)
    u = jnp.sqrt(-jnp.tanh(log_a) * (a * a + 1.0)) * (gi * xc)
    row = lax.broadcasted_iota(I32, (tt, D_REC), 0)
    s = 1
    while s < tt:
        keep = row >= s
        u = jnp.where(keep, u + a * pltpu.roll(u, s, 0), u)
        a = jnp.where(keep, a * pltpu.roll(a, s, 0), a)
        s *= 2
    hs = u + a * hcar[0:1, :]
    hcar[...] = jnp.broadcast_to(hs[tt - 1:tt, :], hcar.shape)
    xprev[...] = rx[tt - SUBLANES:, :]
    out_ref[...] = (_gelu(rg) * hs).astype(out_ref.dtype)
    hlast_ref[0] = hs[tt - SUBLANES:, :]
    cstate_ref[0] = rx[tt - SUBLANES:, :]


def _rec_branch(rec, hist8, h08, cw, cb, wa, ba, wx, bx, lam, batch, t_len, tt):
    nt = t_len // tt
    state = pl.BlockSpec((1, SUBLANES, D_REC), lambda b, t: (b, 0, 0))
    vec = _const_spec((1, D_REC))
    return pl.pallas_call(
        _rec_body,
        grid=(batch, nt),
        in_specs=[pl.BlockSpec((tt, 2 * D_REC), lambda b, t: (b * nt + t, 0)), state, state,
                  _const_spec((REC_CONV_W, D_REC)), vec, _const_spec((D_REC, D_REC)), vec,
                  _const_spec((D_REC, D_REC)), vec, vec],
        out_specs=(pl.BlockSpec((tt, D_REC), lambda b, t: (b * nt + t, 0)), state, state),
        out_shape=(jax.ShapeDtypeStruct((batch * t_len, D_REC), MXU_DTYPE),
                   jax.ShapeDtypeStruct((batch, SUBLANES, D_REC), F32),
                   jax.ShapeDtypeStruct((batch, SUBLANES, D_REC), F32)),
        scratch_shapes=[pltpu.VMEM((SUBLANES, D_REC), F32), pltpu.VMEM((SUBLANES, D_REC), F32)],
        compiler_params=_cparams("arbitrary", "arbitrary"),
        name="rec_branch",
    )(rec, hist8, h08, cw, cb, wa, ba, wx, bx, lam)


def _gmlp_body(gm_ref, wmix_ref, bias_ref, lng_ref, lnb_ref, out_ref, vn_ref):
    rows = wmix_ref.shape[1]
    n_sub = gm_ref.shape[0] // rows
    head_of_lane = lax.broadcasted_iota(I32, (rows, D_GM), 1) // GM_HEAD_DIM
    for sb in range(n_sub):
        sl = slice(sb * rows, (sb + 1) * rows)
        u = _gelu(gm_ref[sl, :D_GM])
        gv = _gelu(gm_ref[sl, D_GM:])
        xc = gv - jnp.mean(gv, axis=-1, keepdims=True)
        var = jnp.mean(xc * xc, axis=-1, keepdims=True)
        vn = xc * lax.rsqrt(var + EPS) * lng_ref[...] + lnb_ref[...]
        vn_ref[sl, :] = vn
        vnb = vn.astype(MXU_DTYPE)
        mix = jnp.zeros((rows, D_GM), F32)
        for g in range(N_GM_HEADS):
            mix = jnp.where(head_of_lane == g, _mm(wmix_ref[g], vnb), mix)
        out_ref[sl, :] = (u * (mix + bias_ref[...])).astype(out_ref.dtype)


def _gmlp(gm, wmix, bias, lng, lnb, tm):
    m = gm.shape[0]
    rows = wmix.shape[1]
    return pl.pallas_call(
        _gmlp_body,
        grid=(m // tm,),
        in_specs=[pl.BlockSpec((tm, 2 * D_GM), lambda i: (i, 0)),
                  _const_spec((N_GM_HEADS, rows, rows)), _const_spec((rows, D_GM)),
                  _const_spec((1, D_GM)), _const_spec((1, D_GM))],
        out_specs=(pl.BlockSpec((tm, D_GM), lambda i: (i, 0)), pl.BlockSpec((tm, D_GM), lambda i: (i, 0))),
        out_shape=(jax.ShapeDtypeStruct((m, D_GM), MXU_DTYPE), jax.ShapeDtypeStruct((m, D_GM), F32)),
        compiler_params=_cparams("arbitrary"),
        name="gmlp",
    )(gm, wmix, bias, lng, lnb)


def _out_proj_body(x_ref, att_ref, rec_ref, gm_ref, w_ref, g_ref, x1_ref, h2_ref):
    y = (_mm(att_ref[...], w_ref[0:D_ATT, :])
         + _mm(rec_ref[...], w_ref[D_ATT:D_ATT + D_REC, :])
         + _mm(gm_ref[...], w_ref[D_ATT + D_REC:, :]))
    x1 = x_ref[...] + y
    x1_ref[...] = x1
    h2_ref[...] = _rmsnorm(x1, g_ref[...]).astype(h2_ref.dtype)


def _out_proj(x, att, rec, gm, w, g, tm):
    m = x.shape[0]
    row = lambda n: pl.BlockSpec((tm, n), lambda i: (i, 0))
    return pl.pallas_call(
        _out_proj_body,
        grid=(m // tm,),
        in_specs=[row(D_MODEL), row(D_ATT), row(D_REC), row(D_GM),
                  _const_spec((D_MODEL, D_MODEL)), _const_spec((1, D_MODEL))],
        out_specs=(row(D_MODEL), row(D_MODEL)),
        out_shape=(jax.ShapeDtypeStruct((m, D_MODEL), F32), jax.ShapeDtypeStruct((m, D_MODEL), MXU_DTYPE)),
        compiler_params=_cparams("arbitrary"),
        name="out_proj",
    )(x, att, rec, gm, w, g)


FF_TILE = 256
HALO = 16


def _ffn_finish(x1, acc, gf_ref, out_ref, final):
    x2 = x1 + acc
    out_ref[...] = _rmsnorm(x2, gf_ref[...]) if final else x2


def _ffn_prompt_body(x_ref, xh_ref, att_ref, atth_ref, rec_ref, rech_ref, gm_ref, gmh_ref, wo_ref, g2_ref,
                     hist_ref, wup_ref, cw_ref, cb_ref, wdn_ref, gf_ref,
                     out_ref, upst_ref, act_s, *, tiles_per_seq, final):
    tm = x_ref.shape[0]
    first = (pl.program_id(0) % tiles_per_seq) == 0
    ext = lambda halo, main: jnp.concatenate([halo[...], main[...]], axis=0)
    x1_ext = ext(xh_ref, x_ref) + (_mm(ext(atth_ref, att_ref), wo_ref[0:D_ATT, :])
                                   + _mm(ext(rech_ref, rec_ref), wo_ref[D_ATT:D_ATT + D_REC, :])
                                   + _mm(ext(gmh_ref, gm_ref), wo_ref[D_ATT + D_REC:, :]))
    h_ext = _rmsnorm(x1_ext, g2_ref[...]).astype(MXU_DTYPE)
    for j in range(D_FF // FF_TILE):
        halves = []
        for base in (0, D_FF):
            cols = slice(base + j * FF_TILE, base + (j + 1) * FF_TILE)
            up = _mm(h_ext, wup_ref[:, cols])
            up_m = up[HALO:]
            upst_ref[0, :, cols] = up_m[tm - HALO:, :]
            ext = jnp.concatenate([jnp.where(first, hist_ref[0, :, cols], up[:HALO]), up_m], axis=0)
            uc = cb_ref[:, cols]
            for jj in range(FFN_CONV_W - 1):
                uc = uc + pltpu.roll(ext, FFN_CONV_W - 1 - jj, 0)[HALO:] * cw_ref[jj:jj + 1, cols]
            halves.append(uc + up_m * cw_ref[FFN_CONV_W - 1:FFN_CONV_W, cols])
        act_s[:, j * FF_TILE:(j + 1) * FF_TILE] = (_gelu(halves[0]) * halves[1]).astype(MXU_DTYPE)
    _ffn_finish(x1_ext[HALO:], _mm(act_s[...], wdn_ref[...]), gf_ref, out_ref, final)


def _ffn_prompt(x, att, rec, gm, wo, g2, hist16, wup, cw, cb, wdn, gf, batch, seq, tm, final):
    m = x.shape[0]
    tps = seq // tm
    row = lambda n: pl.BlockSpec((tm, n), lambda i: (i, 0))
    halo = lambda n: pl.BlockSpec((HALO, n), lambda i: (jnp.maximum(i * (tm // HALO) - 1, 0), 0))
    state = pl.BlockSpec((1, HALO, 2 * D_FF), lambda i: (i // tps, 0, 0))
    return pl.pallas_call(
        functools.partial(_ffn_prompt_body, tiles_per_seq=tps, final=final),
        grid=(m // tm,),
        in_specs=[row(D_MODEL), halo(D_MODEL), row(D_ATT), halo(D_ATT), row(D_REC), halo(D_REC),
                  row(D_GM), halo(D_GM), _const_spec((D_MODEL, D_MODEL)), _const_spec((1, D_MODEL)), state,
                  _const_spec((D_MODEL, 2 * D_FF)), _const_spec((FFN_CONV_W, 2 * D_FF)),
                  _const_spec((1, 2 * D_FF)), _const_spec((D_FF, D_MODEL)), _const_spec((1, D_MODEL))],
        out_specs=(row(D_MODEL), state),
        out_shape=(jax.ShapeDtypeStruct((m, D_MODEL), F32),
                   jax.ShapeDtypeStruct((batch, HALO, 2 * D_FF), F32)),
        scratch_shapes=[pltpu.VMEM((tm, D_FF), MXU_DTYPE)],
        compiler_params=_cparams("arbitrary"),
        name="ffn_prompt",
    )(x, x, att, att, rec, rec, gm, gm, wo, g2, hist16, wup, cw, cb, wdn, gf)


def _ffn_sample_body(h_ref, x1_ref, p1_ref, p2_ref, wup_ref, cw_ref, cb_ref, wdn_ref, gf_ref,
                     out_ref, up_ref, *, t_new, final):
    tm = h_ref.shape[0]
    hm = h_ref[...]
    pos = lax.broadcasted_iota(I32, (tm, FF_TILE), 0) % t_new
    acc = jnp.zeros((tm, D_MODEL), F32)
    for j in range(D_FF // FF_TILE):
        halves = []
        for base in (0, D_FF):
            cols = slice(base + j * FF_TILE, base + (j + 1) * FF_TILE)
            up = _mm(hm, wup_ref[:, cols])
            up_ref[:, cols] = up
            tap2 = jnp.where(pos < 2, p2_ref[:, cols], pltpu.roll(up, 2, 0))
            tap1 = jnp.where(pos < 1, p1_ref[:, cols], pltpu.roll(up, 1, 0))
            halves.append(cb_ref[:, cols] + tap2 * cw_ref[0:1, cols] + tap1 * cw_ref[1:2, cols]
                          + up * cw_ref[2:3, cols])
        act = (_gelu(halves[0]) * halves[1]).astype(MXU_DTYPE)
        acc = acc + _mm(act, wdn_ref[j * FF_TILE:(j + 1) * FF_TILE, :])
    _ffn_finish(x1_ref[...], acc, gf_ref, out_ref, final)


def _ffn_sample(h2, x1, p1, p2, wup, cw, cb, wdn, gf, t_new, final):
    m = h2.shape[0]
    full = lambda n: _const_spec((m, n))
    return pl.pallas_call(
        functools.partial(_ffn_sample_body, t_new=t_new, final=final),
        grid=(1,),
        in_specs=[full(D_MODEL), full(D_MODEL), full(2 * D_FF), full(2 * D_FF),
                  _const_spec((D_MODEL, 2 * D_FF)), _const_spec((FFN_CONV_W, 2 * D_FF)),
                  _const_spec((1, 2 * D_FF)), _const_spec((D_FF, D_MODEL)), _const_spec((1, D_MODEL))],
        out_specs=(full(D_MODEL), full(2 * D_FF)),
        out_shape=(jax.ShapeDtypeStruct((m, D_MODEL), F32), jax.ShapeDtypeStruct((m, 2 * D_FF), F32)),
        compiler_params=_cparams("arbitrary"),
        name="ffn_sample",
    )(h2, x1, p1, p2, wup, cw, cb, wdn, gf)


def _block_diag(w):
    n, blk, _ = w.shape
    out = jnp.zeros((n * blk, n * blk), w.dtype)
    for i in range(n):
        out = out.at[i * blk:(i + 1) * blk, i * blk:(i + 1) * blk].set(w[i])
    return out


def _pad_rows_front(a, rows):
    pad = rows - a.shape[1]
    return jnp.pad(a, ((0, 0), (pad, 0), (0, 0)))


def _layer_weights(l, g_mix, w_in, rec_conv_w, rec_conv_b, lru_wa, lru_ba, lru_wx, lru_bx, lru_lam,
                   gm_ln_g, gm_ln_b, gm_ws, gm_bs, w_out, g_ffn, w_up, ffn_conv_w, ffn_conv_b, w_down):
    w = w_in[l]
    c_ki = D_ATT + 2 * D_KV + D_QI
    c_rx = c_ki + IDX_DIM + N_IDX_HEADS
    w_r = jnp.concatenate([w[:, :c_ki], w[:, c_rx:], w[:, c_ki:c_rx],
                           jnp.zeros((D_MODEL, LANES - IDX_DIM - N_IDX_HEADS), w.dtype)], axis=1)
    tril = jnp.tril(jnp.ones((CHUNK, CHUNK), gm_ws.dtype))
    row2 = lambda a: a.reshape(1, -1)
    return dict(
        g_mix=row2(g_mix[l]), w_in=w_r.astype(MXU_DTYPE),
        rec_cw=rec_conv_w[l], rec_cb=row2(rec_conv_b[l]),
        wa=_block_diag(lru_wa[l]).astype(MXU_DTYPE), ba=row2(lru_ba[l]),
        wx=_block_diag(lru_wx[l]).astype(MXU_DTYPE), bx=row2(lru_bx[l]), lam=row2(lru_lam[l]),
        ln_g=row2(gm_ln_g[l]), ln_b=row2(gm_ln_b[l]), gm_w=gm_ws[l] * tril, gm_b=gm_bs[l],
        w_out=w_out[l].astype(MXU_DTYPE), g_ffn=row2(g_ffn[l]),
        w_up=w_up[l].astype(MXU_DTYPE), ffn_cw=ffn_conv_w[l], ffn_cb=row2(ffn_conv_b[l]),
        w_down=w_down[l].astype(MXU_DTYPE))


def _gmlp_mix_weights(p, c, n_seq):
    w = p['gm_w'][:, :c, :c]
    if n_seq > 1:
        w = jnp.einsum('ab,gts->gatbs', jnp.eye(n_seq, dtype=w.dtype), w).reshape(
            N_GM_HEADS, n_seq * c, n_seq * c)
    bias = jnp.repeat(p['gm_b'][:, :c].T, GM_HEAD_DIM, axis=1)
    return w.astype(MXU_DTYPE), jnp.tile(bias, (n_seq, 1))


def _prompt_layer(x, p, rel_bias, g_final, batch, seq, final):
    tm = 512
    (q, kt, vt, kit, kbf, vtbf, kibf, qi, _, wit, rec, gm) = _in_proj(x, p['g_mix'], p['w_in'], tm, seq)
    att = _prompt_attention(rel_bias, q, qi, wit, kibf, kbf, vtbf, batch, seq)
    zeros8 = jnp.zeros((batch, SUBLANES, D_REC), F32)
    rec_out, hlast, cstate = _rec_branch(rec, zeros8, zeros8, p['rec_cw'], p['rec_cb'], p['wa'], p['ba'],
                                         p['wx'], p['bx'], p['lam'], batch, seq, 256)
    c = min(CHUNK, seq)
    wmix, bias = _gmlp_mix_weights(p, c, 1)
    gm_out, _ = _gmlp(gm, wmix, bias, p['ln_g'], p['ln_b'], tm)
    hist16 = jnp.zeros((batch, HALO, 2 * D_FF), F32)
    x2, upst = _ffn_prompt(x, att, rec_out, gm_out, p['w_out'], p['g_ffn'], hist16, p['w_up'], p['ffn_cw'],
                           p['ffn_cb'], p['w_down'], g_final, batch, seq, tm, final)
    heads_last = lambda a: a.reshape(batch, N_KV_HEADS, HEAD_DIM, seq).transpose(0, 3, 1, 2)
    state = (heads_last(kt), heads_last(vt), kit.transpose(0, 2, 1), hlast[:, SUBLANES - 1, :],
             cstate[:, SUBLANES - (REC_CONV_W - 1):, :], upst[:, HALO - (FFN_CONV_W - 1):, :])
    return x2, state


def _sample_layer(x, p, rel_bias, g_final, l, batch, t_new, final, page_table, cache_k, cache_v, cache_kidx,
                  h0, rec_hist, ffn_hist):
    m = batch * t_new
    (q, k, v, ki, _, _, _, qi, kiwi, _, rec, gm) = _in_proj(x, p['g_mix'], p['w_in'], m)
    att = _sample_attention(page_table, rel_bias, q, qi, kiwi, k, v, cache_k, cache_v, cache_kidx,
                            l, batch, t_new)
    hist8 = _pad_rows_front(rec_hist, SUBLANES)
    h08 = jnp.broadcast_to(h0[:, None, :], (batch, SUBLANES, D_REC))
    rec_out, hlast, cstate = _rec_branch(rec, hist8, h08, p['rec_cw'], p['rec_cb'], p['wa'], p['ba'],
                                         p['wx'], p['bx'], p['lam'], batch, t_new, t_new)
    c = min(CHUNK, t_new)
    wmix, bias = _gmlp_mix_weights(p, c, m // c)
    gm_out, vn = _gmlp(gm, wmix, bias, p['ln_g'], p['ln_b'], m)
    x1, h2 = _out_proj(x, att, rec_out, gm_out, p['w_out'], p['g_ffn'], m)
    zrow = jnp.zeros((batch, t_new - 2, 2 * D_FF), F32)
    p2 = jnp.concatenate([ffn_hist, zrow], axis=1).reshape(m, 2 * D_FF)
    p1 = jnp.concatenate([ffn_hist[:, 1:], zrow, zrow[:, :1]], axis=1).reshape(m, 2 * D_FF)
    x2, up = _ffn_sample(h2, x1, p1, p2, p['w_up'], p['ffn_cw'], p['ffn_cb'], p['w_down'], g_final,
                         t_new, final)
    state = (k.reshape(batch, t_new, N_KV_HEADS, HEAD_DIM), v.reshape(batch, t_new, N_KV_HEADS, HEAD_DIM),
             ki.reshape(batch, t_new, IDX_DIM), hlast[:, SUBLANES - 1, :],
             cstate[:, SUBLANES - (REC_CONV_W - 1):, :],
             up.reshape(batch, t_new, 2 * D_FF)[:, t_new - (FFN_CONV_W - 1):, :],
             vn.reshape(batch, t_new, D_GM))
    return x2, state


def kernel(x_prompt, x_sample, cache_k, cache_v, cache_kidx, page_table, state_lru_h, state_conv_rec,
           state_conv_ffn, rel_bias, g_mix, w_in, rec_conv_w, rec_conv_b, lru_wa, lru_ba, lru_wx, lru_bx,
           lru_lam, gm_ln_g, gm_ln_b, gm_ws, gm_bs, w_out, g_ffn, w_up, ffn_conv_w, ffn_conv_b, w_down,
           g_final):
    batch, seq, _ = x_prompt.shape
    dec_batch, t_new, _ = x_sample.shape
    depth = w_in.shape[0]
    assert seq % 512 == 0 and t_new == SUBLANES and (dec_batch * t_new) % KEY_CHUNK == 0
    xp = x_prompt.reshape(batch * seq, D_MODEL)
    xs = x_sample.reshape(dec_batch * t_new, D_MODEL)
    gf = g_final.reshape(1, D_MODEL)
    p_states, s_states = [], []
    for l in range(depth):
        p = _layer_weights(l, g_mix, w_in, rec_conv_w, rec_conv_b, lru_wa, lru_ba, lru_wx, lru_bx, lru_lam,
                           gm_ln_g, gm_ln_b, gm_ws, gm_bs, w_out, g_ffn, w_up, ffn_conv_w, ffn_conv_b, w_down)
        final = l == depth - 1
        xp, st = _prompt_layer(xp, p, rel_bias, gf, batch, seq, final)
        p_states.append(st)
        xs, st = _sample_layer(xs, p, rel_bias, gf, l, dec_batch, t_new, final, page_table, cache_k, cache_v,
                               cache_kidx, state_lru_h[l], state_conv_rec[l], state_conv_ffn[l])
        s_states.append(st)
    stack = lambda states, i: jnp.stack([s[i] for s in states])
    return ((xp.reshape(batch, seq, D_MODEL), xs.reshape(dec_batch, t_new, D_MODEL))
            + tuple(stack(p_states, i) for i in range(6))
            + tuple(stack(s_states, i) for i in range(7)))
```

```python
import functools
import math

import jax
import jax.numpy as jnp
from jax import lax
from jax.experimental import pallas as pl
from jax.experimental.pallas import tpu as pltpu

F32 = jnp.float32
I32 = jnp.int32
MXU_DTYPE = jnp.bfloat16

D_MODEL = 1024
N_HEADS = 8
HEAD_DIM = 64
D_ATT = N_HEADS * HEAD_DIM
N_KV_HEADS = 2
GROUP = N_HEADS // N_KV_HEADS
D_KV = N_KV_HEADS * HEAD_DIM
N_IDX_HEADS = 4
IDX_DIM = 64
D_QI = N_IDX_HEADS * IDX_DIM
TOPK_MAX = 256
NUM_BUCKETS = 32
MAX_EXACT = NUM_BUCKETS // 2
MAX_DISTANCE = 128
D_REC = 256
N_REC_BLOCKS = 4
REC_CONV_W = 4
LRU_C = 8.0
D_GM = 256
N_GM_HEADS = 4
GM_HEAD_DIM = D_GM // N_GM_HEADS
CHUNK = 128
D_FF = 2816
FFN_CONV_W = 3
EPS = 1e-6
PAGE_SIZE = 128
Q_SCALE = HEAD_DIM ** -0.5

LANES = 128
SUBLANES = 8
VMEM_LIMIT_BYTES = 56 * 1024 * 1024

C_Q = 0
C_KV = C_Q + D_ATT
C_QI = C_KV + 2 * D_KV
C_REC = C_QI + D_QI
C_GM = C_REC + 2 * D_REC
C_KIWI = C_GM + 2 * D_GM
D_IN_PAD = C_KIWI + LANES

PROJ_TILE = 1024
FFN_ROW_TILE = 512
REC_TILE = 256
KEY_CHUNK = 256
Q_BLOCK = 128
INT_MIN = -2 ** 31
F32_MIN_NORMAL = 2.0 ** -126
PACKED_ROWS = 2 * SUBLANES
NEG_BIG = -2.0 ** 100
COUNT_ACCUMULATORS = 8

BUCKET_LO = tuple(
    b if b <= MAX_EXACT else math.ceil(MAX_EXACT * (MAX_DISTANCE / MAX_EXACT) ** ((b - MAX_EXACT) / (NUM_BUCKETS - MAX_EXACT)))
    for b in range(NUM_BUCKETS))


def _cparams(*sem):
    return pltpu.CompilerParams(dimension_semantics=sem, vmem_limit_bytes=VMEM_LIMIT_BYTES)


def _const_spec(shape):
    nd = len(shape)
    return pl.BlockSpec(shape, lambda *_: (0,) * nd, pipeline_mode=pl.Buffered(1))


def _rmsnorm(x, g):
    return x * lax.rsqrt(jnp.mean(x * x, axis=-1, keepdims=True) + EPS) * g


def _gelu(x):
    return 0.5 * x * (1.0 + jnp.tanh(math.sqrt(2.0 / math.pi) * (x + 0.044715 * (x * x * x))))


def _sigmoid(x):
    return 1.0 / (1.0 + jnp.exp(-x))


def _mm(a, b):
    return jnp.dot(a, b, preferred_element_type=F32)


def _mm_nt(a, b):
    return lax.dot_general(a, b, (((1,), (1,)), ((), ())), preferred_element_type=F32)


def _flush_subnormal(f):
    return jnp.where(jnp.abs(f) < F32_MIN_NORMAL, 0.0, f)


def _pattern_to_f32(c):
    return _flush_subnormal(pltpu.bitcast(jnp.where(c >= 0, c, c ^ 0x7FFFFFFF), F32))


def _tree_sum(xs):
    while len(xs) > 1:
        xs = [a + b for a, b in zip(xs[0::2], xs[1::2])] + ([xs[-1]] if len(xs) % 2 else [])
    return xs[0]


def _pair_loop(n_pairs, chunk_fn, init):
    def body(p, carry):
        return chunk_fn(2 * p + 1, chunk_fn(2 * p, carry))
    return lax.fori_loop(0, n_pairs, body, init)


def _count_above(ref, n_chunks, cand, strict):
    cb = jnp.broadcast_to(cand, (SUBLANES, LANES))
    accs = [jnp.zeros((SUBLANES, LANES), F32)] * COUNT_ACCUMULATORS
    for c in range(n_chunks):
        x = ref[c]
        for j in range(KEY_CHUNK // SUBLANES):
            blk = x[j * SUBLANES:(j + 1) * SUBLANES]
            k = j % COUNT_ACCUMULATORS
            accs[k] = accs[k] + jnp.where((blk > cb) if strict else (blk >= cb), 1.0, 0.0)
    return jnp.sum(_tree_sum(accs), axis=0, keepdims=True)


def _kth_largest(count_ge, need, start, two_bits_per_step=False):
    def step(i, t):
        cand = t + lax.shift_left(jnp.int32(1), 31 - i)
        return jnp.where(count_ge(_pattern_to_f32(cand)) >= need, cand, t)

    def step2(i, t):
        d = lax.shift_left(jnp.int32(1), 30 - 2 * i)
        cands = [t + k * d for k in (1, 2, 3)]
        fits = [count_ge(_pattern_to_f32(c)) >= need for c in cands]
        return jnp.where(fits[2], cands[2], jnp.where(fits[1], cands[1], jnp.where(fits[0], cands[0], t)))

    t = lax.fori_loop(0, 16, step2, start) if two_bits_per_step else lax.fori_loop(0, 32, step, start)
    return jnp.where(t == INT_MIN, -jnp.inf, _pattern_to_f32(t))


def _in_proj_body(x_ref, g_ref, w_ref, q_ref, k_ref, v_ref, ki_ref, kbf_ref, vt_ref, kibf_ref,
                  qi_ref, kiwi_ref, wit_ref, rec_ref, gm_ref, *, transposed_state):
    n_chunks = kbf_ref.shape[0]
    h = _rmsnorm(x_ref[...], g_ref[...]).astype(MXU_DTYPE)

    def proj(lo, hi):
        return _mm(h, w_ref[:, lo:hi])

    q_ref[...] = (proj(C_Q, C_KV) * Q_SCALE).astype(MXU_DTYPE)
    kv = proj(C_KV, C_QI)
    k = kv[:, :D_KV]
    v = kv[:, D_KV:]
    vt = v.T
    kb = k.astype(MXU_DTYPE)
    qi_ref[...] = proj(C_QI, C_REC).astype(MXU_DTYPE)
    rec_ref[...] = proj(C_REC, C_GM)
    gm_ref[...] = proj(C_GM, C_KIWI)
    kiwi = proj(C_KIWI, D_IN_PAD)
    ki = kiwi[:, :IDX_DIM]
    kiwi_ref[...] = kiwi
    kib = ki.astype(MXU_DTYPE)
    kiwi_t = kiwi.T
    wit_ref[...] = kiwi_t[IDX_DIM:IDX_DIM + SUBLANES, :]
    if transposed_state:
        k_ref[0] = k.T
        v_ref[0] = vt
        ki_ref[0] = kiwi_t[:IDX_DIM, :]
    else:
        k_ref[...] = k
        v_ref[...] = v
        ki_ref[...] = ki
    for c in range(n_chunks):
        rows = slice(c * KEY_CHUNK, (c + 1) * KEY_CHUNK)
        kbf_ref[c] = kb[rows]
        kibf_ref[c] = kib[rows]
        vt_ref[c] = vt[:, rows].astype(MXU_DTYPE)


def _in_proj(x, g, w, tm, seq=None):
    m = x.shape[0]
    nc = tm // KEY_CHUNK
    row = lambda n: pl.BlockSpec((tm, n), lambda i: (i, 0))
    chunked = lambda a, b: pl.BlockSpec((nc, a, b), lambda i: (i, 0, 0))
    if seq is None:
        state_shape = lambda n: jax.ShapeDtypeStruct((m, n), F32)
        state_spec = row
    else:
        tps = seq // tm
        state_shape = lambda n: jax.ShapeDtypeStruct((m // seq, n, seq), F32)
        state_spec = lambda n: pl.BlockSpec((1, n, tm), lambda i: (i // tps, 0, i % tps))
    out_shape = (
        jax.ShapeDtypeStruct((m, D_ATT), MXU_DTYPE),
        state_shape(D_KV),
        state_shape(D_KV),
        state_shape(IDX_DIM),
        jax.ShapeDtypeStruct((m // KEY_CHUNK, KEY_CHUNK, D_KV), MXU_DTYPE),
        jax.ShapeDtypeStruct((m // KEY_CHUNK, D_KV, KEY_CHUNK), MXU_DTYPE),
        jax.ShapeDtypeStruct((m // KEY_CHUNK, KEY_CHUNK, IDX_DIM), MXU_DTYPE),
        jax.ShapeDtypeStruct((m, D_QI), MXU_DTYPE),
        jax.ShapeDtypeStruct((m, LANES), F32),
        jax.ShapeDtypeStruct((SUBLANES, m), F32),
        jax.ShapeDtypeStruct((m, 2 * D_REC), F32),
        jax.ShapeDtypeStruct((m, 2 * D_GM), F32),
    )
    out_specs = (
        row(D_ATT), state_spec(D_KV), state_spec(D_KV), state_spec(IDX_DIM),
        chunked(KEY_CHUNK, D_KV), chunked(D_KV, KEY_CHUNK), chunked(KEY_CHUNK, IDX_DIM),
        row(D_QI), row(LANES), pl.BlockSpec((SUBLANES, tm), lambda i: (0, i)),
        row(2 * D_REC), row(2 * D_GM),
    )
    return pl.pallas_call(
        functools.partial(_in_proj_body, transposed_state=seq is not None),
        grid=(m // tm,),
        in_specs=[row(D_MODEL), _const_spec((1, D_MODEL)), _const_spec((D_MODEL, D_IN_PAD))],
        out_specs=out_specs,
        out_shape=out_shape,
        compiler_params=_cparams("arbitrary"),
        name="in_proj",
    )(x, g, w)


def _bias_table(rb_ref, head, delta, shape):
    row = lax.broadcasted_iota(I32, shape, 0)
    lane = lax.broadcasted_iota(I32, shape, 1)
    d = delta + lane - row
    far = rb_ref[NUM_BUCKETS - 1, head]
    val = jnp.full(shape, rb_ref[0, head] - far, F32)
    for b in range(1, NUM_BUCKETS - 1):
        val = jnp.where(d >= BUCKET_LO[b], rb_ref[b, head] - far, val)
    return jnp.where(d >= BUCKET_LO[NUM_BUCKETS - 1], 0.0, val)


def _prompt_attn_body(rb_ref, q_ref, qi_ref, wit_ref, ki_ref, k_ref, vt_ref, att_ref,
                      s_s, mb_s, tab_s, tri_s, thr_s, need_s, m_s, acc_s, qpad_s, *, topk):
    b = pl.program_id(0)
    i = pl.program_id(1)
    n_pairs = i // 4 + 1
    ck = (KEY_CHUNK, Q_BLOCK)

    @pl.when((b == 0) & (i == 0))
    def _init_tables():
        r = lax.broadcasted_iota(I32, (KEY_CHUNK, KEY_CHUNK), 0)
        c = lax.broadcasted_iota(I32, (KEY_CHUNK, KEY_CHUNK), 1)
        tri_s[...] = jnp.where(c <= r, 1.0, 0.0).astype(MXU_DTYPE)

        def per_head(h, carry):
            for ti in range(4):
                tab_s[ti, h] = _bias_table(rb_ref, h, ti * Q_BLOCK, ck)
            return carry

        lax.fori_loop(0, N_HEADS, per_head, 0)

    row = lax.broadcasted_iota(I32, ck, 0)
    lane = lax.broadcasted_iota(I32, ck, 1)
    q_pos = i * Q_BLOCK + lane

    qi = qi_ref[...]
    qis = jnp.concatenate([qi[:, h * IDX_DIM:(h + 1) * IDX_DIM] for h in range(N_IDX_HEADS)], axis=0)
    wit = wit_ref[...]
    w_row = jnp.concatenate([wit[h:h + 1, :] for h in range(N_IDX_HEADS)], axis=1)

    def score_chunk(c, carry):
        s4 = jnp.maximum(_mm_nt(ki_ref[c], qis), 0.0) * w_row
        s = _tree_sum([s4[:, h * Q_BLOCK:(h + 1) * Q_BLOCK] for h in range(N_IDX_HEADS)])
        valid = (c * KEY_CHUNK + row) <= q_pos
        s_s[c] = jnp.where(valid, s, -jnp.inf)
        return carry

    _pair_loop(n_pairs, score_chunk, 0)

    def search(n_chunks):
        t = _kth_largest(lambda cand: _count_above(s_s, n_chunks, cand, False), topk,
                         jnp.full((1, Q_BLOCK), INT_MIN, I32))
        thr_s[0:1, :] = t
        need_s[0:1, :] = topk - _count_above(s_s, n_chunks, t, True)

    n_chunks_causal = (i + 2) // 2
    for n in range(1, s_s.shape[0] + 1):
        pl.when(n_chunks_causal == n)(functools.partial(search, n))
    t = thr_s[0:1, :]
    need = need_s[0:1, :]

    tri = tri_s[...]

    def mask_chunk(c, carry):
        sc = s_s[c]
        tie = (sc == t) & (sc > -jnp.inf)
        tie_f = jnp.where(tie, 1.0, 0.0)
        incl = _mm(tri, tie_f.astype(MXU_DTYPE))
        rank = carry + incl - tie_f
        sel = (sc > t) | (tie & (rank < need))
        mb_s[c] = jnp.where(sel, 0.0, -jnp.inf).astype(mb_s.dtype)
        return carry + incl[KEY_CHUNK - 1:KEY_CHUNK, :]

    _pair_loop(n_pairs, mask_chunk, jnp.zeros((1, Q_BLOCK), F32))

    q = q_ref[...]
    zeros = jnp.zeros((Q_BLOCK, HEAD_DIM), MXU_DTYPE)
    for h in range(N_HEADS):
        qh = q[:, h * HEAD_DIM:(h + 1) * HEAD_DIM]
        parts = [qh, zeros] if h < GROUP else [zeros, qh]
        qpad_s[h * Q_BLOCK:(h + 1) * Q_BLOCK, :] = jnp.concatenate(parts, axis=1)
    m_s[...] = jnp.full(m_s.shape, NEG_BIG, F32)
    acc_s[...] = jnp.zeros(acc_s.shape, F32)
    ones_rows = jnp.ones((PACKED_ROWS, 2 * KEY_CHUNK), MXU_DTYPE)

    def attend_pair(p, near):
        c0, c1 = 2 * p, 2 * p + 1
        keys = jnp.concatenate([k_ref[c0], k_ref[c1]], axis=0)
        logits = _mm_nt(keys, qpad_s[...])
        mb = jnp.concatenate([mb_s[c0], mb_s[c1]], axis=0)
        if near:
            t0 = jnp.clip(i - 2 * c0, 0, 3)
            t1 = jnp.clip(i - 2 * c1, 0, 3)
        for g in range(N_KV_HEADS):
            ps, alphas = [], []
            for hh in range(GROUP):
                h = g * GROUP + hh
                cols = slice(h * Q_BLOCK, (h + 1) * Q_BLOCK)
                lt = logits[:, cols]
                if near:
                    lt = lt + jnp.concatenate([tab_s[t0, h], tab_s[t1, h]], axis=0)
                lt = lt.astype(MXU_DTYPE) + mb
                m_old = m_s[0:1, cols]
                m_new = jnp.maximum(m_old, jnp.max(lt, axis=0, keepdims=True).astype(F32))
                alphas.append(jnp.exp(m_old - m_new))
                ps.append(jnp.exp(lt - m_new.astype(MXU_DTYPE)))
                m_s[0:1, cols] = m_new
            pg = jnp.concatenate(ps, axis=1)
            ag = jnp.concatenate(alphas, axis=1)
            rows = slice(g * HEAD_DIM, (g + 1) * HEAD_DIM)
            vt = jnp.concatenate([jnp.concatenate([vt_ref[c0, rows, :], vt_ref[c1, rows, :]], axis=1), ones_rows],
                                 axis=0)
            acc_s[g] = acc_s[g] * ag + _mm(vt, pg)

    def far_pair(p, carry):
        attend_pair(2 * p, False)
        attend_pair(2 * p + 1, False)
        return carry

    n_far = jnp.maximum(n_pairs - 2, 0)
    lax.fori_loop(0, n_far // 2, far_pair, 0)

    @pl.when(n_far % 2 == 1)
    def _odd_far_pair():
        attend_pair(n_far - 1, False)

    @pl.when(n_pairs >= 2)
    def _near_pairs():
        attend_pair(n_pairs - 2, True)
        attend_pair(n_pairs - 1, True)

    @pl.when(n_pairs == 1)
    def _only_pair():
        attend_pair(0, True)

    for h in range(N_HEADS):
        g, hh = divmod(h, GROUP)
        cols = slice(hh * Q_BLOCK, (hh + 1) * Q_BLOCK)
        o = acc_s[g][:HEAD_DIM, cols] / acc_s[g][HEAD_DIM:HEAD_DIM + 1, cols]
        att_ref[:, h * HEAD_DIM:(h + 1) * HEAD_DIM] = o.T.astype(att_ref.dtype)


def _prompt_attention(rel_bias, q, qi, wit, kibf, kbf, vtbf, batch, seq):
    nq = seq // Q_BLOCK
    nc = seq // KEY_CHUNK
    topk = float(min(TOPK_MAX, seq // 4))
    qrow = lambda n: pl.BlockSpec((Q_BLOCK, n), lambda b, i: (b * nq + i, 0))
    per_batch = lambda a, c: pl.BlockSpec((nc, a, c), lambda b, i: (b, 0, 0))
    return pl.pallas_call(
        functools.partial(_prompt_attn_body, topk=topk),
        grid=(batch, nq),
        in_specs=[
            pl.BlockSpec(memory_space=pltpu.SMEM),
            qrow(D_ATT), qrow(D_QI),
            pl.BlockSpec((SUBLANES, Q_BLOCK), lambda b, i: (0, b * nq + i)),
            per_batch(KEY_CHUNK, IDX_DIM), per_batch(KEY_CHUNK, D_KV), per_batch(D_KV, KEY_CHUNK),
        ],
        out_specs=qrow(D_ATT),
        out_shape=jax.ShapeDtypeStruct((batch * seq, D_ATT), MXU_DTYPE),
        scratch_shapes=[
            pltpu.VMEM((nc, KEY_CHUNK, Q_BLOCK), F32),
            pltpu.VMEM((nc, KEY_CHUNK, Q_BLOCK), MXU_DTYPE),
            pltpu.VMEM((4, N_HEADS, KEY_CHUNK, Q_BLOCK), F32),
            pltpu.VMEM((KEY_CHUNK, KEY_CHUNK), MXU_DTYPE),
            pltpu.VMEM((SUBLANES, Q_BLOCK), F32),
            pltpu.VMEM((SUBLANES, Q_BLOCK), F32),
            pltpu.VMEM((SUBLANES, N_HEADS * Q_BLOCK), F32),
            pltpu.VMEM((N_KV_HEADS, HEAD_DIM + PACKED_ROWS, GROUP * Q_BLOCK), F32),
            pltpu.VMEM((N_HEADS * Q_BLOCK, D_KV), MXU_DTYPE),
        ],
        compiler_params=_cparams("arbitrary", "arbitrary"),
        name="prompt_attn",
    )(rel_bias, q, qi, wit, kibf, kbf, vtbf)


def _sample_attn_body(pt_ref, rb_ref, q_ref, qi_ref, kiwi_ref, knew_ref, vnew_ref,
                      ck_hbm, cv_hbm, cki_hbm, att_ref,
                      ktbuf, vtbuf, kitbuf, sem, *, topk, n_pages, layer, t_new):
    b = pl.program_id(0)
    past = n_pages * PAGE_SIZE
    slot = b % 2

    def page_copies(seq, buf, p):
        phys = pt_ref[seq, p]
        cols = pl.ds(pl.multiple_of(p * PAGE_SIZE, PAGE_SIZE), PAGE_SIZE)
        return (pltpu.make_async_copy(ck_hbm.at[layer, phys], ktbuf.at[buf, :, cols], sem.at[0, buf]),
                pltpu.make_async_copy(cv_hbm.at[layer, phys], vtbuf.at[buf, :, cols], sem.at[1, buf]),
                pltpu.make_async_copy(cki_hbm.at[layer, phys], kitbuf.at[buf, :, cols], sem.at[2, buf]))

    def start_pages(seq, buf):
        def body(p, carry):
            for cp in page_copies(seq, buf, p):
                cp.start()
            return carry
        lax.fori_loop(0, n_pages, body, 0)

    def wait_pages(seq, buf):
        def body(p, carry):
            for cp in page_copies(seq, buf, p):
                cp.wait()
            return carry
        lax.fori_loop(0, n_pages, body, 0)

    @pl.when(b == 0)
    def _first_fetch():
        start_pages(0, 0)

    @pl.when(b + 1 < pl.num_programs(0))
    def _prefetch_next():
        start_pages(b + 1, 1 - slot)

    qi = qi_ref[...]
    qis = jnp.concatenate([qi[:, h * IDX_DIM:(h + 1) * IDX_DIM] for h in range(N_IDX_HEADS)], axis=0)
    kiwi = kiwi_ref[...]
    w_col = jnp.concatenate([kiwi[:, IDX_DIM + h:IDX_DIM + h + 1] for h in range(N_IDX_HEADS)], axis=0)
    q = q_ref[...]
    zeros = jnp.zeros((t_new, HEAD_DIM), MXU_DTYPE)
    qpad = jnp.concatenate(
        [jnp.concatenate([q[:, h * HEAD_DIM:(h + 1) * HEAD_DIM], zeros] if h < GROUP else
                         [zeros, q[:, h * HEAD_DIM:(h + 1) * HEAD_DIM]], axis=1)
         for h in range(N_HEADS)], axis=0)

    wait_pages(b, slot)

    def idx_score(qk):
        s4 = jnp.maximum(qk, 0.0) * w_col
        s = s4[0:t_new]
        for h in range(1, N_IDX_HEADS):
            s = s + s4[h * t_new:(h + 1) * t_new]
        return s

    s_p = idx_score(_mm(qis, kitbuf[slot].astype(MXU_DTYPE)))
    ki_new = kiwi[:, :IDX_DIM].astype(MXU_DTYPE)
    ki_new = jnp.concatenate([ki_new, jnp.zeros((LANES - t_new, IDX_DIM), MXU_DTYPE)], axis=0)
    rown = lax.broadcasted_iota(I32, (t_new, LANES), 0)
    lanen = lax.broadcasted_iota(I32, (t_new, LANES), 1)
    valid_n = lanen <= rown
    s_n = jnp.where(valid_n, idx_score(_mm_nt(qis, ki_new)), -jnp.inf)

    def count(cand, strict):
        cmp = (lambda x: x > cand) if strict else (lambda x: x >= cand)
        return (jnp.sum(jnp.where(cmp(s_p), 1.0, 0.0), axis=1, keepdims=True)
                + jnp.sum(jnp.where(cmp(s_n), 1.0, 0.0), axis=1, keepdims=True))

    t = _kth_largest(lambda cand: count(cand, False), topk, jnp.full((t_new, 1), INT_MIN, I32),
                     two_bits_per_step=True)
    need = topk - count(t, True)

    r = lax.broadcasted_iota(I32, (KEY_CHUNK, KEY_CHUNK), 0)
    c = lax.broadcasted_iota(I32, (KEY_CHUNK, KEY_CHUNK), 1)
    tri = jnp.where(r <= c, 1.0, 0.0).astype(MXU_DTYPE)
    tie_p = s_p == t
    carry = jnp.zeros((t_new, 1), F32)
    mask_parts = []
    for ch in range(past // KEY_CHUNK):
        cols = slice(ch * KEY_CHUNK, (ch + 1) * KEY_CHUNK)
        tie_c = tie_p[:, cols]
        tie_f = jnp.where(tie_c, 1.0, 0.0)
        incl = _mm(tie_f.astype(MXU_DTYPE), tri)
        sel = (s_p[:, cols] > t) | (tie_c & ((carry + incl - tie_f) < need))
        mask_parts.append(jnp.where(sel, 0.0, -jnp.inf))
        carry = carry + incl[:, KEY_CHUNK - 1:KEY_CHUNK]
    mb_p = jnp.concatenate(mask_parts, axis=1)
    tie_n = (s_n == t) & valid_n
    tie_nf = jnp.where(tie_n, 1.0, 0.0)
    incl_n = _mm(tie_nf.astype(MXU_DTYPE), tri[:LANES, :LANES])
    sel_n = (s_n > t) | (tie_n & ((carry + incl_n - tie_nf) < need))
    mb_n = jnp.where(sel_n & valid_n, 0.0, -jnp.inf)

    lg_p = _mm(qpad, ktbuf[slot].astype(MXU_DTYPE))
    k_new = jnp.concatenate([knew_ref[...].astype(MXU_DTYPE),
                             jnp.zeros((LANES - t_new, D_KV), MXU_DTYPE)], axis=0)
    lg_n = _mm_nt(qpad, k_new)
    near = slice(past - LANES, past)
    lp_rows, ln_rows = [], []
    for h in range(N_HEADS):
        rows = slice(h * t_new, (h + 1) * t_new)
        far = rb_ref[NUM_BUCKETS - 1, h]

        def bias(d):
            val = jnp.full(d.shape, rb_ref[0, h] - far, F32)
            for bk in range(1, NUM_BUCKETS - 1):
                val = jnp.where(d >= BUCKET_LO[bk], rb_ref[bk, h] - far, val)
            return jnp.where(d >= BUCKET_LO[NUM_BUCKETS - 1], 0.0, val)

        lp = lg_p[rows] + mb_p
        lp_near = lp[:, near] + bias(rown + (LANES - lanen))
        lp_rows.append(jnp.concatenate([lp[:, :past - LANES], lp_near], axis=1))
        ln_rows.append(lg_n[rows] + mb_n + bias(rown - lanen))
    lp_all = jnp.concatenate(lp_rows, axis=0)
    ln_all = jnp.concatenate(ln_rows, axis=0)
    m = jnp.maximum(jnp.max(lp_all, axis=1, keepdims=True), jnp.max(ln_all, axis=1, keepdims=True))
    p_p = jnp.exp(lp_all - m)
    p_n = jnp.exp(ln_all - m)
    denom = jnp.sum(p_p, axis=1, keepdims=True) + jnp.sum(p_n, axis=1, keepdims=True)
    v_new = jnp.concatenate([vnew_ref[...].astype(MXU_DTYPE),
                             jnp.zeros((LANES - t_new, D_KV), MXU_DTYPE)], axis=0)
    o = (_mm_nt(p_p.astype(MXU_DTYPE), vtbuf[slot].astype(MXU_DTYPE))
         + _mm(p_n.astype(MXU_DTYPE), v_new)) / denom
    for h in range(N_HEADS):
        g = h // GROUP
        att_ref[:, h * HEAD_DIM:(h + 1) * HEAD_DIM] = (
            o[h * t_new:(h + 1) * t_new, g * HEAD_DIM:(g + 1) * HEAD_DIM].astype(att_ref.dtype))


def _sample_attention(page_table, rel_bias, q, qi, kiwi, k_new, v_new, cache_k, cache_v, cache_kidx,
                      layer, batch, t_new):
    n_pages = page_table.shape[1]
    past = n_pages * PAGE_SIZE
    topk = float(min(TOPK_MAX, (past + t_new) // 4))
    depth, n_pool = cache_k.shape[:2]
    ck = cache_k.transpose(0, 1, 3, 4, 2).reshape(depth, n_pool, D_KV, PAGE_SIZE)
    cv = cache_v.transpose(0, 1, 3, 4, 2).reshape(depth, n_pool, D_KV, PAGE_SIZE)
    cki = cache_kidx.transpose(0, 1, 3, 2)
    qrow = lambda n: pl.BlockSpec((t_new, n), lambda b, pt: (b, 0))
    hbm = pl.BlockSpec(memory_space=pl.ANY)
    grid_spec = pltpu.PrefetchScalarGridSpec(
        num_scalar_prefetch=1,
        grid=(batch,),
        in_specs=[pl.BlockSpec(memory_space=pltpu.SMEM),
                  qrow(D_ATT), qrow(D_QI), qrow(LANES), qrow(D_KV), qrow(D_KV), hbm, hbm, hbm],
        out_specs=qrow(D_ATT),
        scratch_shapes=[
            pltpu.VMEM((2, D_KV, past), F32),
            pltpu.VMEM((2, D_KV, past), F32),
            pltpu.VMEM((2, IDX_DIM, past), F32),
            pltpu.SemaphoreType.DMA((3, 2)),
        ],
    )
    return pl.pallas_call(
        functools.partial(_sample_attn_body, topk=topk, n_pages=n_pages, layer=layer, t_new=t_new),
        grid_spec=grid_spec,
        out_shape=jax.ShapeDtypeStruct((batch * t_new, D_ATT), MXU_DTYPE),
        compiler_params=_cparams("arbitrary"),
        name="sample_attn",
    )(page_table, rel_bias, q, qi, kiwi, k_new, v_new, ck, cv, cki)


def _rec_body(rec_ref, hist_ref, h0_ref, cw_ref, cb_ref, wa_ref, ba_ref, wx_ref, bx_ref, lam_ref,
              out_ref, hlast_ref, cstate_ref, hcar, xprev):
    tt = rec_ref.shape[0]

    @pl.when(pl.program_id(1) == 0)
    def _load_state():
        hcar[...] = h0_ref[0]
        xprev[...] = hist_ref[0]

    rx = rec_ref[:, :D_REC]
    rg = rec_ref[:, D_REC:]
    ext = jnp.concatenate([xprev[...], rx], axis=0)
    xc = cb_ref[...]
    for j in range(REC_CONV_W - 1):
        xc = xc + pltpu.roll(ext, REC_CONV_W - 1 - j, 0)[SUBLANES:] * cw_ref[j:j + 1, :]
    xc = xc + rx * cw_ref[REC_CONV_W - 1:REC_CONV_W, :]
    xcb = xc.astype(MXU_DTYPE)
    r = _sigmoid(_mm(xcb, wa_ref[...]) + ba_ref[...])
    gi = _sigmoid(_mm(xcb, wx_ref[...]) + bx_ref[...])
    nl = -lam_ref[...]
    softplus = jnp.maximum(nl, 0.0) + jnp.log1p(jnp.exp(-jnp.abs(nl)))
    log_a = -LRU_C * r * softplus
    a = jnp.exp(log_a)
    u = jnp.sqrt(-jnp.tanh(log_a) * (a * a + 1.0)) * (gi * xc)
    row = lax.broadcasted_iota(I32, (tt, D_REC), 0)
    s = 1
    while s < tt:
        keep = row >= s
        u = jnp.where(keep, u + a * pltpu.roll(u, s, 0), u)
        a = jnp.where(keep, a * pltpu.roll(a, s, 0), a)
        s *= 2
    hs = u + a * hcar[0:1, :]
    hcar[...] = jnp.broadcast_to(hs[tt - 1:tt, :], hcar.shape)
    xprev[...] = rx[tt - SUBLANES:, :]
    out_ref[...] = (_gelu(rg) * hs).astype(out_ref.dtype)
    hlast_ref[0] = hs[tt - SUBLANES:, :]
    cstate_ref[0] = rx[tt - SUBLANES:, :]


def _rec_branch(rec, hist8, h08, cw, cb, wa, ba, wx, bx, lam, batch, t_len, tt):
    nt = t_len // tt
    state = pl.BlockSpec((1, SUBLANES, D_REC), lambda b, t: (b, 0, 0))
    vec = _const_spec((1, D_REC))
    return pl.pallas_call(
        _rec_body,
        grid=(batch, nt),
        in_specs=[pl.BlockSpec((tt, 2 * D_REC), lambda b, t: (b * nt + t, 0)), state, state,
                  _const_spec((REC_CONV_W, D_REC)), vec, _const_spec((D_REC, D_REC)), vec,
                  _const_spec((D_REC, D_REC)), vec, vec],
        out_specs=(pl.BlockSpec((tt, D_REC), lambda b, t: (b * nt + t, 0)), state, state),
        out_shape=(jax.ShapeDtypeStruct((batch * t_len, D_REC), MXU_DTYPE),
                   jax.ShapeDtypeStruct((batch, SUBLANES, D_REC), F32),
                   jax.ShapeDtypeStruct((batch, SUBLANES, D_REC), F32)),
        scratch_shapes=[pltpu.VMEM((SUBLANES, D_REC), F32), pltpu.VMEM((SUBLANES, D_REC), F32)],
        compiler_params=_cparams("arbitrary", "arbitrary"),
        name="rec_branch",
    )(rec, hist8, h08, cw, cb, wa, ba, wx, bx, lam)


def _gmlp_body(gm_ref, wmix_ref, bias_ref, lng_ref, lnb_ref, out_ref, vn_ref):
    rows = wmix_ref.shape[1]
    n_sub = gm_ref.shape[0] // rows
    head_of_lane = lax.broadcasted_iota(I32, (rows, D_GM), 1) // GM_HEAD_DIM
    for sb in range(n_sub):
        sl = slice(sb * rows, (sb + 1) * rows)
        u = _gelu(gm_ref[sl, :D_GM])
        gv = _gelu(gm_ref[sl, D_GM:])
        xc = gv - jnp.mean(gv, axis=-1, keepdims=True)
        var = jnp.mean(xc * xc, axis=-1, keepdims=True)
        vn = xc * lax.rsqrt(var + EPS) * lng_ref[...] + lnb_ref[...]
        vn_ref[sl, :] = vn
        vnb = vn.astype(MXU_DTYPE)
        mix = jnp.zeros((rows, D_GM), F32)
        for g in range(N_GM_HEADS):
            mix = jnp.where(head_of_lane == g, _mm(wmix_ref[g], vnb), mix)
        out_ref[sl, :] = (u * (mix + bias_ref[...])).astype(out_ref.dtype)


def _gmlp(gm, wmix, bias, lng, lnb, tm):
    m = gm.shape[0]
    rows = wmix.shape[1]
    return pl.pallas_call(
        _gmlp_body,
        grid=(m // tm,),
        in_specs=[pl.BlockSpec((tm, 2 * D_GM), lambda i: (i, 0)),
                  _const_spec((N_GM_HEADS, rows, rows)), _const_spec((rows, D_GM)),
                  _const_spec((1, D_GM)), _const_spec((1, D_GM))],
        out_specs=(pl.BlockSpec((tm, D_GM), lambda i: (i, 0)), pl.BlockSpec((tm, D_GM), lambda i: (i, 0))),
        out_shape=(jax.ShapeDtypeStruct((m, D_GM), MXU_DTYPE), jax.ShapeDtypeStruct((m, D_GM), F32)),
        compiler_params=_cparams("arbitrary"),
        name="gmlp",
    )(gm, wmix, bias, lng, lnb)


def _out_proj_body(x_ref, att_ref, rec_ref, gm_ref, w_ref, g_ref, x1_ref, h2_ref):
    y = (_mm(att_ref[...], w_ref[0:D_ATT, :])
         + _mm(rec_ref[...], w_ref[D_ATT:D_ATT + D_REC, :])
         + _mm(gm_ref[...], w_ref[D_ATT + D_REC:, :]))
    x1 = x_ref[...] + y
    x1_ref[...] = x1
    h2_ref[...] = _rmsnorm(x1, g_ref[...]).astype(h2_ref.dtype)


def _out_proj(x, att, rec, gm, w, g, tm):
    m = x.shape[0]
    row = lambda n: pl.BlockSpec((tm, n), lambda i: (i, 0))
    return pl.pallas_call(
        _out_proj_body,
        grid=(m // tm,),
        in_specs=[row(D_MODEL), row(D_ATT), row(D_REC), row(D_GM),
                  _const_spec((D_MODEL, D_MODEL)), _const_spec((1, D_MODEL))],
        out_specs=(row(D_MODEL), row(D_MODEL)),
        out_shape=(jax.ShapeDtypeStruct((m, D_MODEL), F32), jax.ShapeDtypeStruct((m, D_MODEL), MXU_DTYPE)),
        compiler_params=_cparams("arbitrary"),
        name="out_proj",
    )(x, att, rec, gm, w, g)


FF_TILE = 256
HALO = 16


def _ffn_finish(x1, acc, gf_ref, out_ref, final):
    x2 = x1 + acc
    out_ref[...] = _rmsnorm(x2, gf_ref[...]) if final else x2


def _ffn_prompt_body(x_ref, xh_ref, att_ref, atth_ref, rec_ref, rech_ref, gm_ref, gmh_ref, wo_ref, g2_ref,
                     hist_ref, wup_ref, cw_ref, cb_ref, wdn_ref, gf_ref,
                     out_ref, upst_ref, act_s, *, tiles_per_seq, final):
    tm = x_ref.shape[0]
    first = (pl.program_id(0) % tiles_per_seq) == 0
    ext = lambda halo, main: jnp.concatenate([halo[...], main[...]], axis=0)
    x1_ext = ext(xh_ref, x_ref) + (_mm(ext(atth_ref, att_ref), wo_ref[0:D_ATT, :])
                                   + _mm(ext(rech_ref, rec_ref), wo_ref[D_ATT:D_ATT + D_REC, :])
                                   + _mm(ext(gmh_ref, gm_ref), wo_ref[D_ATT + D_REC:, :]))
    h_ext = _rmsnorm(x1_ext, g2_ref[...]).astype(MXU_DTYPE)
    for j in range(D_FF // FF_TILE):
        halves = []
        for base in (0, D_FF):
            cols = slice(base + j * FF_TILE, base + (j + 1) * FF_TILE)
            up = _mm(h_ext, wup_ref[:, cols])
            up_m = up[HALO:]
            upst_ref[0, :, cols] = up_m[tm - HALO:, :]
            ext = jnp.concatenate([jnp.where(first, hist_ref[0, :, cols], up[:HALO]), up_m], axis=0)
            uc = cb_ref[:, cols]
            for jj in range(FFN_CONV_W - 1):
                uc = uc + pltpu.roll(ext, FFN_CONV_W - 1 - jj, 0)[HALO:] * cw_ref[jj:jj + 1, cols]
            halves.append(uc + up_m * cw_ref[FFN_CONV_W - 1:FFN_CONV_W, cols])
        act_s[:, j * FF_TILE:(j + 1) * FF_TILE] = (_gelu(halves[0]) * halves[1]).astype(MXU_DTYPE)
    _ffn_finish(x1_ext[HALO:], _mm(act_s[...], wdn_ref[...]), gf_ref, out_ref, final)


def _ffn_prompt(x, att, rec, gm, wo, g2, hist16, wup, cw, cb, wdn, gf, batch, seq, tm, final):
    m = x.shape[0]
    tps = seq // tm
    row = lambda n: pl.BlockSpec((tm, n), lambda i: (i, 0))
    halo = lambda n: pl.BlockSpec((HALO, n), lambda i: (jnp.maximum(i * (tm // HALO) - 1, 0), 0))
    state = pl.BlockSpec((1, HALO, 2 * D_FF), lambda i: (i // tps, 0, 0))
    return pl.pallas_call(
        functools.partial(_ffn_prompt_body, tiles_per_seq=tps, final=final),
        grid=(m // tm,),
        in_specs=[row(D_MODEL), halo(D_MODEL), row(D_ATT), halo(D_ATT), row(D_REC), halo(D_REC),
                  row(D_GM), halo(D_GM), _const_spec((D_MODEL, D_MODEL)), _const_spec((1, D_MODEL)), state,
                  _const_spec((D_MODEL, 2 * D_FF)), _const_spec((FFN_CONV_W, 2 * D_FF)),
                  _const_spec((1, 2 * D_FF)), _const_spec((D_FF, D_MODEL)), _const_spec((1, D_MODEL))],
        out_specs=(row(D_MODEL), state),
        out_shape=(jax.ShapeDtypeStruct((m, D_MODEL), F32),
                   jax.ShapeDtypeStruct((batch, HALO, 2 * D_FF), F32)),
        scratch_shapes=[pltpu.VMEM((tm, D_FF), MXU_DTYPE)],
        compiler_params=_cparams("arbitrary"),
        name="ffn_prompt",
    )(x, x, att, att, rec, rec, gm, gm, wo, g2, hist16, wup, cw, cb, wdn, gf)


def _ffn_sample_body(h_ref, x1_ref, p1_ref, p2_ref, wup_ref, cw_ref, cb_ref, wdn_ref, gf_ref,
                     out_ref, up_ref, *, t_new, final):
    tm = h_ref.shape[0]
    hm = h_ref[...]
    pos = lax.broadcasted_iota(I32, (tm, FF_TILE), 0) % t_new
    acc = jnp.zeros((tm, D_MODEL), F32)
    for j in range(D_FF // FF_TILE):
        halves = []
        for base in (0, D_FF):
            cols = slice(base + j * FF_TILE, base + (j + 1) * FF_TILE)
            up = _mm(hm, wup_ref[:, cols])
            up_ref[:, cols] = up
            tap2 = jnp.where(pos < 2, p2_ref[:, cols], pltpu.roll(up, 2, 0))
            tap1 = jnp.where(pos < 1, p1_ref[:, cols], pltpu.roll(up, 1, 0))
            halves.append(cb_ref[:, cols] + tap2 * cw_ref[0:1, cols] + tap1 * cw_ref[1:2, cols]
                          + up * cw_ref[2:3, cols])
        act = (_gelu(halves[0]) * halves[1]).astype(MXU_DTYPE)
        acc = acc + _mm(act, wdn_ref[j * FF_TILE:(j + 1) * FF_TILE, :])
    _ffn_finish(x1_ref[...], acc, gf_ref, out_ref, final)


def _ffn_sample(h2, x1, p1, p2, wup, cw, cb, wdn, gf, t_new, final):
    m = h2.shape[0]
    full = lambda n: _const_spec((m, n))
    return pl.pallas_call(
        functools.partial(_ffn_sample_body, t_new=t_new, final=final),
        grid=(1,),
        in_specs=[full(D_MODEL), full(D_MODEL), full(2 * D_FF), full(2 * D_FF),
                  _const_spec((D_MODEL, 2 * D_FF)), _const_spec((FFN_CONV_W, 2 * D_FF)),
                  _const_spec((1, 2 * D_FF)), _const_spec((D_FF, D_MODEL)), _const_spec((1, D_MODEL))],
        out_specs=(full(D_MODEL), full(2 * D_FF)),
        out_shape=(jax.ShapeDtypeStruct((m, D_MODEL), F32), jax.ShapeDtypeStruct((m, 2 * D_FF), F32)),
        compiler_params=_cparams("arbitrary"),
        name="ffn_sample",
    )(h2, x1, p1, p2, wup, cw, cb, wdn, gf)


def _block_diag(w):
    n, blk, _ = w.shape
    out = jnp.zeros((n * blk, n * blk), w.dtype)
    for i in range(n):
        out = out.at[i * blk:(i + 1) * blk, i * blk:(i + 1) * blk].set(w[i])
    return out


def _pad_rows_front(a, rows):
    pad = rows - a.shape[1]
    return jnp.pad(a, ((0, 0), (pad, 0), (0, 0)))


def _layer_weights(l, g_mix, w_in, rec_conv_w, rec_conv_b, lru_wa, lru_ba, lru_wx, lru_bx, lru_lam,
                   gm_ln_g, gm_ln_b, gm_ws, gm_bs, w_out, g_ffn, w_up, ffn_conv_w, ffn_conv_b, w_down):
    w = w_in[l]
    c_ki = D_ATT + 2 * D_KV + D_QI
    c_rx = c_ki + IDX_DIM + N_IDX_HEADS
    w_r = jnp.concatenate([w[:, :c_ki], w[:, c_rx:], w[:, c_ki:c_rx],
                           jnp.zeros((D_MODEL, LANES - IDX_DIM - N_IDX_HEADS), w.dtype)], axis=1)
    tril = jnp.tril(jnp.ones((CHUNK, CHUNK), gm_ws.dtype))
    row2 = lambda a: a.reshape(1, -1)
    return dict(
        g_mix=row2(g_mix[l]), w_in=w_r.astype(MXU_DTYPE),
        rec_cw=rec_conv_w[l], rec_cb=row2(rec_conv_b[l]),
        wa=_block_diag(lru_wa[l]).astype(MXU_DTYPE), ba=row2(lru_ba[l]),
        wx=_block_diag(lru_wx[l]).astype(MXU_DTYPE), bx=row2(lru_bx[l]), lam=row2(lru_lam[l]),
        ln_g=row2(gm_ln_g[l]), ln_b=row2(gm_ln_b[l]), gm_w=gm_ws[l] * tril, gm_b=gm_bs[l],
        w_out=w_out[l].astype(MXU_DTYPE), g_ffn=row2(g_ffn[l]),
        w_up=w_up[l].astype(MXU_DTYPE), ffn_cw=ffn_conv_w[l], ffn_cb=row2(ffn_conv_b[l]),
        w_down=w_down[l].astype(MXU_DTYPE))


def _gmlp_mix_weights(p, c, n_seq):
    w = p['gm_w'][:, :c, :c]
    if n_seq > 1:
        w = jnp.einsum('ab,gts->gatbs', jnp.eye(n_seq, dtype=w.dtype), w).reshape(
            N_GM_HEADS, n_seq * c, n_seq * c)
    bias = jnp.repeat(p['gm_b'][:, :c].T, GM_HEAD_DIM, axis=1)
    return w.astype(MXU_DTYPE), jnp.tile(bias, (n_seq, 1))


def _prompt_layer(x, p, rel_bias, g_final, batch, seq, final):
    (q, kt, vt, kit, kbf, vtbf, kibf, qi, _, wit, rec, gm) = _in_proj(x, p['g_mix'], p['w_in'], PROJ_TILE, seq)
    att = _prompt_attention(rel_bias, q, qi, wit, kibf, kbf, vtbf, batch, seq)
    zeros8 = jnp.zeros((batch, SUBLANES, D_REC), F32)
    rec_out, hlast, cstate = _rec_branch(rec, zeros8, zeros8, p['rec_cw'], p['rec_cb'], p['wa'], p['ba'],
                                         p['wx'], p['bx'], p['lam'], batch, seq, REC_TILE)
    c = min(CHUNK, seq)
    wmix, bias = _gmlp_mix_weights(p, c, 1)
    gm_out, _ = _gmlp(gm, wmix, bias, p['ln_g'], p['ln_b'], PROJ_TILE)
    hist16 = jnp.zeros((batch, HALO, 2 * D_FF), F32)
    x2, upst = _ffn_prompt(x, att, rec_out, gm_out, p['w_out'], p['g_ffn'], hist16, p['w_up'], p['ffn_cw'],
                           p['ffn_cb'], p['w_down'], g_final, batch, seq, FFN_ROW_TILE, final)
    heads_last = lambda a: a.reshape(batch, N_KV_HEADS, HEAD_DIM, seq).transpose(0, 3, 1, 2)
    state = (heads_last(kt), heads_last(vt), kit.transpose(0, 2, 1), hlast[:, SUBLANES - 1, :],
             cstate[:, SUBLANES - (REC_CONV_W - 1):, :], upst[:, HALO - (FFN_CONV_W - 1):, :])
    return x2, state


def _sample_layer(x, p, rel_bias, g_final, l, batch, t_new, final, page_table, cache_k, cache_v, cache_kidx,
                  h0, rec_hist, ffn_hist):
    m = batch * t_new
    (q, k, v, ki, _, _, _, qi, kiwi, _, rec, gm) = _in_proj(x, p['g_mix'], p['w_in'], m)
    att = _sample_attention(page_table, rel_bias, q, qi, kiwi, k, v, cache_k, cache_v, cache_kidx,
                            l, batch, t_new)
    hist8 = _pad_rows_front(rec_hist, SUBLANES)
    h08 = jnp.broadcast_to(h0[:, None, :], (batch, SUBLANES, D_REC))
    rec_out, hlast, cstate = _rec_branch(rec, hist8, h08, p['rec_cw'], p['rec_cb'], p['wa'], p['ba'],
                                         p['wx'], p['bx'], p['lam'], batch, t_new, t_new)
    c = min(CHUNK, t_new)
    wmix, bias = _gmlp_mix_weights(p, c, m // c)
    gm_out, vn = _gmlp(gm, wmix, bias, p['ln_g'], p['ln_b'], m)
    x1, h2 = _out_proj(x, att, rec_out, gm_out, p['w_out'], p['g_ffn'], m)
    zrow = jnp.zeros((batch, t_new - 2, 2 * D_FF), F32)
    p2 = jnp.concatenate([ffn_hist, zrow], axis=1).reshape(m, 2 * D_FF)
    p1 = jnp.concatenate([ffn_hist[:, 1:], zrow, zrow[:, :1]], axis=1).reshape(m, 2 * D_FF)
    x2, up = _ffn_sample(h2, x1, p1, p2, p['w_up'], p['ffn_cw'], p['ffn_cb'], p['w_down'], g_final,
                         t_new, final)
    state = (k.reshape(batch, t_new, N_KV_HEADS, HEAD_DIM), v.reshape(batch, t_new, N_KV_HEADS, HEAD_DIM),
             ki.reshape(batch, t_new, IDX_DIM), hlast[:, SUBLANES - 1, :],
             cstate[:, SUBLANES - (REC_CONV_W - 1):, :],
             up.reshape(batch, t_new, 2 * D_FF)[:, t_new - (FFN_CONV_W - 1):, :],
             vn.reshape(batch, t_new, D_GM))
    return x2, state


def kernel(x_prompt, x_sample, cache_k, cache_v, cache_kidx, page_table, state_lru_h, state_conv_rec,
           state_conv_ffn, rel_bias, g_mix, w_in, rec_conv_w, rec_conv_b, lru_wa, lru_ba, lru_wx, lru_bx,
           lru_lam, gm_ln_g, gm_ln_b, gm_ws, gm_bs, w_out, g_ffn, w_up, ffn_conv_w, ffn_conv_b, w_down,
           g_final):
    batch, seq, _ = x_prompt.shape
    dec_batch, t_new, _ = x_sample.shape
    depth = w_in.shape[0]
    assert seq % PROJ_TILE == 0 and seq % FFN_ROW_TILE == 0 and seq % (2 * KEY_CHUNK) == 0
    assert t_new == SUBLANES and (dec_batch * t_new) % KEY_CHUNK == 0
    xp = x_prompt.reshape(batch * seq, D_MODEL)
    xs = x_sample.reshape(dec_batch * t_new, D_MODEL)
    gf = g_final.reshape(1, D_MODEL)
    p_states, s_states = [], []
    for l in range(depth):
        p = _layer_weights(l, g_mix, w_in, rec_conv_w, rec_conv_b, lru_wa, lru_ba, lru_wx, lru_bx, lru_lam,
                           gm_ln_g, gm_ln_b, gm_ws, gm_bs, w_out, g_ffn, w_up, ffn_conv_w, ffn_conv_b, w_down)
        final = l == depth - 1
        xp, st = _prompt_layer(xp, p, rel_bias, gf, batch, seq, final)
        p_states.append(st)
        xs, st = _sample_layer(xs, p, rel_bias, gf, l, dec_batch, t_new, final, page_table, cache_k, cache_v,
                               cache_kidx, state_lru_h[l], state_conv_rec[l], state_conv_ffn[l])
        s_states.append(st)
    stack = lambda states, i: jnp.stack([s[i] for s in states])
    return ((xp.reshape(batch, seq, D_MODEL), xs.reshape(dec_batch, t_new, D_MODEL))
            + tuple(stack(p_states, i) for i in range(6))
            + tuple(stack(s_states, i) for i in range(7)))
```
